```python
import jax, jax.numpy as jnp
from jax import lax
import numpy as np

D_MODEL = 1024
BATCH = 4
SEQ = 4096
DEPTH = 2

CHUNK = 64
N_MIXERS = 2
HEAD_DIM = 64
N_HEADS = D_MODEL // HEAD_DIM
Q_BLOCK = 128
FOX_IN = 4 * D_MODEL + N_HEADS
D_FF = -(-8 * D_MODEL // (3 * 256)) * 256
DECAY_LORA = 64
AAA_LORA = 64
GATE_LORA = 128
N_FOX = (DEPTH + N_MIXERS - 1) // N_MIXERS
N_RWKV = DEPTH // N_MIXERS
DEEPNORM_ALPHA = (2 * DEPTH) ** 0.25
DEEPNORM_BETA = (8 * DEPTH) ** -0.25
LN_EPS = 1e-5
QK_EPS = 1e-6
GN_EPS = HEAD_DIM * 1e-5
FORGET_BIAS_INIT = 2.0

kernel_name = "fox_rwkv7_deepnorm_adaln_trunk"


def layer_norm(x, g, b):
    xf = x.astype(jnp.float32)
    mu = jnp.mean(xf, axis=-1, keepdims=True)
    var = jnp.mean(jnp.square(xf - mu), axis=-1, keepdims=True)
    return ((xf - mu) * lax.rsqrt(var + LN_EPS) * g + b).astype(x.dtype)


def rms_norm(x, g):
    xf = x.astype(jnp.float32)
    return (xf * lax.rsqrt(jnp.mean(jnp.square(xf), axis=-1, keepdims=True) + QK_EPS) * g).astype(x.dtype)


def fox_block_attention(q, k, v, F):
    T = q.shape[2]
    scale = HEAD_DIM ** -0.5
    outs = []
    for blk in range(T // Q_BLOCK):
        q0 = blk * Q_BLOCK
        L = q0 + Q_BLOCK
        s = jnp.einsum('bhqd,bhkd->bhqk', q[:, :, q0:L], k[:, :, :L]).astype(jnp.float32) * scale
        s = s + F[:, :, q0:L, None] - F[:, :, None, :L]
        causal = jnp.arange(L)[None, :] <= (q0 + jnp.arange(Q_BLOCK))[:, None]
        p = jax.nn.softmax(jnp.where(causal, s, -jnp.inf), axis=-1)
        outs.append(jnp.einsum('bhqk,bhkd->bhqd', p.astype(v.dtype), v[:, :, :L]))
    return jnp.concatenate(outs, axis=2)


def fox_mixer(h, w_in, b_f, q_g, k_g, w_o):
    B, T, D = h.shape
    proj = h @ w_in
    q, k, v, f_logit, o = jnp.split(proj, [D, 2 * D, 3 * D, 3 * D + N_HEADS], axis=-1)
    heads = lambda t: t.reshape(B, T, N_HEADS, HEAD_DIM).transpose(0, 2, 1, 3)
    q = rms_norm(heads(q), q_g)
    k = rms_norm(heads(k), k_g)
    v = heads(v)
    log_f = jax.nn.log_sigmoid(f_logit.astype(jnp.float32) + b_f)
    F = jnp.cumsum(log_f, axis=1).transpose(0, 2, 1)
    y = fox_block_attention(q, k, v, F)
    y = y.transpose(0, 2, 1, 3).reshape(B, T, D)
    return (y * jax.nn.sigmoid(o)) @ w_o


def rwkv7_step(S, inp):
    r, w, k, v, kk, a = inp
    sa = jnp.einsum('bhvk,bhk->bhv', S, -kk)
    S = S * w[:, :, None, :] + sa[..., None] * (kk * a)[:, :, None, :] + v[..., None] * k[:, :, None, :]
    return S, jnp.einsum('bhvk,bhk->bhv', S, r)


def rwkv7_mixer(h, mu, w_rkv, w0, w1, w2, a0, a1, a2, g1, g2, k_k, k_a, r_k, lnx_g, lnx_b, w_o):
    B, T, D = h.shape
    f32 = jnp.float32
    dx = jnp.pad(h, ((0, 0), (1, 0), (0, 0)))[:, :-1] - h
    xs = h[:, :, None, :] + dx[:, :, None, :] * mu
    rkv = jnp.einsum('btnd,nde->btne', xs[:, :, :3], w_rkv).astype(f32)
    r, k, v = rkv[:, :, 0], rkv[:, :, 1], rkv[:, :, 2]
    xw, xa, xg = xs[:, :, 3], xs[:, :, 4], xs[:, :, 5]
    w_log = -jax.nn.softplus(-(w0 + jnp.tanh(xw @ w1) @ w2).astype(f32)) - 0.5
    decay = jnp.exp(-jnp.exp(w_log))
    a = jax.nn.sigmoid((a0 + (xa @ a1) @ a2).astype(f32))
    g = (jax.nn.sigmoid(xg @ g1) @ g2).astype(f32)
    kk = k * k_k
    k = k * (1.0 + (a - 1.0) * k_a)
    heads = lambda t: t.reshape(B, T, N_HEADS, HEAD_DIM)
    r, k, v, decay, a, kk = heads(r), heads(k), heads(v), heads(decay), heads(a), heads(kk)
    kk = kk / jnp.maximum(jnp.linalg.norm(kk, axis=-1, keepdims=True), 1e-12)
    tm = lambda t: jnp.moveaxis(t, 1, 0)
    S0 = jnp.zeros((B, N_HEADS, HEAD_DIM, HEAD_DIM), f32)
    _, y = lax.scan(rwkv7_step, S0, (tm(r), tm(decay), tm(k), tm(v), tm(kk), tm(a)))
    y = jnp.moveaxis(y, 0, 1)
    mean = jnp.mean(y, axis=-1, keepdims=True)
    var = jnp.mean(jnp.square(y - mean), axis=-1, keepdims=True)
    y = ((y - mean) * lax.rsqrt(var + GN_EPS)).reshape(B, T, D) * lnx_g + lnx_b
    bonus = jnp.sum(r * k * r_k, axis=-1, keepdims=True) * v
    y = (y + bonus.reshape(B, T, D)) * g
    return y.astype(h.dtype) @ w_o


def swiglu(h, w_in, w_out):
    gate, up = jnp.split(h @ w_in, 2, axis=-1)
    return (jax.nn.silu(gate) * up) @ w_out


def setup_inputs(seed: int = 0) -> dict:
    key = jax.random.key(seed)
    ks = iter(jax.random.split(key, 40))
    f32 = jnp.float32
    nrm = lambda shape, s: jax.random.normal(next(ks), shape, f32) * s
    D = D_MODEL
    fan = D ** -0.5
    fox_col_scale = jnp.ones((FOX_IN,), f32).at[2 * D:3 * D].set(DEEPNORM_BETA)
    rkv_scale = jnp.array([1.0, 1.0, DEEPNORM_BETA], f32).reshape(1, 3, 1, 1)
    return {
        "x": nrm((BATCH, SEQ, D), 1.0),
        "c": nrm((BATCH, D), 1.0),
        "ada_w": nrm((DEPTH, D, 6 * D), 0.5 * fan),
        "ada_b": nrm((DEPTH, 6 * D), 0.02),
        "ln_g": 1.0 + nrm((DEPTH, 2, D), 0.02),
        "ln_b": nrm((DEPTH, 2, D), 0.02),
        "ffn_w_in": nrm((DEPTH, D, 2 * D_FF), fan),
        "ffn_w_out": nrm((DEPTH, D_FF, D), D_FF ** -0.5 * DEEPNORM_BETA),
        "fox_w_in": nrm((N_FOX, D, FOX_IN), fan) * fox_col_scale,
        "fox_b_f": FORGET_BIAS_INIT + nrm((N_FOX, N_HEADS), 0.1),
        "fox_q_g": 1.0 + nrm((N_FOX, HEAD_DIM), 0.02),
        "fox_k_g": 1.0 + nrm((N_FOX, HEAD_DIM), 0.02),
        "fox_w_o": nrm((N_FOX, D, D), fan * DEEPNORM_BETA),
        "rwkv_mu": jax.random.uniform(next(ks), (N_RWKV, 6, D), f32, 0.0, 1.0),
        "rwkv_w_rkv": nrm((N_RWKV, 3, D, D), fan) * rkv_scale,
        "rwkv_w0": nrm((N_RWKV, D), 0.5),
        "rwkv_w1": nrm((N_RWKV, D, DECAY_LORA), fan),
        "rwkv_w2": nrm((N_RWKV, DECAY_LORA, D), 0.1 * DECAY_LORA ** -0.5),
        "rwkv_a0": nrm((N_RWKV, D), 0.1),
        "rwkv_a1": nrm((N_RWKV, D, AAA_LORA), fan),
        "rwkv_a2": nrm((N_RWKV, AAA_LORA, D), 0.1 * AAA_LORA ** -0.5),
        "rwkv_g1": nrm((N_RWKV, D, GATE_LORA), fan),
        "rwkv_g2": nrm((N_RWKV, GATE_LORA, D), GATE_LORA ** -0.5),
        "rwkv_k_k": 0.85 + nrm((N_RWKV, D), 0.05),
        "rwkv_k_a": 1.0 + nrm((N_RWKV, D), 0.05),
        "rwkv_r_k": nrm((N_RWKV, N_HEADS, HEAD_DIM), 0.1),
        "rwkv_lnx_g": 1.0 + nrm((N_RWKV, D), 0.02),
        "rwkv_lnx_b": nrm((N_RWKV, D), 0.02),
        "rwkv_w_o": nrm((N_RWKV, D, D), fan * DEEPNORM_BETA),
    }


def reference(x, c, ada_w, ada_b, ln_g, ln_b, ffn_w_in, ffn_w_out,
              fox_w_in, fox_b_f, fox_q_g, fox_k_g, fox_w_o,
              rwkv_mu, rwkv_w_rkv, rwkv_w0, rwkv_w1, rwkv_w2, rwkv_a0, rwkv_a1, rwkv_a2,
              rwkv_g1, rwkv_g2, rwkv_k_k, rwkv_k_a, rwkv_r_k, rwkv_lnx_g, rwkv_lnx_b, rwkv_w_o):
    B, D = c.shape
    c_act = jax.nn.silu(c)
    for i in range(DEPTH):
        mods = (c_act @ ada_w[i] + ada_b[i]).reshape(B, 6, D)[:, :, None, :]
        shift1, scale1, gate1, shift2, scale2, gate2 = [mods[:, n] for n in range(6)]
        j = i // N_MIXERS
        h = x * (1.0 + scale1) + shift1
        if i % N_MIXERS == 0:
            y = fox_mixer(h, fox_w_in[j], fox_b_f[j], fox_q_g[j], fox_k_g[j], fox_w_o[j])
        else:
            y = rwkv7_mixer(h, rwkv_mu[j], rwkv_w_rkv[j], rwkv_w0[j], rwkv_w1[j], rwkv_w2[j],
                            rwkv_a0[j], rwkv_a1[j], rwkv_a2[j], rwkv_g1[j], rwkv_g2[j],
                            rwkv_k_k[j], rwkv_k_a[j], rwkv_r_k[j], rwkv_lnx_g[j], rwkv_lnx_b[j],
                            rwkv_w_o[j])
        x = layer_norm(DEEPNORM_ALPHA * x + gate1 * y, ln_g[i, 0], ln_b[i, 0])
        h = x * (1.0 + scale2) + shift2
        x = layer_norm(DEEPNORM_ALPHA * x + gate2 * swiglu(h, ffn_w_in[i], ffn_w_out[i]), ln_g[i, 1], ln_b[i, 1])
    return x
```

```python
import functools

import jax
import jax.numpy as jnp
from jax import lax
from jax.experimental import pallas as pl
from jax.experimental.pallas import tpu as pltpu

F32 = jnp.float32
BF16 = jnp.bfloat16

HEAD_DIM = 64
DEPTH = 2
DEEPNORM_ALPHA = (2 * DEPTH) ** 0.25
LN_EPS = 1e-5
QK_EPS = 1e-6
GN_EPS = HEAD_DIM * 1e-5
LANES = 128
CHUNK = 64
NEG_BIG = -1e30
VMEM_LIMIT = 56 * 1024 * 1024


def _dot(a, b):
    return jnp.dot(a.astype(BF16), b.astype(BF16), preferred_element_type=F32)


def _dot_nt(a, b):
    return lax.dot_general(a.astype(BF16), b.astype(BF16), (((1,), (1,)), ((), ())),
                           preferred_element_type=F32)


def _dot_tn(a, b):
    return lax.dot_general(a.astype(BF16), b.astype(BF16), (((0,), (0,)), ((), ())),
                           preferred_element_type=F32)


def _split2(a):
    hi = a.astype(BF16)
    lo = (a - hi.astype(F32)).astype(BF16)
    return hi, lo


def _split3(a):
    hi = a.astype(BF16)
    r1 = a - hi.astype(F32)
    mid = r1.astype(BF16)
    lo = (r1 - mid.astype(F32)).astype(BF16)
    return hi, mid, lo


def _dot3(a, b):
    ah, al = _split2(a)
    bh, bl = _split2(b)
    d = functools.partial(jnp.dot, preferred_element_type=F32)
    return d(ah, bh) + (d(al, bh) + d(ah, bl))


def _dot_exact_rhs(a, b_exact):
    ah, al = _split2(a)
    d = functools.partial(jnp.dot, preferred_element_type=F32)
    return d(ah, b_exact) + d(al, b_exact)


def _layer_norm(z, g, b):
    mu = jnp.mean(z, axis=-1, keepdims=True)
    zc = z - mu
    var = jnp.mean(zc * zc, axis=-1, keepdims=True)
    return zc * lax.rsqrt(var + LN_EPS) * g + b


def _softplus(z):
    return jnp.maximum(z, 0.0) + jnp.log(1.0 + jnp.exp(-jnp.abs(z)))


def _params(*sem):
    return pltpu.CompilerParams(dimension_semantics=sem, vmem_limit_bytes=VMEM_LIMIT)


def _const_spec(shape):
    nd = len(shape)
    return pl.BlockSpec(shape, lambda *_: (0,) * nd, pipeline_mode=pl.Buffered(1))


def _mods_kernel(c_ref, w_ref, b_ref, o_ref):
    c = c_ref[...]
    ca = c * jax.nn.sigmoid(c)
    o_ref[0] = _dot3(ca, w_ref[0]) + b_ref[0]


def _mods(c, ada_w, ada_b, tn=1536):
    depth, d, n = ada_w.shape
    b = c.shape[0]
    rows = 8
    cp = jnp.pad(c, ((0, rows - b), (0, 0)))
    out = pl.pallas_call(
        _mods_kernel,
        grid=(depth, n // tn),
        in_specs=[pl.BlockSpec((rows, d), lambda l, j: (0, 0)),
                  pl.BlockSpec((1, d, tn), lambda l, j: (l, 0, j)),
                  pl.BlockSpec((1, 1, tn), lambda l, j: (l, 0, j))],
        out_specs=pl.BlockSpec((1, rows, tn), lambda l, j: (l, 0, j)),
        out_shape=jax.ShapeDtypeStruct((depth, rows, n), F32),
        compiler_params=_params("parallel", "parallel"),
        name="adaln_mods",
    )(cp, ada_w, ada_b.reshape(depth, 1, n))
    return out[:, :b].reshape(depth, b, 6, d)


def _fox_in_kernel(x_ref, mod_ref, w_ref, wf_ref, bf_ref, qg_ref, kg_ref, bd_ref, tri_ref,
                   q_ref, k_ref, v_ref, o_ref, f_ref, carry_ref, *, d, nc):
    t = pl.program_id(1)
    x = x_ref[0]
    mod = mod_ref[0]
    h = (x * (1.0 + mod[1:2]) + mod[0:1]).astype(BF16)
    bd = bd_ref[...]
    qg = qg_ref[...]
    kg = kg_ref[...]

    def head_rms(zj, g):
        ss = _dot_exact_rhs(zj * zj, bd)
        return zj * lax.rsqrt(ss * (1.0 / HEAD_DIM) + QK_EPS) * g

    for sec, out_ref in enumerate((q_ref, k_ref, v_ref, o_ref)):
        for c in range(d // nc):
            col = sec * d + c * nc
            z = jnp.dot(h, w_ref[:, col:col + nc], preferred_element_type=F32)
            for j in range(nc // LANES):
                zj = z[:, j * LANES:(j + 1) * LANES]
                if sec == 0:
                    zj = head_rms(zj, qg)
                elif sec == 1:
                    zj = head_rms(zj, kg)
                elif sec == 3:
                    zj = jax.nn.sigmoid(zj)
                lo = c * nc + j * LANES
                out_ref[0, :, lo:lo + LANES] = zj.astype(out_ref.dtype)

    fl = jnp.dot(h, wf_ref[...], preferred_element_type=F32) + bf_ref[...]
    lf = jnp.minimum(fl, 0.0) - jnp.log(1.0 + jnp.exp(-jnp.abs(fl)))
    tri = tri_ref[...]
    hi, mid, lo = _split3(lf)
    dd = functools.partial(jnp.dot, preferred_element_type=F32)
    cs = dd(tri, hi) + (dd(tri, mid) + dd(tri, lo))

    @pl.when(t == 0)
    def _():
        carry_ref[...] = jnp.zeros_like(carry_ref)

    f = cs + carry_ref[0:1, :]
    f_ref[0] = f
    tm = f.shape[0]
    carry_ref[...] = jnp.broadcast_to(f[tm - 1:tm, :], carry_ref.shape)


def _fox_in(x, mod, w_qkvo, w_f, b_f, q_g, k_g, tm=512, nc=512):
    b, t, d = x.shape
    heads_per_tile = LANES // HEAD_DIM
    qg = jnp.tile(q_g * (HEAD_DIM ** -0.5), heads_per_tile).reshape(1, LANES)
    kg = jnp.tile(k_g, heads_per_tile).reshape(1, LANES)
    idx = jnp.arange(LANES) // HEAD_DIM
    bd = (idx[:, None] == idx[None, :]).astype(BF16)
    tri = (jnp.arange(tm)[:, None] >= jnp.arange(tm)[None, :]).astype(BF16)
    act = jax.ShapeDtypeStruct((b, t, d), BF16)
    row_spec = pl.BlockSpec((1, tm, d), lambda i, j: (i, j, 0))
    kern = functools.partial(_fox_in_kernel, d=d, nc=nc)
    return pl.pallas_call(
        kern,
        grid=(b, t // tm),
        in_specs=[row_spec,
                  pl.BlockSpec((1, 6, d), lambda i, j: (i, 0, 0)),
                  _const_spec(w_qkvo.shape), _const_spec(w_f.shape), _const_spec(b_f.shape),
                  _const_spec(qg.shape), _const_spec(kg.shape), _const_spec(bd.shape),
                  _const_spec(tri.shape)],
        out_specs=[row_spec, row_spec, row_spec, row_spec,
                   pl.BlockSpec((1, tm, LANES), lambda i, j: (i, j, 0))],
        out_shape=[act, act, act, act, jax.ShapeDtypeStruct((b, t, LANES), F32)],
        scratch_shapes=[pltpu.VMEM((8, LANES), F32)],
        compiler_params=_params("parallel", "arbitrary"),
        name="fox_in_proj",
    )(x, mod, w_qkvo, w_f, b_f, qg, kg, bd, tri)


def _attn_kernel(q_ref, k_ref, v_ref, g_ref, fq_ref, fk_ref, y_ref, *, tq):
    hp = pl.program_id(1)
    qi = pl.program_id(2)
    q = q_ref[0]
    fq_blk = fq_ref[0]
    lane = lax.broadcasted_iota(jnp.int32, (tq, LANES), 1)
    row = lax.broadcasted_iota(jnp.int32, (tq, tq), 0)
    col = lax.broadcasted_iota(jnp.int32, (tq, tq), 1)
    heads = LANES // HEAD_DIM
    outs = []
    for hh in range(heads):
        in_head = (lane >= HEAD_DIM * hh) & (lane < HEAD_DIM * (hh + 1))
        qm = jnp.where(in_head, q, jnp.zeros_like(q))
        fq = jnp.sum(jnp.where(lane == heads * hp + hh, fq_blk, 0.0), axis=1, keepdims=True)

        def block(j, carry, masked, qm=qm, fq=fq, hh=hh):
            m, l, acc = carry
            start = pl.multiple_of(j * tq, tq)
            kb = k_ref[0, pl.ds(start, tq), :]
            vb = v_ref[0, pl.ds(start, tq), :]
            fk = fk_ref[0, 0, hh:hh + 1, pl.ds(start, tq)]
            z = _dot_nt(qm, kb) - fk
            if masked:
                z = jnp.where(col <= row, z, NEG_BIG)
            m_new = jnp.maximum(m, jnp.max(z, axis=1, keepdims=True) + fq)
            p = jnp.exp(z - (m_new - fq))
            alpha = jnp.exp(m - m_new)
            l = alpha * l + jnp.sum(p, axis=1, keepdims=True)
            acc = alpha * acc + jnp.dot(p.astype(BF16), vb, preferred_element_type=F32)
            return m_new, l, acc

        init = (jnp.full((tq, 1), NEG_BIG, F32), jnp.zeros((tq, 1), F32),
                jnp.zeros((tq, LANES), F32))
        carry = lax.fori_loop(0, qi, functools.partial(block, masked=False), init)
        _, l, acc = block(qi, carry, True)
        outs.append(acc / l)
    y = outs[-1]
    for hh in range(heads - 2, -1, -1):
        y = jnp.where(lane < HEAD_DIM * (hh + 1), outs[hh], y)
    y_ref[0] = (y * g_ref[0].astype(F32)).astype(y_ref.dtype)


def _fox_attention(q, k, v, og, f, tq=512):
    b, t, d = q.shape
    heads = LANES // HEAD_DIM
    n_hp = d // LANES
    n_heads = d // HEAD_DIM
    fk = jnp.transpose(f[:, :, :n_heads], (0, 2, 1)).reshape(b, n_hp, heads, t)
    blk = pl.BlockSpec((1, tq, LANES), lambda i, p, j: (i, j, p))
    full = pl.BlockSpec((1, t, LANES), lambda i, p, j: (i, 0, p))
    return pl.pallas_call(
        functools.partial(_attn_kernel, tq=tq),
        grid=(b, n_hp, t // tq),
        in_specs=[blk, full, full, blk,
                  pl.BlockSpec((1, tq, LANES), lambda i, p, j: (i, j, 0)),
                  pl.BlockSpec((1, 1, heads, t), lambda i, p, j: (i, p, 0, 0))],
        out_specs=blk,
        out_shape=jax.ShapeDtypeStruct((b, t, d), BF16),
        compiler_params=_params("parallel", "parallel", "arbitrary"),
        name="fox_attention",
    )(q, k, v, og, f, fk)


def _proj_ln_kernel(a_ref, x_ref, mod_ref, w_ref, lng_ref, lnb_ref, o_ref):
    y = jnp.dot(a_ref[0], w_ref[...], preferred_element_type=F32)
    z = DEEPNORM_ALPHA * x_ref[0] + mod_ref[0][2:3] * y
    o_ref[0] = _layer_norm(z, lng_ref[...], lnb_ref[...])


def _proj_ln(a, x, mod, w, ln_g, ln_b, tm=512):
    b, t, d = x.shape
    row_spec = pl.BlockSpec((1, tm, d), lambda i, j: (i, j, 0))
    return pl.pallas_call(
        _proj_ln_kernel,
        grid=(b, t // tm),
        in_specs=[row_spec, row_spec,
                  pl.BlockSpec((1, 6, d), lambda i, j: (i, 0, 0)),
                  _const_spec(w.shape), _const_spec((1, d)), _const_spec((1, d))],
        out_specs=row_spec,
        out_shape=jax.ShapeDtypeStruct((b, t, d), F32),
        compiler_params=_params("parallel", "parallel"),
        name="attn_out_proj_ln",
    )(a, x, mod, w, ln_g.reshape(1, d), ln_b.reshape(1, d))


def _ffn_kernel(x_ref, mod_ref, win_ref, wout_ref, lng_ref, lnb_ref, o_ref, act_ref, *, d_ff, fc):
    x = x_ref[0]
    mod = mod_ref[0]
    h = (x * (1.0 + mod[4:5]) + mod[3:4]).astype(BF16)
    for c in range(d_ff // fc):
        g = jnp.dot(h, win_ref[:, c * fc:(c + 1) * fc], preferred_element_type=F32)
        u = jnp.dot(h, win_ref[:, d_ff + c * fc:d_ff + (c + 1) * fc], preferred_element_type=F32)
        act_ref[:, c * fc:(c + 1) * fc] = (g * jax.nn.sigmoid(g) * u).astype(BF16)
    y = jnp.dot(act_ref[...], wout_ref[...], preferred_element_type=F32)
    z = DEEPNORM_ALPHA * x + mod[5:6] * y
    o_ref[0] = _layer_norm(z, lng_ref[...], lnb_ref[...])


def _ffn(x, mod, w_in, w_out, ln_g, ln_b, tm=512, fc=256):
    b, t, d = x.shape
    d_ff = w_out.shape[0]
    row_spec = pl.BlockSpec((1, tm, d), lambda i, j: (i, j, 0))
    return pl.pallas_call(
        functools.partial(_ffn_kernel, d_ff=d_ff, fc=fc),
        grid=(b, t // tm),
        in_specs=[row_spec,
                  pl.BlockSpec((1, 6, d), lambda i, j: (i, 0, 0)),
                  _const_spec(w_in.shape), _const_spec(w_out.shape),
                  _const_spec((1, d)), _const_spec((1, d))],
        out_specs=row_spec,
        out_shape=jax.ShapeDtypeStruct((b, t, d), F32),
        scratch_shapes=[pltpu.VMEM((tm, d_ff), BF16)],
        compiler_params=_params("parallel", "parallel"),
        name="swiglu_ln",
    )(x, mod, w_in, w_out, ln_g.reshape(1, d), ln_b.reshape(1, d))


def _rwkv_in_kernel(x_ref, xp_ref, mod_ref, mu_ref, wrkv_ref, w1_ref, w2_ref, a1_ref, a2_ref,
                    g1_ref, g2_ref, vec_ref, bd_ref,
                    r_ref, lw_ref, k_ref, v_ref, kk_ref, a_ref, g_ref, *, d):
    t = pl.program_id(1)
    mod = mod_ref[0]
    sc = 1.0 + mod[1:2]
    sh = mod[0:1]
    h = x_ref[0] * sc + sh
    tm = h.shape[0]
    prev = xp_ref[0][7:8, :] * sc + sh
    prev = jnp.where(t == 0, jnp.zeros_like(prev), prev)
    rows = lax.broadcasted_iota(jnp.int32, h.shape, 0)
    hprev = jnp.where(rows == 0, prev, pltpu.roll(h, 1, axis=0))
    dx = hprev - h
    mu = mu_ref[...]
    vec = vec_ref[...]
    w0, a0, k_k, k_a = vec[0:1], vec[1:2], vec[2:3], vec[3:4]

    def mix(n):
        return (h + dx * mu[n:n + 1]).astype(BF16)

    dd = functools.partial(jnp.dot, preferred_element_type=F32)
    r = dd(mix(0), wrkv_ref[0])
    k = dd(mix(1), wrkv_ref[1])
    v = dd(mix(2), wrkv_ref[2])
    ww = w0 + dd(jnp.tanh(dd(mix(3), w1_ref[...])).astype(BF16), w2_ref[...])
    lw = -jnp.exp(-_softplus(-ww) - 0.5)
    a = jax.nn.sigmoid(a0 + dd(dd(mix(4), a1_ref[...]).astype(BF16), a2_ref[...]))
    g = dd(jax.nn.sigmoid(dd(mix(5), g1_ref[...])).astype(BF16), g2_ref[...])
    kk = k * k_k
    k = k * (1.0 + (a - 1.0) * k_a)
    bd = bd_ref[...]
    for j in range(d // LANES):
        sl = slice(j * LANES, (j + 1) * LANES)
        kkj = kk[:, sl]
        ss = _dot_exact_rhs(kkj * kkj, bd)
        kk_ref[0, :, sl] = (kkj / jnp.maximum(jnp.sqrt(ss), 1e-12)).astype(kk_ref.dtype)
    r_ref[0] = r.astype(r_ref.dtype)
    lw_ref[0] = lw
    k_ref[0] = k.astype(k_ref.dtype)
    v_ref[0] = v.astype(v_ref.dtype)
    a_ref[0] = a.astype(a_ref.dtype)
    g_ref[0] = g.astype(g_ref.dtype)


def _rwkv_in(x, mod, mu, w_rkv, w1, w2, a1, a2, g1, g2, vec, tm=512):
    b, t, d = x.shape
    idx = jnp.arange(LANES) // HEAD_DIM
    bd = (idx[:, None] == idx[None, :]).astype(BF16)
    row_spec = pl.BlockSpec((1, tm, d), lambda i, j: (i, j, 0))
    prev_spec = pl.BlockSpec((1, 8, d), lambda i, j: (i, jnp.maximum(j * (tm // 8) - 1, 0), 0))
    act = jax.ShapeDtypeStruct((b, t, d), BF16)
    consts = (mu, w_rkv, w1, w2, a1, a2, g1, g2, vec, bd)
    return pl.pallas_call(
        functools.partial(_rwkv_in_kernel, d=d),
        grid=(b, t // tm),
        in_specs=[row_spec, prev_spec, pl.BlockSpec((1, 6, d), lambda i, j: (i, 0, 0))]
        + [_const_spec(c.shape) for c in consts],
        out_specs=[row_spec] * 7,
        out_shape=[act, jax.ShapeDtypeStruct((b, t, d), F32), act, act, act, act, act],
        compiler_params=_params("parallel", "parallel"),
        name="rwkv_in_proj",
    )(x, x, mod, *consts)


def _rwkv_scan_kernel(r_ref, lw_ref, k_ref, v_ref, kk_ref, a_ref, lev_ref, tri_ref,
                      y_ref, h_ref, *, groups):
    c = pl.program_id(2)
    gr = 2 * CHUNK
    st = 2 * gr
    n_lev = CHUNK.bit_length() - 1

    @pl.when(c == 0)
    def _():
        h_ref[...] = jnp.zeros_like(h_ref)

    lane = lax.broadcasted_iota(jnp.int32, (gr, LANES), 1)
    head0 = lane < HEAD_DIM
    lev = lev_ref[...]
    tri = tri_ref[...]
    strict = lev >= 0
    srow = lax.broadcasted_iota(jnp.int32, (st, st), 0)
    scol = lax.broadcasted_iota(jnp.int32, (st, st), 1)
    eye_st = srow == scol
    incl = strict | eye_st
    hrow = lax.broadcasted_iota(jnp.int32, (LANES, LANES), 0)
    hcol = lax.broadcasted_iota(jnp.int32, (LANES, LANES), 1)
    same_head = (hrow < HEAD_DIM) == (hcol < HEAD_DIM)
    eye_h = hrow == hcol

    def stack(x):
        zero = jnp.zeros_like(x)
        return jnp.concatenate([jnp.where(head0, x, zero), jnp.where(head0, zero, x)], axis=0)

    def unstack(x):
        return x[:gr] + x[gr:]

    pre = []
    for g in range(groups):
        rs = slice(g * gr, (g + 1) * gr)
        lw = lw_ref[0, rs, :]
        hi, mid, lo = _split3(lw)
        dd = functools.partial(jnp.dot, preferred_element_type=F32)
        cum = dd(tri, hi) + (dd(tri, mid) + dd(tri, lo))
        clast = jnp.concatenate(
            [jnp.broadcast_to(cum[(j + 1) * CHUNK - 1:(j + 1) * CHUNK], (CHUNK, LANES))
             for j in range(2)], axis=0)
        g_prev = jnp.exp(cum - lw)
        g_cur = jnp.exp(cum)
        g_inv = jnp.exp(-cum)
        g_rem = jnp.exp(clast - cum)
        g_last = jnp.exp(clast)
        r = r_ref[0, rs, :].astype(F32)
        k = k_ref[0, rs, :].astype(F32)
        v = v_ref[0, rs, :].astype(F32)
        kk = kk_ref[0, rs, :].astype(F32)
        lr = a_ref[0, rs, :].astype(F32)
        bt = kk * lr
        a_g = -kk * g_prev
        r_g = r * g_cur
        b_inv = bt * g_inv
        k_inv = k * g_inv
        b_rem = bt * g_rem
        k_rem = k * g_rem

        a_st = stack(a_g)
        r_st = stack(r_g)
        v_st = stack(v)
        lhs = jnp.concatenate([a_st, r_st], axis=0)
        rhs = jnp.concatenate([stack(b_inv), stack(k_inv)], axis=0)
        aa = _dot_nt(lhs, rhs)
        a_ab = jnp.where(strict, aa[:st, :st], 0.0)
        a_ak = jnp.where(strict, aa[:st, st:], 0.0)
        a_rb = jnp.where(incl, aa[st:, :st], 0.0)
        a_rk = jnp.where(incl, aa[st:, st:], 0.0)

        x = jnp.where(eye_st, 1.0, jnp.where(lev == 0, a_ab, 0.0))
        for level in range(1, n_lev):
            a_off = jnp.where(lev == level, a_ab, 0.0)
            x = x + _dot3(x, _dot3(a_off, x))

        gmat = jnp.concatenate([_dot(a_ak, v_st), a_st], axis=1)
        tg = _dot(x, gmat)
        rb = _dot(a_rb, tg)
        yi = unstack(rb[:, :LANES] + _dot(a_rk, v_st))
        rq = unstack(r_st + rb[:, LANES:])
        uv = unstack(tg[:, :LANES])
        wa = unstack(tg[:, LANES:])
        per_chunk = []
        for j in range(2):
            cs = slice(j * CHUNK, (j + 1) * CHUNK)
            lhs_t = jnp.concatenate([b_rem[cs], k_rem[cs]], axis=0)
            rhs_t = jnp.concatenate(
                [jnp.concatenate([uv[cs], wa[cs]], axis=1),
                 jnp.concatenate([v[cs], jnp.zeros_like(v[cs])], axis=1)], axis=0)
            nm = _dot_tn(lhs_t, rhs_t)
            n_j = jnp.where(same_head, nm[:, :LANES], 0.0)
            decay = jnp.broadcast_to(g_last[j * CHUNK:j * CHUNK + 1], (LANES, LANES))
            m_j = jnp.where(same_head, nm[:, LANES:], 0.0) + jnp.where(eye_h, decay, 0.0)
            per_chunk.append((yi[cs], rq[cs], m_j, n_j))
        pre.append(per_chunk)

    h = h_ref[...]
    for g in range(groups):
        for j in range(2):
            yi, rq, m_j, n_j = pre[g][j]
            lo = g * gr + j * CHUNK
            y_ref[0, lo:lo + CHUNK, :] = yi + _dot(rq, h)
            h = _dot3(m_j, h) + n_j
    h_ref[...] = h


def _rwkv_scan(r, lw, k, v, kk, a, groups=2):
    b, t, d = r.shape
    gr = 2 * CHUNK
    st = 2 * gr
    rows = gr * groups
    idx = jnp.arange(st)
    xor = idx[:, None] ^ idx[None, :]
    same = (idx[:, None] // CHUNK) == (idx[None, :] // CHUNK)
    lower = idx[None, :] < idx[:, None]
    msb = jnp.floor(jnp.log2(jnp.maximum(xor, 1).astype(F32))).astype(jnp.int32)
    lev = jnp.where(same & lower, msb, -1).astype(jnp.int32)
    ti = jnp.arange(gr)
    tri = ((ti[:, None] >= ti[None, :]) & ((ti[:, None] // CHUNK) == (ti[None, :] // CHUNK))).astype(BF16)
    blk = pl.BlockSpec((1, rows, LANES), lambda i, p, j: (i, j, p))
    return pl.pallas_call(
        functools.partial(_rwkv_scan_kernel, groups=groups),
        grid=(b, d // LANES, t // rows),
        in_specs=[blk] * 6 + [_const_spec(lev.shape), _const_spec(tri.shape)],
        out_specs=blk,
        out_shape=jax.ShapeDtypeStruct((b, t, d), F32),
        scratch_shapes=[pltpu.VMEM((LANES, LANES), F32)],
        compiler_params=_params("parallel", "parallel", "arbitrary"),
        name="rwkv7_chunk_scan",
    )(r, lw, k, v, kk, a, lev, tri)


def _rwkv_out_kernel(y_ref, r_ref, k_ref, v_ref, g_ref, x_ref, mod_ref, vec_ref, bd_ref, w_ref,
                     lng_ref, lnb_ref, o_ref, act_ref, *, d):
    bd = bd_ref[...]
    vec = vec_ref[...]
    inv = 1.0 / HEAD_DIM
    for j in range(d // LANES):
        sl = slice(j * LANES, (j + 1) * LANES)
        y = y_ref[0, :, sl]
        mean = _dot_exact_rhs(y, bd) * inv
        yc = y - mean
        var = _dot_exact_rhs(yc * yc, bd) * inv
        yn = yc * lax.rsqrt(var + GN_EPS) * vec[0:1, sl] + vec[1:2, sl]
        r = r_ref[0, :, sl].astype(F32)
        k = k_ref[0, :, sl].astype(F32)
        bonus = _dot_exact_rhs(r * k * vec[2:3, sl], bd) * v_ref[0, :, sl].astype(F32)
        act_ref[:, sl] = ((yn + bonus) * g_ref[0, :, sl].astype(F32)).astype(BF16)
    out = jnp.dot(act_ref[...], w_ref[...], preferred_element_type=F32)
    z = DEEPNORM_ALPHA * x_ref[0] + mod_ref[0][2:3] * out
    o_ref[0] = _layer_norm(z, lng_ref[...], lnb_ref[...])


def _rwkv_out(y, r, k, v, g, x, mod, vec, w, ln_g, ln_b, tm=512):
    b, t, d = x.shape
    idx = jnp.arange(LANES) // HEAD_DIM
    bd = (idx[:, None] == idx[None, :]).astype(BF16)
    row_spec = pl.BlockSpec((1, tm, d), lambda i, j: (i, j, 0))
    return pl.pallas_call(
        functools.partial(_rwkv_out_kernel, d=d),
        grid=(b, t // tm),
        in_specs=[row_spec] * 6
        + [pl.BlockSpec((1, 6, d), lambda i, j: (i, 0, 0)),
           _const_spec(vec.shape), _const_spec(bd.shape), _const_spec(w.shape),
           _const_spec((1, d)), _const_spec((1, d))],
        out_specs=row_spec,
        out_shape=jax.ShapeDtypeStruct((b, t, d), F32),
        scratch_shapes=[pltpu.VMEM((tm, d), BF16)],
        compiler_params=_params("parallel", "parallel"),
        name="rwkv_out_proj_ln",
    )(y, r, k, v, g, x, mod, vec, bd, w, ln_g.reshape(1, d), ln_b.reshape(1, d))


def _trunk(x, c, ada_w, ada_b, ln_g, ln_b, ffn_w_in, ffn_w_out, fox_w_in, fox_b_f, fox_q_g, fox_k_g, fox_w_o, rwkv_mu, rwkv_w_rkv, rwkv_w0, rwkv_w1, rwkv_w2, rwkv_a0, rwkv_a1, rwkv_a2, rwkv_g1, rwkv_g2, rwkv_k_k, rwkv_k_a, rwkv_r_k, rwkv_lnx_g, rwkv_lnx_b, rwkv_w_o,
           *, tm, tm_rwkv, tq, groups, tn, nc, fc):
    b, t, d = x.shape
    n_heads = d // HEAD_DIM
    mods = _mods(c, ada_w, ada_b, tn=tn)

    w_in = fox_w_in[0]
    f_lo = 3 * d
    w_qkvo = jnp.concatenate([w_in[:, :f_lo], w_in[:, f_lo + n_heads:]], axis=1).astype(BF16)
    w_f = jnp.pad(w_in[:, f_lo:f_lo + n_heads], ((0, 0), (0, LANES - n_heads))).astype(BF16)
    b_f = jnp.pad(fox_b_f[0], (0, LANES - n_heads)).reshape(1, LANES)
    q, k, v, og, f = _fox_in(x, mods[0], w_qkvo, w_f, b_f, fox_q_g[0], fox_k_g[0], tm=tm, nc=nc)
    att = _fox_attention(q, k, v, og, f, tq=tq)
    x = _proj_ln(att, x, mods[0], fox_w_o[0].astype(BF16), ln_g[0, 0], ln_b[0, 0], tm=tm)
    x = _ffn(x, mods[0], ffn_w_in[0].astype(BF16), ffn_w_out[0].astype(BF16), ln_g[0, 1], ln_b[0, 1],
             tm=tm, fc=fc)

    vec_in = jnp.stack([rwkv_w0[0], rwkv_a0[0], rwkv_k_k[0], rwkv_k_a[0]])
    r, lw, k, v, kk, a, g = _rwkv_in(
        x, mods[1], rwkv_mu[0], rwkv_w_rkv[0].astype(BF16),
        rwkv_w1[0].astype(BF16), rwkv_w2[0].astype(BF16),
        rwkv_a1[0].astype(BF16), rwkv_a2[0].astype(BF16),
        rwkv_g1[0].astype(BF16), rwkv_g2[0].astype(BF16), vec_in, tm=tm_rwkv)
    y = _rwkv_scan(r, lw, k, v, kk, a, groups=groups)
    vec_out = jnp.stack([rwkv_lnx_g[0], rwkv_lnx_b[0], rwkv_r_k[0].reshape(d)])
    x = _rwkv_out(y, r, k, v, g, x, mods[1], vec_out, rwkv_w_o[0].astype(BF16), ln_g[1, 0], ln_b[1, 0],
                  tm=tm)
    x = _ffn(x, mods[1], ffn_w_in[1].astype(BF16), ffn_w_out[1].astype(BF16), ln_g[1, 1], ln_b[1, 1],
             tm=tm, fc=fc)
    return x


def kernel(x, c, ada_w, ada_b, ln_g, ln_b, ffn_w_in, ffn_w_out, fox_w_in, fox_b_f, fox_q_g, fox_k_g, fox_w_o, rwkv_mu, rwkv_w_rkv, rwkv_w0, rwkv_w1, rwkv_w2, rwkv_a0, rwkv_a1, rwkv_a2, rwkv_g1, rwkv_g2, rwkv_k_k, rwkv_k_a, rwkv_r_k, rwkv_lnx_g, rwkv_lnx_b, rwkv_w_o):
    return _trunk(x, c, ada_w, ada_b, ln_g, ln_b, ffn_w_in, ffn_w_out, fox_w_in, fox_b_f, fox_q_g, fox_k_g, fox_w_o, rwkv_mu, rwkv_w_rkv, rwkv_w0, rwkv_w1, rwkv_w2, rwkv_a0, rwkv_a1, rwkv_a2, rwkv_g1, rwkv_g2, rwkv_k_k, rwkv_k_a, rwkv_r_k, rwkv_lnx_g, rwkv_lnx_b, rwkv_w_o,
                  tm=512, tm_rwkv=256, tq=512, groups=2, tn=1536, nc=512, fc=256)
```

```python
import functools

import jax
import jax.numpy as jnp
from jax import lax
from jax.experimental import pallas as pl
from jax.experimental.pallas import tpu as pltpu

F32 = jnp.float32
BF16 = jnp.bfloat16

HEAD_DIM = 64
DEPTH = 2
DEEPNORM_ALPHA = (2 * DEPTH) ** 0.25
LN_EPS = 1e-5
QK_EPS = 1e-6
GN_EPS = HEAD_DIM * 1e-5
LANES = 128
CHUNK = 64
NEG_BIG = -1e30
VMEM_LIMIT = 56 * 1024 * 1024


def _dot(a, b):
    return jnp.dot(a.astype(BF16), b.astype(BF16), preferred_element_type=F32)


def _dot_nt(a, b):
    return lax.dot_general(a.astype(BF16), b.astype(BF16), (((1,), (1,)), ((), ())),
                           preferred_element_type=F32)


def _dot_tn(a, b):
    return lax.dot_general(a.astype(BF16), b.astype(BF16), (((0,), (0,)), ((), ())),
                           preferred_element_type=F32)


def _split2(a):
    hi = a.astype(BF16)
    lo = (a - hi.astype(F32)).astype(BF16)
    return hi, lo


def _split3(a):
    hi = a.astype(BF16)
    r1 = a - hi.astype(F32)
    mid = r1.astype(BF16)
    lo = (r1 - mid.astype(F32)).astype(BF16)
    return hi, mid, lo


def _dot3(a, b):
    ah, al = _split2(a)
    bh, bl = _split2(b)
    d = functools.partial(jnp.dot, preferred_element_type=F32)
    return d(ah, bh) + (d(al, bh) + d(ah, bl))


def _dot_exact_rhs(a, b_exact):
    ah, al = _split2(a)
    d = functools.partial(jnp.dot, preferred_element_type=F32)
    return d(ah, b_exact) + d(al, b_exact)


def _layer_norm(z, g, b):
    mu = jnp.mean(z, axis=-1, keepdims=True)
    zc = z - mu
    var = jnp.mean(zc * zc, axis=-1, keepdims=True)
    return zc * lax.rsqrt(var + LN_EPS) * g + b


def _softplus(z):
    return jnp.maximum(z, 0.0) + jnp.log(1.0 + jnp.exp(-jnp.abs(z)))


def _params(*sem):
    return pltpu.CompilerParams(dimension_semantics=sem, vmem_limit_bytes=VMEM_LIMIT)


def _const_spec(shape):
    nd = len(shape)
    return pl.BlockSpec(shape, lambda *_: (0,) * nd, pipeline_mode=pl.Buffered(1))


def _mods_kernel(c_ref, w_ref, b_ref, o_ref):
    c = c_ref[...]
    ca = c * jax.nn.sigmoid(c)
    o_ref[0] = _dot3(ca, w_ref[0]) + b_ref[0]


def _mods(c, ada_w, ada_b, tn=1536):
    depth, d, n = ada_w.shape
    b = c.shape[0]
    rows = 8
    cp = jnp.pad(c, ((0, rows - b), (0, 0)))
    out = pl.pallas_call(
        _mods_kernel,
        grid=(depth, n // tn),
        in_specs=[pl.BlockSpec((rows, d), lambda l, j: (0, 0)),
                  pl.BlockSpec((1, d, tn), lambda l, j: (l, 0, j)),
                  pl.BlockSpec((1, 1, tn), lambda l, j: (l, 0, j))],
        out_specs=pl.BlockSpec((1, rows, tn), lambda l, j: (l, 0, j)),
        out_shape=jax.ShapeDtypeStruct((depth, rows, n), F32),
        compiler_params=_params("parallel", "parallel"),
        name="adaln_mods",
    )(cp, ada_w, ada_b.reshape(depth, 1, n))
    return out[:, :b].reshape(depth, b, 6, d)


def _fox_in_kernel(x_ref, mod_ref, w_ref, wf_ref, bf_ref, qg_ref, kg_ref, bd_ref, tri_ref,
                   q_ref, k_ref, v_ref, o_ref, f_ref, carry_ref, *, d, nc):
    t = pl.program_id(1)
    x = x_ref[0]
    mod = mod_ref[0]
    h = (x * (1.0 + mod[1:2]) + mod[0:1]).astype(BF16)
    bd = bd_ref[...]
    qg = qg_ref[...]
    kg = kg_ref[...]

    def head_rms(zj, g):
        ss = _dot_exact_rhs(zj * zj, bd)
        return zj * lax.rsqrt(ss * (1.0 / HEAD_DIM) + QK_EPS) * g

    for sec, out_ref in enumerate((q_ref, k_ref, v_ref, o_ref)):
        for c in range(d // nc):
            col = sec * d + c * nc
            z = jnp.dot(h, w_ref[:, col:col + nc], preferred_element_type=F32)
            for j in range(nc // LANES):
                zj = z[:, j * LANES:(j + 1) * LANES]
                if sec == 0:
                    zj = head_rms(zj, qg)
                elif sec == 1:
                    zj = head_rms(zj, kg)
                elif sec == 3:
                    zj = jax.nn.sigmoid(zj)
                lo = c * nc + j * LANES
                out_ref[0, :, lo:lo + LANES] = zj.astype(out_ref.dtype)

    fl = jnp.dot(h, wf_ref[...], preferred_element_type=F32) + bf_ref[...]
    lf = jnp.minimum(fl, 0.0) - jnp.log(1.0 + jnp.exp(-jnp.abs(fl)))
    tri = tri_ref[...]
    hi, mid, lo = _split3(lf)
    dd = functools.partial(jnp.dot, preferred_element_type=F32)
    cs = dd(tri, hi) + (dd(tri, mid) + dd(tri, lo))

    @pl.when(t == 0)
    def _():
        carry_ref[...] = jnp.zeros_like(carry_ref)

    f = cs + carry_ref[0:1, :]
    f_ref[0] = f
    tm = f.shape[0]
    carry_ref[...] = jnp.broadcast_to(f[tm - 1:tm, :], carry_ref.shape)


def _fox_in(x, mod, w_qkvo, w_f, b_f, q_g, k_g, tm=512, nc=512):
    b, t, d = x.shape
    heads_per_tile = LANES // HEAD_DIM
    qg = jnp.tile(q_g * (HEAD_DIM ** -0.5), heads_per_tile).reshape(1, LANES)
    kg = jnp.tile(k_g, heads_per_tile).reshape(1, LANES)
    idx = jnp.arange(LANES) // HEAD_DIM
    bd = (idx[:, None] == idx[None, :]).astype(BF16)
    tri = (jnp.arange(tm)[:, None] >= jnp.arange(tm)[None, :]).astype(BF16)
    act = jax.ShapeDtypeStruct((b, t, d), BF16)
    row_spec = pl.BlockSpec((1, tm, d), lambda i, j: (i, j, 0))
    kern = functools.partial(_fox_in_kernel, d=d, nc=nc)
    return pl.pallas_call(
        kern,
        grid=(b, t // tm),
        in_specs=[row_spec,
                  pl.BlockSpec((1, 6, d), lambda i, j: (i, 0, 0)),
                  _const_spec(w_qkvo.shape), _const_spec(w_f.shape), _const_spec(b_f.shape),
                  _const_spec(qg.shape), _const_spec(kg.shape), _const_spec(bd.shape),
                  _const_spec(tri.shape)],
        out_specs=[row_spec, row_spec, row_spec, row_spec,
                   pl.BlockSpec((1, tm, LANES), lambda i, j: (i, j, 0))],
        out_shape=[act, act, act, act, jax.ShapeDtypeStruct((b, t, LANES), F32)],
        scratch_shapes=[pltpu.VMEM((8, LANES), F32)],
        compiler_params=_params("parallel", "arbitrary"),
        name="fox_in_proj",
    )(x, mod, w_qkvo, w_f, b_f, qg, kg, bd, tri)


def _attn_kernel(q_ref, k_ref, v_ref, g_ref, fq_ref, fk_ref, y_ref, *, tq):
    hp = pl.program_id(1)
    qi = pl.program_id(2)
    q = q_ref[0]
    fq_blk = fq_ref[0]
    lane = lax.broadcasted_iota(jnp.int32, (tq, LANES), 1)
    row = lax.broadcasted_iota(jnp.int32, (tq, tq), 0)
    col = lax.broadcasted_iota(jnp.int32, (tq, tq), 1)
    heads = LANES // HEAD_DIM
    outs = []
    for hh in range(heads):
        in_head = (lane >= HEAD_DIM * hh) & (lane < HEAD_DIM * (hh + 1))
        qm = jnp.where(in_head, q, jnp.zeros_like(q))
        fq = jnp.sum(jnp.where(lane == heads * hp + hh, fq_blk, 0.0), axis=1, keepdims=True)

        def block(j, carry, masked, qm=qm, fq=fq, hh=hh):
            m, l, acc = carry
            start = pl.multiple_of(j * tq, tq)
            kb = k_ref[0, pl.ds(start, tq), :]
            vb = v_ref[0, pl.ds(start, tq), :]
            fk = fk_ref[0, 0, hh:hh + 1, pl.ds(start, tq)]
            z = _dot_nt(qm, kb) - fk
            if masked:
                z = jnp.where(col <= row, z, NEG_BIG)
            m_new = jnp.maximum(m, jnp.max(z, axis=1, keepdims=True) + fq)
            p = jnp.exp(z - (m_new - fq))
            alpha = jnp.exp(m - m_new)
            l = alpha * l + jnp.sum(p, axis=1, keepdims=True)
            acc = alpha * acc + jnp.dot(p.astype(BF16), vb, preferred_element_type=F32)
            return m_new, l, acc

        init = (jnp.full((tq, 1), NEG_BIG, F32), jnp.zeros((tq, 1), F32),
                jnp.zeros((tq, LANES), F32))
        carry = lax.fori_loop(0, qi, functools.partial(block, masked=False), init)
        _, l, acc = block(qi, carry, True)
        outs.append(acc / l)
    y = outs[-1]
    for hh in range(heads - 2, -1, -1):
        y = jnp.where(lane < HEAD_DIM * (hh + 1), outs[hh], y)
    y_ref[0] = (y * g_ref[0].astype(F32)).astype(y_ref.dtype)


def _fox_attention(q, k, v, og, f, tq=512):
    b, t, d = q.shape
    heads = LANES // HEAD_DIM
    n_hp = d // LANES
    n_heads = d // HEAD_DIM
    fk = jnp.transpose(f[:, :, :n_heads], (0, 2, 1)).reshape(b, n_hp, heads, t)
    blk = pl.BlockSpec((1, tq, LANES), lambda i, p, j: (i, j, p))
    full = pl.BlockSpec((1, t, LANES), lambda i, p, j: (i, 0, p))
    return pl.pallas_call(
        functools.partial(_attn_kernel, tq=tq),
        grid=(b, n_hp, t // tq),
        in_specs=[blk, full, full, blk,
                  pl.BlockSpec((1, tq, LANES), lambda i, p, j: (i, j, 0)),
                  pl.BlockSpec((1, 1, heads, t), lambda i, p, j: (i, p, 0, 0))],
        out_specs=blk,
        out_shape=jax.ShapeDtypeStruct((b, t, d), BF16),
        compiler_params=_params("parallel", "parallel", "arbitrary"),
        name="fox_attention",
    )(q, k, v, og, f, fk)


def _proj_ln_kernel(a_ref, x_ref, mod_ref, w_ref, lng_ref, lnb_ref, o_ref):
    y = jnp.dot(a_ref[0], w_ref[...], preferred_element_type=F32)
    z = DEEPNORM_ALPHA * x_ref[0] + mod_ref[0][2:3] * y
    o_ref[0] = _layer_norm(z, lng_ref[...], lnb_ref[...])


def _proj_ln(a, x, mod, w, ln_g, ln_b, tm=512):
    b, t, d = x.shape
    row_spec = pl.BlockSpec((1, tm, d), lambda i, j: (i, j, 0))
    return pl.pallas_call(
        _proj_ln_kernel,
        grid=(b, t // tm),
        in_specs=[row_spec, row_spec,
                  pl.BlockSpec((1, 6, d), lambda i, j: (i, 0, 0)),
                  _const_spec(w.shape), _const_spec((1, d)), _const_spec((1, d))],
        out_specs=row_spec,
        out_shape=jax.ShapeDtypeStruct((b, t, d), F32),
        compiler_params=_params("parallel", "parallel"),
        name="attn_out_proj_ln",
    )(a, x, mod, w, ln_g.reshape(1, d), ln_b.reshape(1, d))


def _ffn_kernel(x_ref, mod_ref, win_ref, wout_ref, lng_ref, lnb_ref, o_ref, act_ref, *, d_ff, fc):
    x = x_ref[0]
    mod = mod_ref[0]
    h = (x * (1.0 + mod[4:5]) + mod[3:4]).astype(BF16)
    for c in range(d_ff // fc):
        g = jnp.dot(h, win_ref[:, c * fc:(c + 1) * fc], preferred_element_type=F32)
        u = jnp.dot(h, win_ref[:, d_ff + c * fc:d_ff + (c + 1) * fc], preferred_element_type=F32)
        act_ref[:, c * fc:(c + 1) * fc] = (g * jax.nn.sigmoid(g) * u).astype(BF16)
    y = jnp.dot(act_ref[...], wout_ref[...], preferred_element_type=F32)
    z = DEEPNORM_ALPHA * x + mod[5:6] * y
    o_ref[0] = _layer_norm(z, lng_ref[...], lnb_ref[...])


def _ffn(x, mod, w_in, w_out, ln_g, ln_b, tm=512, fc=256):
    b, t, d = x.shape
    d_ff = w_out.shape[0]
    row_spec = pl.BlockSpec((1, tm, d), lambda i, j: (i, j, 0))
    return pl.pallas_call(
        functools.partial(_ffn_kernel, d_ff=d_ff, fc=fc),
        grid=(b, t // tm),
        in_specs=[row_spec,
                  pl.BlockSpec((1, 6, d), lambda i, j: (i, 0, 0)),
                  _const_spec(w_in.shape), _const_spec(w_out.shape),
                  _const_spec((1, d)), _const_spec((1, d))],
        out_specs=row_spec,
        out_shape=jax.ShapeDtypeStruct((b, t, d), F32),
        scratch_shapes=[pltpu.VMEM((tm, d_ff), BF16)],
        compiler_params=_params("parallel", "parallel"),
        name="swiglu_ln",
    )(x, mod, w_in, w_out, ln_g.reshape(1, d), ln_b.reshape(1, d))


def _rwkv_in_kernel(x_ref, xp_ref, mod_ref, mu_ref, wrkv_ref, w1_ref, w2_ref, a1_ref, a2_ref,
                    g1_ref, g2_ref, vec_ref, bd_ref,
                    r_ref, lw_ref, k_ref, v_ref, kk_ref, a_ref, g_ref, *, d):
    t = pl.program_id(1)
    mod = mod_ref[0]
    sc = 1.0 + mod[1:2]
    sh = mod[0:1]
    h = x_ref[0] * sc + sh
    tm = h.shape[0]
    prev = xp_ref[0][7:8, :] * sc + sh
    prev = jnp.where(t == 0, jnp.zeros_like(prev), prev)
    rows = lax.broadcasted_iota(jnp.int32, h.shape, 0)
    hprev = jnp.where(rows == 0, prev, pltpu.roll(h, 1, axis=0))
    dx = hprev - h
    mu = mu_ref[...]
    vec = vec_ref[...]
    w0, a0, k_k, k_a = vec[0:1], vec[1:2], vec[2:3], vec[3:4]

    def mix(n):
        return (h + dx * mu[n:n + 1]).astype(BF16)

    dd = functools.partial(jnp.dot, preferred_element_type=F32)
    r = dd(mix(0), wrkv_ref[0])
    k = dd(mix(1), wrkv_ref[1])
    v = dd(mix(2), wrkv_ref[2])
    ww = w0 + dd(jnp.tanh(dd(mix(3), w1_ref[...])).astype(BF16), w2_ref[...])
    lw = -jnp.exp(-_softplus(-ww) - 0.5)
    a = jax.nn.sigmoid(a0 + dd(dd(mix(4), a1_ref[...]).astype(BF16), a2_ref[...]))
    g = dd(jax.nn.sigmoid(dd(mix(5), g1_ref[...])).astype(BF16), g2_ref[...])
    kk = k * k_k
    k = k * (1.0 + (a - 1.0) * k_a)
    bd = bd_ref[...]
    for j in range(d // LANES):
        sl = slice(j * LANES, (j + 1) * LANES)
        kkj = kk[:, sl]
        ss = _dot_exact_rhs(kkj * kkj, bd)
        kk_ref[0, :, sl] = (kkj / jnp.maximum(jnp.sqrt(ss), 1e-12)).astype(kk_ref.dtype)
    r_ref[0] = r.astype(r_ref.dtype)
    lw_ref[0] = lw
    k_ref[0] = k.astype(k_ref.dtype)
    v_ref[0] = v.astype(v_ref.dtype)
    a_ref[0] = a.astype(a_ref.dtype)
    g_ref[0] = g.astype(g_ref.dtype)


def _rwkv_in(x, mod, mu, w_rkv, w1, w2, a1, a2, g1, g2, vec, tm=512):
    b, t, d = x.shape
    idx = jnp.arange(LANES) // HEAD_DIM
    bd = (idx[:, None] == idx[None, :]).astype(BF16)
    row_spec = pl.BlockSpec((1, tm, d), lambda i, j: (i, j, 0))
    prev_spec = pl.BlockSpec((1, 8, d), lambda i, j: (i, jnp.maximum(j * (tm // 8) - 1, 0), 0))
    act = jax.ShapeDtypeStruct((b, t, d), BF16)
    consts = (mu, w_rkv, w1, w2, a1, a2, g1, g2, vec, bd)
    return pl.pallas_call(
        functools.partial(_rwkv_in_kernel, d=d),
        grid=(b, t // tm),
        in_specs=[row_spec, prev_spec, pl.BlockSpec((1, 6, d), lambda i, j: (i, 0, 0))]
        + [_const_spec(c.shape) for c in consts],
        out_specs=[row_spec] * 7,
        out_shape=[act, jax.ShapeDtypeStruct((b, t, d), F32), act, act, act, act, act],
        compiler_params=_params("parallel", "parallel"),
        name="rwkv_in_proj",
    )(x, x, mod, *consts)


def _rwkv_scan_kernel(r_ref, lw_ref, k_ref, v_ref, kk_ref, a_ref, lev_ref, tri_ref,
                      y_ref, h_ref, yi_ref, rq_ref, m_ref, n_ref, *, groups, steps_per_seq):
    s = pl.program_id(0)
    gr = 2 * CHUNK
    st = 2 * gr
    n_lev = CHUNK.bit_length() - 1

    @pl.when(s == 0)
    def _():
        h_ref[...] = jnp.zeros_like(h_ref)
        yi_ref[...] = jnp.zeros_like(yi_ref)
        rq_ref[...] = jnp.zeros_like(rq_ref)
        m_ref[...] = jnp.zeros_like(m_ref)
        n_ref[...] = jnp.zeros_like(n_ref)

    first_of_seq = lax.rem(jnp.maximum(s - 1, 0), steps_per_seq) == 0
    state = [jnp.where(first_of_seq, 0.0, h_ref[...])]
    pending = list(range(2 * groups))

    def state_steps(count):
        for _ in range(count):
            if pending:
                i = pending.pop(0)
                lo = i * CHUNK
                h = state[0]
                y_ref[0, lo:lo + CHUNK, :] = (yi_ref[lo:lo + CHUNK, :]
                                              + _dot(rq_ref[lo:lo + CHUNK, :], h))
                state[0] = _dot3(m_ref[i], h) + n_ref[i]

    lane = lax.broadcasted_iota(jnp.int32, (gr, LANES), 1)
    head0 = lane < HEAD_DIM
    lev = lev_ref[...]
    tri = tri_ref[...]
    strict = lev >= 0
    srow = lax.broadcasted_iota(jnp.int32, (st, st), 0)
    scol = lax.broadcasted_iota(jnp.int32, (st, st), 1)
    eye_st = srow == scol
    incl = strict | eye_st
    hrow = lax.broadcasted_iota(jnp.int32, (LANES, LANES), 0)
    hcol = lax.broadcasted_iota(jnp.int32, (LANES, LANES), 1)
    same_head = (hrow < HEAD_DIM) == (hcol < HEAD_DIM)
    eye_h = hrow == hcol

    def stack(x):
        zero = jnp.zeros_like(x)
        return jnp.concatenate([jnp.where(head0, x, zero), jnp.where(head0, zero, x)], axis=0)

    def unstack(x):
        return x[:gr] + x[gr:]

    def prep(g):
        rs = slice(g * gr, (g + 1) * gr)
        lw = lw_ref[0, rs, :]
        hi, mid, lo = _split3(lw)
        dd = functools.partial(jnp.dot, preferred_element_type=F32)
        cum = dd(tri, hi) + (dd(tri, mid) + dd(tri, lo))
        clast = jnp.concatenate(
            [jnp.broadcast_to(cum[(j + 1) * CHUNK - 1:(j + 1) * CHUNK], (CHUNK, LANES))
             for j in range(2)], axis=0)
        g_inv = jnp.exp(-cum)
        g_rem = jnp.exp(clast - cum)
        r = r_ref[0, rs, :].astype(F32)
        k = k_ref[0, rs, :].astype(F32)
        v = v_ref[0, rs, :].astype(F32)
        kk = kk_ref[0, rs, :].astype(F32)
        bt = kk * a_ref[0, rs, :].astype(F32)
        a_st = stack(-kk * jnp.exp(cum - lw))
        r_st = stack(r * jnp.exp(cum))
        lhs = jnp.concatenate([a_st, r_st], axis=0)
        rhs = jnp.concatenate([stack(bt * g_inv), stack(k * g_inv)], axis=0)
        aa = _dot_nt(lhs, rhs)
        return dict(
            v=v, v_st=stack(v), a_st=a_st, r_st=r_st, b_rem=bt * g_rem, k_rem=k * g_rem,
            g_last=jnp.exp(clast),
            a_ab=jnp.where(strict, aa[:st, :st], 0.0), a_ak=jnp.where(strict, aa[:st, st:], 0.0),
            a_rb=jnp.where(incl, aa[st:, :st], 0.0), a_rk=jnp.where(incl, aa[st:, st:], 0.0))

    per_stage = -(-2 * groups // (n_lev + 2))
    ps = [prep(g) for g in range(groups)]
    gs = range(groups)
    state_steps(per_stage)

    xs = [jnp.where(eye_st, 1.0, jnp.where(lev == 0, p["a_ab"], 0.0)) for p in ps]
    for level in range(1, n_lev):
        ws = [_dot(jnp.where(lev == level, ps[g]["a_ab"], 0.0), xs[g]) for g in gs]
        xs = [xs[g] + _dot(xs[g], ws[g]) for g in gs]
        state_steps(per_stage)
    res = [jnp.where(eye_st, 1.0, 0.0) - xs[g] + _dot3(ps[g]["a_ab"], xs[g]) for g in gs]
    state_steps(per_stage)
    xs = [xs[g] + _dot(xs[g], res[g]) for g in gs]

    gm = [jnp.concatenate([_dot(ps[g]["a_ak"], ps[g]["v_st"]), ps[g]["a_st"]], axis=1) for g in gs]
    state_steps(2 * groups)
    h_ref[...] = state[0]
    tg = [_dot(xs[g], gm[g]) for g in gs]
    rb = [_dot(ps[g]["a_rb"], tg[g]) for g in gs]
    rk = [_dot(ps[g]["a_rk"], ps[g]["v_st"]) for g in gs]
    for g in gs:
        p = ps[g]
        yi_ref[g * gr:(g + 1) * gr, :] = unstack(rb[g][:, :LANES] + rk[g])
        rq_ref[g * gr:(g + 1) * gr, :] = unstack(p["r_st"] + rb[g][:, LANES:])
        uv = unstack(tg[g][:, :LANES])
        wa = unstack(tg[g][:, LANES:])
        for j in range(2):
            cs = slice(j * CHUNK, (j + 1) * CHUNK)
            lhs_t = jnp.concatenate([p["b_rem"][cs], p["k_rem"][cs]], axis=0)
            rhs_t = jnp.concatenate(
                [jnp.concatenate([uv[cs], wa[cs]], axis=1),
                 jnp.concatenate([p["v"][cs], jnp.zeros_like(p["v"][cs])], axis=1)], axis=0)
            nm = _dot_tn(lhs_t, rhs_t)
            n_ref[2 * g + j] = jnp.where(same_head, nm[:, :LANES], 0.0)
            decay = jnp.broadcast_to(p["g_last"][j * CHUNK:j * CHUNK + 1], (LANES, LANES))
            m_ref[2 * g + j] = (jnp.where(same_head, nm[:, LANES:], 0.0)
                                + jnp.where(eye_h, decay, 0.0))


def _rwkv_scan(r, lw, k, v, kk, a, groups=2):
    b, t, d = r.shape
    gr = 2 * CHUNK
    st = 2 * gr
    rows = gr * groups
    n_hp = d // LANES
    steps_per_seq = t // rows
    n_blocks = b * n_hp * steps_per_seq

    def block_index(blk):
        seq = blk // steps_per_seq
        return seq // n_hp, blk % steps_per_seq, seq % n_hp

    in_blk = pl.BlockSpec((1, rows, LANES), lambda s: block_index(jnp.minimum(s, n_blocks - 1)))
    out_blk = pl.BlockSpec((1, rows, LANES), lambda s: block_index(jnp.maximum(s - 1, 0)))
    idx = jnp.arange(st)
    xor = idx[:, None] ^ idx[None, :]
    same = (idx[:, None] // CHUNK) == (idx[None, :] // CHUNK)
    lower = idx[None, :] < idx[:, None]
    msb = jnp.floor(jnp.log2(jnp.maximum(xor, 1).astype(F32))).astype(jnp.int32)
    lev = jnp.where(same & lower, msb, -1).astype(jnp.int32)
    ti = jnp.arange(gr)
    tri = ((ti[:, None] >= ti[None, :]) & ((ti[:, None] // CHUNK) == (ti[None, :] // CHUNK))).astype(BF16)
    return pl.pallas_call(
        functools.partial(_rwkv_scan_kernel, groups=groups, steps_per_seq=steps_per_seq),
        grid=(n_blocks + 1,),
        in_specs=[in_blk] * 6 + [_const_spec(lev.shape), _const_spec(tri.shape)],
        out_specs=out_blk,
        out_shape=jax.ShapeDtypeStruct((b, t, d), F32),
        scratch_shapes=[pltpu.VMEM((LANES, LANES), F32),
                        pltpu.VMEM((rows, LANES), F32), pltpu.VMEM((rows, LANES), F32),
                        pltpu.VMEM((2 * groups, LANES, LANES), F32),
                        pltpu.VMEM((2 * groups, LANES, LANES), F32)],
        compiler_params=_params("arbitrary"),
        name="rwkv7_chunk_scan",
    )(r, lw, k, v, kk, a, lev, tri)


def _rwkv_out_kernel(y_ref, r_ref, k_ref, v_ref, g_ref, x_ref, mod_ref, vec_ref, bd_ref, w_ref,
                     lng_ref, lnb_ref, o_ref, act_ref, *, d):
    bd = bd_ref[...]
    vec = vec_ref[...]
    inv = 1.0 / HEAD_DIM
    for j in range(d // LANES):
        sl = slice(j * LANES, (j + 1) * LANES)
        y = y_ref[0, :, sl]
        mean = _dot_exact_rhs(y, bd) * inv
        yc = y - mean
        var = _dot_exact_rhs(yc * yc, bd) * inv
        yn = yc * lax.rsqrt(var + GN_EPS) * vec[0:1, sl] + vec[1:2, sl]
        r = r_ref[0, :, sl].astype(F32)
        k = k_ref[0, :, sl].astype(F32)
        bonus = _dot_exact_rhs(r * k * vec[2:3, sl], bd) * v_ref[0, :, sl].astype(F32)
        act_ref[:, sl] = ((yn + bonus) * g_ref[0, :, sl].astype(F32)).astype(BF16)
    out = jnp.dot(act_ref[...], w_ref[...], preferred_element_type=F32)
    z = DEEPNORM_ALPHA * x_ref[0] + mod_ref[0][2:3] * out
    o_ref[0] = _layer_norm(z, lng_ref[...], lnb_ref[...])


def _rwkv_out(y, r, k, v, g, x, mod, vec, w, ln_g, ln_b, tm=512):
    b, t, d = x.shape
    idx = jnp.arange(LANES) // HEAD_DIM
    bd = (idx[:, None] == idx[None, :]).astype(BF16)
    row_spec = pl.BlockSpec((1, tm, d), lambda i, j: (i, j, 0))
    return pl.pallas_call(
        functools.partial(_rwkv_out_kernel, d=d),
        grid=(b, t // tm),
        in_specs=[row_spec] * 6
        + [pl.BlockSpec((1, 6, d), lambda i, j: (i, 0, 0)),
           _const_spec(vec.shape), _const_spec(bd.shape), _const_spec(w.shape),
           _const_spec((1, d)), _const_spec((1, d))],
        out_specs=row_spec,
        out_shape=jax.ShapeDtypeStruct((b, t, d), F32),
        scratch_shapes=[pltpu.VMEM((tm, d), BF16)],
        compiler_params=_params("parallel", "parallel"),
        name="rwkv_out_proj_ln",
    )(y, r, k, v, g, x, mod, vec, bd, w, ln_g.reshape(1, d), ln_b.reshape(1, d))


def _trunk(x, c, ada_w, ada_b, ln_g, ln_b, ffn_w_in, ffn_w_out, fox_w_in, fox_b_f, fox_q_g, fox_k_g, fox_w_o, rwkv_mu, rwkv_w_rkv, rwkv_w0, rwkv_w1, rwkv_w2, rwkv_a0, rwkv_a1, rwkv_a2, rwkv_g1, rwkv_g2, rwkv_k_k, rwkv_k_a, rwkv_r_k, rwkv_lnx_g, rwkv_lnx_b, rwkv_w_o,
           *, tm, tm_rwkv, tq, groups, tn, nc, fc):
    b, t, d = x.shape
    n_heads = d // HEAD_DIM
    mods = _mods(c, ada_w, ada_b, tn=tn)

    w_in = fox_w_in[0]
    f_lo = 3 * d
    w_qkvo = jnp.concatenate([w_in[:, :f_lo], w_in[:, f_lo + n_heads:]], axis=1).astype(BF16)
    w_f = jnp.pad(w_in[:, f_lo:f_lo + n_heads], ((0, 0), (0, LANES - n_heads))).astype(BF16)
    b_f = jnp.pad(fox_b_f[0], (0, LANES - n_heads)).reshape(1, LANES)
    q, k, v, og, f = _fox_in(x, mods[0], w_qkvo, w_f, b_f, fox_q_g[0], fox_k_g[0], tm=tm, nc=nc)
    att = _fox_attention(q, k, v, og, f, tq=tq)
    x = _proj_ln(att, x, mods[0], fox_w_o[0].astype(BF16), ln_g[0, 0], ln_b[0, 0], tm=tm)
    x = _ffn(x, mods[0], ffn_w_in[0].astype(BF16), ffn_w_out[0].astype(BF16), ln_g[0, 1], ln_b[0, 1],
             tm=tm, fc=fc)

    vec_in = jnp.stack([rwkv_w0[0], rwkv_a0[0], rwkv_k_k[0], rwkv_k_a[0]])
    r, lw, k, v, kk, a, g = _rwkv_in(
        x, mods[1], rwkv_mu[0], rwkv_w_rkv[0].astype(BF16),
        rwkv_w1[0].astype(BF16), rwkv_w2[0].astype(BF16),
        rwkv_a1[0].astype(BF16), rwkv_a2[0].astype(BF16),
        rwkv_g1[0].astype(BF16), rwkv_g2[0].astype(BF16), vec_in, tm=tm_rwkv)
    y = _rwkv_scan(r, lw, k, v, kk, a, groups=groups)
    vec_out = jnp.stack([rwkv_lnx_g[0], rwkv_lnx_b[0], rwkv_r_k[0].reshape(d)])
    x = _rwkv_out(y, r, k, v, g, x, mods[1], vec_out, rwkv_w_o[0].astype(BF16), ln_g[1, 0], ln_b[1, 0],
                  tm=tm)
    x = _ffn(x, mods[1], ffn_w_in[1].astype(BF16), ffn_w_out[1].astype(BF16), ln_g[1, 1], ln_b[1, 1],
             tm=tm, fc=fc)
    return x


def kernel(x, c, ada_w, ada_b, ln_g, ln_b, ffn_w_in, ffn_w_out, fox_w_in, fox_b_f, fox_q_g, fox_k_g, fox_w_o, rwkv_mu, rwkv_w_rkv, rwkv_w0, rwkv_w1, rwkv_w2, rwkv_a0, rwkv_a1, rwkv_a2, rwkv_g1, rwkv_g2, rwkv_k_k, rwkv_k_a, rwkv_r_k, rwkv_lnx_g, rwkv_lnx_b, rwkv_w_o):
    return _trunk(x, c, ada_w, ada_b, ln_g, ln_b, ffn_w_in, ffn_w_out, fox_w_in, fox_b_f, fox_q_g, fox_k_g, fox_w_o, rwkv_mu, rwkv_w_rkv, rwkv_w0, rwkv_w1, rwkv_w2, rwkv_a0, rwkv_a1, rwkv_a2, rwkv_g1, rwkv_g2, rwkv_k_k, rwkv_k_a, rwkv_r_k, rwkv_lnx_g, rwkv_lnx_b, rwkv_w_o,
                  tm=512, tm_rwkv=256, tq=512, groups=4, tn=1536, nc=512, fc=256)
```

```python
import functools

import jax
import jax.numpy as jnp
import numpy as np
from jax import lax
from jax.experimental import pallas as pl
from jax.experimental.pallas import tpu as pltpu

F32 = jnp.float32
BF16 = jnp.bfloat16

HEAD_DIM = 64
DEPTH = 2
DEEPNORM_ALPHA = (2 * DEPTH) ** 0.25
LN_EPS = 1e-5
QK_EPS = 1e-6
GN_EPS = HEAD_DIM * 1e-5
LANES = 128
CHUNK = 64
NEG_BIG = -1e30
LOG2E = 1.4426950408889634
N_SPLIT = 3
VMEM_LIMIT = 56 * 1024 * 1024


def _dot(a, b):
    return jnp.dot(a.astype(BF16), b.astype(BF16), preferred_element_type=F32)


def _dot_nt(a, b):
    return lax.dot_general(a.astype(BF16), b.astype(BF16), (((1,), (1,)), ((), ())),
                           preferred_element_type=F32)


def _dot_tn(a, b):
    return lax.dot_general(a.astype(BF16), b.astype(BF16), (((0,), (0,)), ((), ())),
                           preferred_element_type=F32)


def _split2(a):
    hi = a.astype(BF16)
    lo = (a - hi.astype(F32)).astype(BF16)
    return hi, lo


def _split3(a):
    hi = a.astype(BF16)
    r1 = a - hi.astype(F32)
    mid = r1.astype(BF16)
    lo = (r1 - mid.astype(F32)).astype(BF16)
    return hi, mid, lo


def _dot3(a, b):
    ah, al = _split2(a)
    bh, bl = _split2(b)
    d = functools.partial(jnp.dot, preferred_element_type=F32)
    return d(ah, bh) + (d(al, bh) + d(ah, bl))


def _dot_exact_rhs(a, b_exact):
    ah, al = _split2(a)
    d = functools.partial(jnp.dot, preferred_element_type=F32)
    return d(ah, b_exact) + d(al, b_exact)


def _layer_norm(z, g, b):
    mu = jnp.mean(z, axis=-1, keepdims=True)
    zc = z - mu
    var = jnp.mean(zc * zc, axis=-1, keepdims=True)
    return zc * lax.rsqrt(var + LN_EPS) * g + b


def _softplus(z):
    return jnp.maximum(z, 0.0) + jnp.log(1.0 + jnp.exp(-jnp.abs(z)))


def _params(*sem):
    return pltpu.CompilerParams(dimension_semantics=sem, vmem_limit_bytes=VMEM_LIMIT)


def _const_spec(shape):
    nd = len(shape)
    return pl.BlockSpec(shape, lambda *_: (0,) * nd, pipeline_mode=pl.Buffered(1))


def _mods_kernel(c_ref, w_ref, b_ref, o_ref):
    c = c_ref[...]
    ca = c * jax.nn.sigmoid(c)
    o_ref[0] = _dot3(ca, w_ref[0]) + b_ref[0]


def _mods(c, ada_w, ada_b, tn=1536):
    depth, d, n = ada_w.shape
    b = c.shape[0]
    rows = 8
    cp = jnp.pad(c, ((0, rows - b), (0, 0)))
    out = pl.pallas_call(
        _mods_kernel,
        grid=(depth, n // tn),
        in_specs=[pl.BlockSpec((rows, d), lambda l, j: (0, 0)),
                  pl.BlockSpec((1, d, tn), lambda l, j: (l, 0, j)),
                  pl.BlockSpec((1, 1, tn), lambda l, j: (l, 0, j))],
        out_specs=pl.BlockSpec((1, rows, tn), lambda l, j: (l, 0, j)),
        out_shape=jax.ShapeDtypeStruct((depth, rows, n), F32),
        compiler_params=_params("parallel", "parallel"),
        name="adaln_mods",
    )(cp, ada_w, ada_b.reshape(depth, 1, n))
    return out[:, :b].reshape(depth, b, 6, d)


def _fox_in_kernel(x_ref, mod_ref, w_ref, wf_ref, bf_ref, qg_ref, kg_ref, bd_ref, tri_ref,
                   pq_ref, pk_ref, oq_ref, ok_ref, wvt_ref,
                   q_ref, k_ref, o_ref, vt_ref, carry_ref, *, d, nc):
    t = pl.program_id(1)
    x = x_ref[0]
    mod = mod_ref[0]
    h = (x * (1.0 + mod[1:2]) + mod[0:1]).astype(BF16)
    bd = bd_ref[...]
    qg = qg_ref[...]
    kg = kg_ref[...]
    dd = functools.partial(jnp.dot, preferred_element_type=F32)

    fl = dd(h, wf_ref[...]) + bf_ref[...]
    lf = jnp.minimum(fl, 0.0) - jnp.log(1.0 + jnp.exp(-jnp.abs(fl)))
    tri = tri_ref[...]
    hi, mid, lo = _split3(lf)
    cs = dd(tri, hi) + (dd(tri, mid) + dd(tri, lo))

    @pl.when(t == 0)
    def _():
        carry_ref[...] = jnp.zeros_like(carry_ref)

    f = cs + carry_ref[0:1, :]
    tm = f.shape[0]
    carry_ref[...] = jnp.broadcast_to(f[tm - 1:tm, :], carry_ref.shape)
    fcat = jnp.concatenate(_split3(f * LOG2E), axis=1)
    low = lax.broadcasted_iota(jnp.int32, (tm, LANES), 1) < HEAD_DIM

    def head_rms(zj, g):
        ss = _dot_exact_rhs(zj * zj, bd)
        return zj * lax.rsqrt(ss * (1.0 / HEAD_DIM) + QK_EPS) * g

    def store_augmented(out_ref, zj, tile, p_ref, ones_ref):
        sl = slice(2 * tile * LANES, (2 * tile + 2) * LANES)
        aug = dd(fcat, p_ref[:, sl]) + ones_ref[:, sl]
        out_ref[0, :, sl] = jnp.concatenate(
            [jnp.where(low, zj, aug[:, :LANES]), jnp.where(low, aug[:, LANES:], zj)],
            axis=1).astype(out_ref.dtype)

    for sec, out_ref in enumerate((q_ref, k_ref, o_ref)):
        for c in range(d // nc):
            col = sec * d + c * nc
            z = dd(h, w_ref[:, col:col + nc])
            for j in range(nc // LANES):
                zj = z[:, j * LANES:(j + 1) * LANES]
                tile = c * (nc // LANES) + j
                if sec == 0:
                    store_augmented(out_ref, head_rms(zj, qg), tile, pq_ref, oq_ref)
                elif sec == 1:
                    store_augmented(out_ref, head_rms(zj, kg), tile, pk_ref, ok_ref)
                else:
                    out_ref[0, :, tile * LANES:(tile + 1) * LANES] = (
                        jax.nn.sigmoid(zj).astype(out_ref.dtype))
    for c in range(d // nc):
        vt_ref[0, c * nc:(c + 1) * nc, :] = lax.dot_general(
            wvt_ref[c * nc:(c + 1) * nc, :], h, (((1,), (1,)), ((), ())),
            preferred_element_type=F32).astype(vt_ref.dtype)


def _fox_in(x, mod, w_qko, w_vt, w_f, b_f, q_g, k_g, tm=512, nc=512):
    b, t, d = x.shape
    n_heads = d // HEAD_DIM
    heads_per_tile = LANES // HEAD_DIM
    qg = jnp.tile(q_g * (HEAD_DIM ** -0.5 * LOG2E), heads_per_tile).reshape(1, LANES)
    kg = jnp.tile(k_g, heads_per_tile).reshape(1, LANES)
    idx = jnp.arange(LANES) // HEAD_DIM
    bd = (idx[:, None] == idx[None, :]).astype(BF16)
    tri = (jnp.arange(tm)[:, None] >= jnp.arange(tm)[None, :]).astype(BF16)
    pq = np.zeros((N_SPLIT * LANES, n_heads * LANES), np.float32)
    pk = np.zeros_like(pq)
    oq = np.zeros((1, n_heads * LANES), np.float32)
    ok = np.zeros_like(oq)
    for hd in range(n_heads):
        base = hd * LANES + (HEAD_DIM if hd % 2 == 0 else 0)
        for part in range(N_SPLIT):
            pq[part * LANES + hd, base + N_SPLIT + part] = 1.0
            ok[0, base + N_SPLIT + part] = 1.0
            pk[part * LANES + hd, base + part] = -1.0
            oq[0, base + part] = 1.0
    pq = jnp.asarray(pq, BF16)
    pk = jnp.asarray(pk, BF16)
    oq = jnp.asarray(oq)
    ok = jnp.asarray(ok)
    act = jax.ShapeDtypeStruct((b, t, d), BF16)
    aug = jax.ShapeDtypeStruct((b, t, n_heads * LANES), BF16)
    row_spec = pl.BlockSpec((1, tm, d), lambda i, j: (i, j, 0))
    aug_spec = pl.BlockSpec((1, tm, n_heads * LANES), lambda i, j: (i, j, 0))
    kern = functools.partial(_fox_in_kernel, d=d, nc=nc)
    consts = (w_qko, w_f, b_f, qg, kg, bd, tri, pq, pk, oq, ok, w_vt)
    return pl.pallas_call(
        kern,
        grid=(b, t // tm),
        in_specs=[row_spec, pl.BlockSpec((1, 6, d), lambda i, j: (i, 0, 0))]
        + [_const_spec(c.shape) for c in consts],
        out_specs=[aug_spec, aug_spec, row_spec, pl.BlockSpec((1, d, tm), lambda i, j: (i, 0, j))],
        out_shape=[aug, aug, act, jax.ShapeDtypeStruct((b, d, t), BF16)],
        scratch_shapes=[pltpu.VMEM((8, LANES), F32)],
        compiler_params=_params("parallel", "arbitrary"),
        name="fox_in_proj",
    )(x, mod, *consts)


def _attn_kernel(q_ref, k_ref, vt_ref, g_ref, y_ref, *, tq, tk):
    qi = pl.program_id(2)
    heads = LANES // HEAD_DIM
    pair = 2 * tk
    key = lax.broadcasted_iota(jnp.int32, (tk, tq), 0)
    qry = lax.broadcasted_iota(jnp.int32, (tk, tq), 1) + qi * tq
    qs = [q_ref[0, :, hh * LANES:(hh + 1) * LANES] for hh in range(heads)]

    def step(j, carry, masked):
        starts = [pl.multiple_of(j * pair + bb * tk, tk) for bb in range(2)]

        def scores(hh, bb):
            z = _dot_nt(k_ref[0, pl.ds(starts[bb], tk), hh * LANES:(hh + 1) * LANES], qs[hh])
            if masked:
                z = jnp.where(key + starts[bb] <= qry, z, NEG_BIG)
            return z

        def weights(z, m):
            p = jnp.exp2(z - m)
            return p, jnp.sum(p, axis=0, keepdims=True)

        def values(hh, bb, p):
            vt = vt_ref[0, hh * HEAD_DIM:(hh + 1) * HEAD_DIM, pl.ds(starts[bb], tk)]
            return jnp.dot(vt, p.astype(BF16), preferred_element_type=F32)

        colmax = lambda z: jnp.max(z, axis=0, keepdims=True)
        (m0, l0, a0), (m1, l1, a1) = carry
        z00 = scores(0, 0)
        z10 = scores(1, 0)
        m0a = jnp.maximum(m0, colmax(z00))
        z01 = scores(0, 1)
        p00, s00 = weights(z00, m0a)
        m1a = jnp.maximum(m1, colmax(z10))
        z11 = scores(1, 1)
        v00 = values(0, 0, p00)
        p10, s10 = weights(z10, m1a)
        m0b = jnp.maximum(m0a, colmax(z01))
        v10 = values(1, 0, p10)
        p01, s01 = weights(z01, m0b)
        m1b = jnp.maximum(m1a, colmax(z11))
        v01 = values(0, 1, p01)
        p11, s11 = weights(z11, m1b)
        v11 = values(1, 1, p11)

        def merge(m, l, a, ma, mb, sa, sb, va, vb):
            ra = jnp.exp2(m - ma)
            rb = jnp.exp2(ma - mb)
            return mb, rb * (ra * l + sa) + sb, rb * (ra * a + va) + vb

        return (merge(m0, l0, a0, m0a, m0b, s00, s01, v00, v01),
                merge(m1, l1, a1, m1a, m1b, s10, s11, v10, v11))

    init = tuple((jnp.full((1, tq), NEG_BIG, F32), jnp.zeros((1, tq), F32),
                  jnp.zeros((HEAD_DIM, tq), F32)) for _ in range(heads))
    n_full = (qi * tq) // pair
    carry = lax.fori_loop(0, n_full, functools.partial(step, masked=False), init)
    for jm in range(tq // pair):
        carry = step(n_full + jm, carry, True)
    yt = jnp.concatenate([carry[hh][2] / carry[hh][1] for hh in range(heads)], axis=0)
    y_ref[0] = (yt.T * g_ref[0].astype(F32)).astype(y_ref.dtype)


def _fox_attention(q_aug, k_aug, vt, og, tq=512, tk=512):
    b, d, t = vt.shape
    heads = LANES // HEAD_DIM
    blk = pl.BlockSpec((1, tq, LANES), lambda i, p, j: (i, j, p))
    q_blk = pl.BlockSpec((1, tq, heads * LANES), lambda i, p, j: (i, j, p))
    k_full = pl.BlockSpec((1, t, heads * LANES), lambda i, p, j: (i, 0, p))
    vt_full = pl.BlockSpec((1, LANES, t), lambda i, p, j: (i, p, 0))
    return pl.pallas_call(
        functools.partial(_attn_kernel, tq=tq, tk=tk),
        grid=(b, d // LANES, t // tq),
        in_specs=[q_blk, k_full, vt_full, blk],
        out_specs=blk,
        out_shape=jax.ShapeDtypeStruct((b, t, d), BF16),
        compiler_params=_params("parallel", "parallel", "arbitrary"),
        name="fox_attention",
    )(q_aug, k_aug, vt, og)


def _proj_ln_kernel(a_ref, x_ref, mod_ref, w_ref, lng_ref, lnb_ref, o_ref):
    y = jnp.dot(a_ref[0], w_ref[...], preferred_element_type=F32)
    z = DEEPNORM_ALPHA * x_ref[0] + mod_ref[0][2:3] * y
    o_ref[0] = _layer_norm(z, lng_ref[...], lnb_ref[...])


def _proj_ln(a, x, mod, w, ln_g, ln_b, tm=512):
    b, t, d = x.shape
    row_spec = pl.BlockSpec((1, tm, d), lambda i, j: (i, j, 0))
    return pl.pallas_call(
        _proj_ln_kernel,
        grid=(b, t // tm),
        in_specs=[row_spec, row_spec,
                  pl.BlockSpec((1, 6, d), lambda i, j: (i, 0, 0)),
                  _const_spec(w.shape), _const_spec((1, d)), _const_spec((1, d))],
        out_specs=row_spec,
        out_shape=jax.ShapeDtypeStruct((b, t, d), F32),
        compiler_params=_params("parallel", "parallel"),
        name="attn_out_proj_ln",
    )(a, x, mod, w, ln_g.reshape(1, d), ln_b.reshape(1, d))


def _ffn_kernel(x_ref, mod_ref, win_ref, wout_ref, lng_ref, lnb_ref, o_ref, act_ref, *, d_ff, fc):
    x = x_ref[0]
    mod = mod_ref[0]
    h = (x * (1.0 + mod[4:5]) + mod[3:4]).astype(BF16)
    for c in range(d_ff // fc):
        g = jnp.dot(h, win_ref[:, c * fc:(c + 1) * fc], preferred_element_type=F32)
        u = jnp.dot(h, win_ref[:, d_ff + c * fc:d_ff + (c + 1) * fc], preferred_element_type=F32)
        act_ref[:, c * fc:(c + 1) * fc] = (g * jax.nn.sigmoid(g) * u).astype(BF16)
    y = jnp.dot(act_ref[...], wout_ref[...], preferred_element_type=F32)
    z = DEEPNORM_ALPHA * x + mod[5:6] * y
    o_ref[0] = _layer_norm(z, lng_ref[...], lnb_ref[...])


def _ffn(x, mod, w_in, w_out, ln_g, ln_b, tm=512, fc=256):
    b, t, d = x.shape
    d_ff = w_out.shape[0]
    row_spec = pl.BlockSpec((1, tm, d), lambda i, j: (i, j, 0))
    return pl.pallas_call(
        functools.partial(_ffn_kernel, d_ff=d_ff, fc=fc),
        grid=(b, t // tm),
        in_specs=[row_spec,
                  pl.BlockSpec((1, 6, d), lambda i, j: (i, 0, 0)),
                  _const_spec(w_in.shape), _const_spec(w_out.shape),
                  _const_spec((1, d)), _const_spec((1, d))],
        out_specs=row_spec,
        out_shape=jax.ShapeDtypeStruct((b, t, d), F32),
        scratch_shapes=[pltpu.VMEM((tm, d_ff), BF16)],
        compiler_params=_params("parallel", "parallel"),
        name="swiglu_ln",
    )(x, mod, w_in, w_out, ln_g.reshape(1, d), ln_b.reshape(1, d))


def _rwkv_in_kernel(x_ref, xp_ref, mod_ref, mu_ref, wrkv_ref, w1_ref, w2_ref, a1_ref, a2_ref,
                    g1_ref, g2_ref, vec_ref, bd_ref,
                    r_ref, lw_ref, k_ref, v_ref, kk_ref, a_ref, g_ref, *, d):
    t = pl.program_id(1)
    mod = mod_ref[0]
    sc = 1.0 + mod[1:2]
    sh = mod[0:1]
    h = x_ref[0] * sc + sh
    tm = h.shape[0]
    prev = xp_ref[0][7:8, :] * sc + sh
    prev = jnp.where(t == 0, jnp.zeros_like(prev), prev)
    rows = lax.broadcasted_iota(jnp.int32, h.shape, 0)
    hprev = jnp.where(rows == 0, prev, pltpu.roll(h, 1, axis=0))
    dx = hprev - h
    mu = mu_ref[...]
    vec = vec_ref[...]
    w0, a0, k_k, k_a = vec[0:1], vec[1:2], vec[2:3], vec[3:4]

    def mix(n):
        return (h + dx * mu[n:n + 1]).astype(BF16)

    dd = functools.partial(jnp.dot, preferred_element_type=F32)
    r = dd(mix(0), wrkv_ref[0])
    k = dd(mix(1), wrkv_ref[1])
    v = dd(mix(2), wrkv_ref[2])
    ww = w0 + dd(jnp.tanh(dd(mix(3), w1_ref[...])).astype(BF16), w2_ref[...])
    lw = -jnp.exp(-_softplus(-ww) - 0.5)
    a = jax.nn.sigmoid(a0 + dd(dd(mix(4), a1_ref[...]).astype(BF16), a2_ref[...]))
    g = dd(jax.nn.sigmoid(dd(mix(5), g1_ref[...])).astype(BF16), g2_ref[...])
    kk = k * k_k
    k = k * (1.0 + (a - 1.0) * k_a)
    bd = bd_ref[...]
    for j in range(d // LANES):
        sl = slice(j * LANES, (j + 1) * LANES)
        kkj = kk[:, sl]
        ss = _dot_exact_rhs(kkj * kkj, bd)
        kk_ref[0, :, sl] = (kkj / jnp.maximum(jnp.sqrt(ss), 1e-12)).astype(kk_ref.dtype)
    r_ref[0] = r.astype(r_ref.dtype)
    lw_ref[0] = lw
    k_ref[0] = k.astype(k_ref.dtype)
    v_ref[0] = v.astype(v_ref.dtype)
    a_ref[0] = a.astype(a_ref.dtype)
    g_ref[0] = g.astype(g_ref.dtype)


def _rwkv_in(x, mod, mu, w_rkv, w1, w2, a1, a2, g1, g2, vec, tm=512):
    b, t, d = x.shape
    idx = jnp.arange(LANES) // HEAD_DIM
    bd = (idx[:, None] == idx[None, :]).astype(BF16)
    row_spec = pl.BlockSpec((1, tm, d), lambda i, j: (i, j, 0))
    prev_spec = pl.BlockSpec((1, 8, d), lambda i, j: (i, jnp.maximum(j * (tm // 8) - 1, 0), 0))
    act = jax.ShapeDtypeStruct((b, t, d), BF16)
    consts = (mu, w_rkv, w1, w2, a1, a2, g1, g2, vec, bd)
    return pl.pallas_call(
        functools.partial(_rwkv_in_kernel, d=d),
        grid=(b, t // tm),
        in_specs=[row_spec, prev_spec, pl.BlockSpec((1, 6, d), lambda i, j: (i, 0, 0))]
        + [_const_spec(c.shape) for c in consts],
        out_specs=[row_spec] * 7,
        out_shape=[act, jax.ShapeDtypeStruct((b, t, d), F32), act, act, act, act, act],
        compiler_params=_params("parallel", "parallel"),
        name="rwkv_in_proj",
    )(x, x, mod, *consts)


def _rwkv_scan_kernel(r_ref, lw_ref, k_ref, v_ref, kk_ref, a_ref, lev_ref, tri_ref,
                      y_ref, h_ref, yi_ref, rq_ref, m_ref, n_ref, *, groups, steps_per_seq):
    s = pl.program_id(0)
    gr = 2 * CHUNK
    st = 2 * gr
    n_lev = CHUNK.bit_length() - 1

    @pl.when(s == 0)
    def _():
        h_ref[...] = jnp.zeros_like(h_ref)
        yi_ref[...] = jnp.zeros_like(yi_ref)
        rq_ref[...] = jnp.zeros_like(rq_ref)
        m_ref[...] = jnp.zeros_like(m_ref)
        n_ref[...] = jnp.zeros_like(n_ref)

    first_of_seq = lax.rem(jnp.maximum(s - 1, 0), steps_per_seq) == 0
    state = [jnp.where(first_of_seq, 0.0, h_ref[...])]
    pending = list(range(2 * groups))

    def state_steps(count):
        for _ in range(count):
            if pending:
                i = pending.pop(0)
                lo = i * CHUNK
                h = state[0]
                y_ref[0, lo:lo + CHUNK, :] = (yi_ref[lo:lo + CHUNK, :]
                                              + _dot(rq_ref[lo:lo + CHUNK, :], h))
                state[0] = _dot3(m_ref[i], h) + n_ref[i]

    lane = lax.broadcasted_iota(jnp.int32, (gr, LANES), 1)
    head0 = lane < HEAD_DIM
    lev = lev_ref[...]
    tri = tri_ref[...]
    strict = lev >= 0
    srow = lax.broadcasted_iota(jnp.int32, (st, st), 0)
    scol = lax.broadcasted_iota(jnp.int32, (st, st), 1)
    eye_st = srow == scol
    incl = strict | eye_st
    hrow = lax.broadcasted_iota(jnp.int32, (LANES, LANES), 0)
    hcol = lax.broadcasted_iota(jnp.int32, (LANES, LANES), 1)
    same_head = (hrow < HEAD_DIM) == (hcol < HEAD_DIM)
    eye_h = hrow == hcol

    def stack(x):
        zero = jnp.zeros_like(x)
        return jnp.concatenate([jnp.where(head0, x, zero), jnp.where(head0, zero, x)], axis=0)

    def unstack(x):
        return x[:gr] + x[gr:]

    def prep(g):
        rs = slice(g * gr, (g + 1) * gr)
        lw = lw_ref[0, rs, :]
        hi, mid, lo = _split3(lw)
        dd = functools.partial(jnp.dot, preferred_element_type=F32)
        cum = dd(tri, hi) + (dd(tri, mid) + dd(tri, lo))
        clast = jnp.concatenate(
            [jnp.broadcast_to(cum[(j + 1) * CHUNK - 1:(j + 1) * CHUNK], (CHUNK, LANES))
             for j in range(2)], axis=0)
        g_inv = jnp.exp(-cum)
        g_rem = jnp.exp(clast - cum)
        r = r_ref[0, rs, :].astype(F32)
        k = k_ref[0, rs, :].astype(F32)
        v = v_ref[0, rs, :].astype(F32)
        kk = kk_ref[0, rs, :].astype(F32)
        bt = kk * a_ref[0, rs, :].astype(F32)
        a_st = stack(-kk * jnp.exp(cum - lw))
        r_st = stack(r * jnp.exp(cum))
        lhs = jnp.concatenate([a_st, r_st], axis=0)
        rhs = jnp.concatenate([stack(bt * g_inv), stack(k * g_inv)], axis=0)
        aa = _dot_nt(lhs, rhs)
        return dict(
            v=v, v_st=stack(v), a_st=a_st, r_st=r_st, b_rem=bt * g_rem, k_rem=k * g_rem,
            g_last=jnp.exp(clast),
            a_ab=jnp.where(strict, aa[:st, :st], 0.0), a_ak=jnp.where(strict, aa[:st, st:], 0.0),
            a_rb=jnp.where(incl, aa[st:, :st], 0.0), a_rk=jnp.where(incl, aa[st:, st:], 0.0))

    per_stage = -(-2 * groups // (n_lev + 2))
    ps = [prep(g) for g in range(groups)]
    gs = range(groups)
    state_steps(per_stage)

    xs = [jnp.where(eye_st, 1.0, jnp.where(lev == 0, p["a_ab"], 0.0)) for p in ps]
    for level in range(1, n_lev):
        ws = [_dot(jnp.where(lev == level, ps[g]["a_ab"], 0.0), xs[g]) for g in gs]
        xs = [xs[g] + _dot(xs[g], ws[g]) for g in gs]
        state_steps(per_stage)
    res = [jnp.where(eye_st, 1.0, 0.0) - xs[g] + _dot3(ps[g]["a_ab"], xs[g]) for g in gs]
    state_steps(per_stage)
    xs = [xs[g] + _dot(xs[g], res[g]) for g in gs]

    gm = [jnp.concatenate([_dot(ps[g]["a_ak"], ps[g]["v_st"]), ps[g]["a_st"]], axis=1) for g in gs]
    state_steps(2 * groups)
    h_ref[...] = state[0]
    tg = [_dot(xs[g], gm[g]) for g in gs]
    rb = [_dot(ps[g]["a_rb"], tg[g]) for g in gs]
    rk = [_dot(ps[g]["a_rk"], ps[g]["v_st"]) for g in gs]
    for g in gs:
        p = ps[g]
        yi_ref[g * gr:(g + 1) * gr, :] = unstack(rb[g][:, :LANES] + rk[g])
        rq_ref[g * gr:(g + 1) * gr, :] = unstack(p["r_st"] + rb[g][:, LANES:])
        uv = unstack(tg[g][:, :LANES])
        wa = unstack(tg[g][:, LANES:])
        for j in range(2):
            cs = slice(j * CHUNK, (j + 1) * CHUNK)
            lhs_t = jnp.concatenate([p["b_rem"][cs], p["k_rem"][cs]], axis=0)
            rhs_t = jnp.concatenate(
                [jnp.concatenate([uv[cs], wa[cs]], axis=1),
                 jnp.concatenate([p["v"][cs], jnp.zeros_like(p["v"][cs])], axis=1)], axis=0)
            nm = _dot_tn(lhs_t, rhs_t)
            n_ref[2 * g + j] = jnp.where(same_head, nm[:, :LANES], 0.0)
            decay = jnp.broadcast_to(p["g_last"][j * CHUNK:j * CHUNK + 1], (LANES, LANES))
            m_ref[2 * g + j] = (jnp.where(same_head, nm[:, LANES:], 0.0)
                                + jnp.where(eye_h, decay, 0.0))


def _rwkv_scan(r, lw, k, v, kk, a, groups=2):
    b, t, d = r.shape
    gr = 2 * CHUNK
    st = 2 * gr
    rows = gr * groups
    n_hp = d // LANES
    steps_per_seq = t // rows
    n_blocks = b * n_hp * steps_per_seq

    def block_index(blk):
        seq = blk // steps_per_seq
        return seq // n_hp, blk % steps_per_seq, seq % n_hp

    in_blk = pl.BlockSpec((1, rows, LANES), lambda s: block_index(jnp.minimum(s, n_blocks - 1)))
    out_blk = pl.BlockSpec((1, rows, LANES), lambda s: block_index(jnp.maximum(s - 1, 0)))
    idx = jnp.arange(st)
    xor = idx[:, None] ^ idx[None, :]
    same = (idx[:, None] // CHUNK) == (idx[None, :] // CHUNK)
    lower = idx[None, :] < idx[:, None]
    msb = jnp.floor(jnp.log2(jnp.maximum(xor, 1).astype(F32))).astype(jnp.int32)
    lev = jnp.where(same & lower, msb, -1).astype(jnp.int32)
    ti = jnp.arange(gr)
    tri = ((ti[:, None] >= ti[None, :]) & ((ti[:, None] // CHUNK) == (ti[None, :] // CHUNK))).astype(BF16)
    return pl.pallas_call(
        functools.partial(_rwkv_scan_kernel, groups=groups, steps_per_seq=steps_per_seq),
        grid=(n_blocks + 1,),
        in_specs=[in_blk] * 6 + [_const_spec(lev.shape), _const_spec(tri.shape)],
        out_specs=out_blk,
        out_shape=jax.ShapeDtypeStruct((b, t, d), F32),
        scratch_shapes=[pltpu.VMEM((LANES, LANES), F32),
                        pltpu.VMEM((rows, LANES), F32), pltpu.VMEM((rows, LANES), F32),
                        pltpu.VMEM((2 * groups, LANES, LANES), F32),
                        pltpu.VMEM((2 * groups, LANES, LANES), F32)],
        compiler_params=_params("arbitrary"),
        name="rwkv7_chunk_scan",
    )(r, lw, k, v, kk, a, lev, tri)


def _rwkv_out_kernel(y_ref, r_ref, k_ref, v_ref, g_ref, x_ref, mod_ref, vec_ref, bd_ref, w_ref,
                     lng_ref, lnb_ref, o_ref, act_ref, *, d):
    bd = bd_ref[...]
    vec = vec_ref[...]
    inv = 1.0 / HEAD_DIM
    for j in range(d // LANES):
        sl = slice(j * LANES, (j + 1) * LANES)
        y = y_ref[0, :, sl]
        mean = _dot_exact_rhs(y, bd) * inv
        yc = y - mean
        var = _dot_exact_rhs(yc * yc, bd) * inv
        yn = yc * lax.rsqrt(var + GN_EPS) * vec[0:1, sl] + vec[1:2, sl]
        r = r_ref[0, :, sl].astype(F32)
        k = k_ref[0, :, sl].astype(F32)
        bonus = _dot_exact_rhs(r * k * vec[2:3, sl], bd) * v_ref[0, :, sl].astype(F32)
        act_ref[:, sl] = ((yn + bonus) * g_ref[0, :, sl].astype(F32)).astype(BF16)
    out = jnp.dot(act_ref[...], w_ref[...], preferred_element_type=F32)
    z = DEEPNORM_ALPHA * x_ref[0] + mod_ref[0][2:3] * out
    o_ref[0] = _layer_norm(z, lng_ref[...], lnb_ref[...])


def _rwkv_out(y, r, k, v, g, x, mod, vec, w, ln_g, ln_b, tm=512):
    b, t, d = x.shape
    idx = jnp.arange(LANES) // HEAD_DIM
    bd = (idx[:, None] == idx[None, :]).astype(BF16)
    row_spec = pl.BlockSpec((1, tm, d), lambda i, j: (i, j, 0))
    return pl.pallas_call(
        functools.partial(_rwkv_out_kernel, d=d),
        grid=(b, t // tm),
        in_specs=[row_spec] * 6
        + [pl.BlockSpec((1, 6, d), lambda i, j: (i, 0, 0)),
           _const_spec(vec.shape), _const_spec(bd.shape), _const_spec(w.shape),
           _const_spec((1, d)), _const_spec((1, d))],
        out_specs=row_spec,
        out_shape=jax.ShapeDtypeStruct((b, t, d), F32),
        scratch_shapes=[pltpu.VMEM((tm, d), BF16)],
        compiler_params=_params("parallel", "parallel"),
        name="rwkv_out_proj_ln",
    )(y, r, k, v, g, x, mod, vec, bd, w, ln_g.reshape(1, d), ln_b.reshape(1, d))


def _trunk(x, c, ada_w, ada_b, ln_g, ln_b, ffn_w_in, ffn_w_out, fox_w_in, fox_b_f, fox_q_g, fox_k_g, fox_w_o, rwkv_mu, rwkv_w_rkv, rwkv_w0, rwkv_w1, rwkv_w2, rwkv_a0, rwkv_a1, rwkv_a2, rwkv_g1, rwkv_g2, rwkv_k_k, rwkv_k_a, rwkv_r_k, rwkv_lnx_g, rwkv_lnx_b, rwkv_w_o,
           *, tm, tm_rwkv, tq, tk, groups, tn, nc, fc):
    b, t, d = x.shape
    n_heads = d // HEAD_DIM
    mods = _mods(c, ada_w, ada_b, tn=tn)

    w_in = fox_w_in[0]
    f_lo = 3 * d
    w_qko = jnp.concatenate([w_in[:, :2 * d], w_in[:, f_lo + n_heads:]], axis=1).astype(BF16)
    w_vt = w_in[:, 2 * d:f_lo].T.astype(BF16)
    w_f = jnp.pad(w_in[:, f_lo:f_lo + n_heads], ((0, 0), (0, LANES - n_heads))).astype(BF16)
    b_f = jnp.pad(fox_b_f[0], (0, LANES - n_heads)).reshape(1, LANES)
    q, k, og, v = _fox_in(x, mods[0], w_qko, w_vt, w_f, b_f, fox_q_g[0], fox_k_g[0], tm=tm, nc=nc)
    att = _fox_attention(q, k, v, og, tq=tq, tk=tk)
    x = _proj_ln(att, x, mods[0], fox_w_o[0].astype(BF16), ln_g[0, 0], ln_b[0, 0], tm=tm)
    x = _ffn(x, mods[0], ffn_w_in[0].astype(BF16), ffn_w_out[0].astype(BF16), ln_g[0, 1], ln_b[0, 1],
             tm=tm, fc=fc)

    vec_in = jnp.stack([rwkv_w0[0], rwkv_a0[0], rwkv_k_k[0], rwkv_k_a[0]])
    r, lw, k, v, kk, a, g = _rwkv_in(
        x, mods[1], rwkv_mu[0], rwkv_w_rkv[0].astype(BF16),
        rwkv_w1[0].astype(BF16), rwkv_w2[0].astype(BF16),
        rwkv_a1[0].astype(BF16), rwkv_a2[0].astype(BF16),
        rwkv_g1[0].astype(BF16), rwkv_g2[0].astype(BF16), vec_in, tm=tm_rwkv)
    y = _rwkv_scan(r, lw, k, v, kk, a, groups=groups)
    vec_out = jnp.stack([rwkv_lnx_g[0], rwkv_lnx_b[0], rwkv_r_k[0].reshape(d)])
    x = _rwkv_out(y, r, k, v, g, x, mods[1], vec_out, rwkv_w_o[0].astype(BF16), ln_g[1, 0], ln_b[1, 0],
                  tm=tm)
    x = _ffn(x, mods[1], ffn_w_in[1].astype(BF16), ffn_w_out[1].astype(BF16), ln_g[1, 1], ln_b[1, 1],
             tm=tm, fc=fc)
    return x


def kernel(x, c, ada_w, ada_b, ln_g, ln_b, ffn_w_in, ffn_w_out, fox_w_in, fox_b_f, fox_q_g, fox_k_g, fox_w_o, rwkv_mu, rwkv_w_rkv, rwkv_w0, rwkv_w1, rwkv_w2, rwkv_a0, rwkv_a1, rwkv_a2, rwkv_g1, rwkv_g2, rwkv_k_k, rwkv_k_a, rwkv_r_k, rwkv_lnx_g, rwkv_lnx_b, rwkv_w_o):
    return _trunk(x, c, ada_w, ada_b, ln_g, ln_b, ffn_w_in, ffn_w_out, fox_w_in, fox_b_f, fox_q_g, fox_k_g, fox_w_o, rwkv_mu, rwkv_w_rkv, rwkv_w0, rwkv_w1, rwkv_w2, rwkv_a0, rwkv_a1, rwkv_a2, rwkv_g1, rwkv_g2, rwkv_k_k, rwkv_k_a, rwkv_r_k, rwkv_lnx_g, rwkv_lnx_b, rwkv_w_o,
                  tm=512, tm_rwkv=256, tq=512, tk=256, groups=4, tn=1536, nc=512, fc=256)
```

```python
import functools

import jax
import jax.numpy as jnp
import numpy as np
from jax import lax
from jax.experimental import pallas as pl
from jax.experimental.pallas import tpu as pltpu

F32 = jnp.float32
BF16 = jnp.bfloat16

HEAD_DIM = 64
DEPTH = 2
DEEPNORM_ALPHA = (2 * DEPTH) ** 0.25
LN_EPS = 1e-5
QK_EPS = 1e-6
GN_EPS = HEAD_DIM * 1e-5
LANES = 128
CHUNK = 64
NEG_BIG = -1e30
LOG2E = 1.4426950408889634
SAFE_LOGIT_RANGE = 96.0
N_SPLIT = 3
VMEM_LIMIT = 56 * 1024 * 1024


def _dot(a, b):
    return jnp.dot(a.astype(BF16), b.astype(BF16), preferred_element_type=F32)


def _dot_nt(a, b):
    return lax.dot_general(a.astype(BF16), b.astype(BF16), (((1,), (1,)), ((), ())),
                           preferred_element_type=F32)


def _dot_tn(a, b):
    return lax.dot_general(a.astype(BF16), b.astype(BF16), (((0,), (0,)), ((), ())),
                           preferred_element_type=F32)


def _split2(a):
    hi = a.astype(BF16)
    lo = (a - hi.astype(F32)).astype(BF16)
    return hi, lo


def _split3(a):
    hi = a.astype(BF16)
    r1 = a - hi.astype(F32)
    mid = r1.astype(BF16)
    lo = (r1 - mid.astype(F32)).astype(BF16)
    return hi, mid, lo


def _dot3(a, b):
    ah, al = _split2(a)
    bh, bl = _split2(b)
    d = functools.partial(jnp.dot, preferred_element_type=F32)
    return d(ah, bh) + (d(al, bh) + d(ah, bl))


def _dot_exact_rhs(a, b_exact):
    ah, al = _split2(a)
    d = functools.partial(jnp.dot, preferred_element_type=F32)
    return d(ah, b_exact) + d(al, b_exact)


def _layer_norm(z, g, b):
    mu = jnp.mean(z, axis=-1, keepdims=True)
    zc = z - mu
    var = jnp.mean(zc * zc, axis=-1, keepdims=True)
    return zc * lax.rsqrt(var + LN_EPS) * g + b


def _softplus(z):
    return jnp.maximum(z, 0.0) + jnp.log(1.0 + jnp.exp(-jnp.abs(z)))


def _params(*sem):
    return pltpu.CompilerParams(dimension_semantics=sem, vmem_limit_bytes=VMEM_LIMIT)


def _const_spec(shape):
    nd = len(shape)
    return pl.BlockSpec(shape, lambda *_: (0,) * nd, pipeline_mode=pl.Buffered(1))


def _mods_kernel(c_ref, w_ref, b_ref, o_ref):
    c = c_ref[...]
    ca = c * jax.nn.sigmoid(c)
    o_ref[0] = _dot3(ca, w_ref[0]) + b_ref[0]


def _mods(c, ada_w, ada_b, tn=1536):
    depth, d, n = ada_w.shape
    b = c.shape[0]
    rows = 8
    cp = jnp.pad(c, ((0, rows - b), (0, 0)))
    out = pl.pallas_call(
        _mods_kernel,
        grid=(depth, n // tn),
        in_specs=[pl.BlockSpec((rows, d), lambda l, j: (0, 0)),
                  pl.BlockSpec((1, d, tn), lambda l, j: (l, 0, j)),
                  pl.BlockSpec((1, 1, tn), lambda l, j: (l, 0, j))],
        out_specs=pl.BlockSpec((1, rows, tn), lambda l, j: (l, 0, j)),
        out_shape=jax.ShapeDtypeStruct((depth, rows, n), F32),
        compiler_params=_params("parallel", "parallel"),
        name="adaln_mods",
    )(cp, ada_w, ada_b.reshape(depth, 1, n))
    return out[:, :b].reshape(depth, b, 6, d)


def _fox_in_kernel(x_ref, mod_ref, w_ref, wf_ref, bf_ref, qg_ref, kg_ref, bd_ref, tri_ref,
                   pq_ref, pk_ref, oq_ref, ok_ref, wvt_ref,
                   q_ref, k_ref, o_ref, vt_ref, carry_ref, *, d, nc):
    t = pl.program_id(1)
    x = x_ref[0]
    mod = mod_ref[0]
    h = (x * (1.0 + mod[1:2]) + mod[0:1]).astype(BF16)
    bd = bd_ref[...]
    qg = qg_ref[...]
    kg = kg_ref[...]
    dd = functools.partial(jnp.dot, preferred_element_type=F32)

    fl = dd(h, wf_ref[...]) + bf_ref[...]
    lf = jnp.minimum(fl, 0.0) - jnp.log(1.0 + jnp.exp(-jnp.abs(fl)))
    tri = tri_ref[...]
    hi, mid, lo = _split3(lf)
    cs = dd(tri, hi) + (dd(tri, mid) + dd(tri, lo))

    @pl.when(t == 0)
    def _():
        carry_ref[...] = jnp.zeros_like(carry_ref)

    f = cs + carry_ref[0:1, :]
    tm = f.shape[0]
    carry_ref[...] = jnp.broadcast_to(f[tm - 1:tm, :], carry_ref.shape)
    fcat = jnp.concatenate(_split3(f * LOG2E), axis=1)
    low = lax.broadcasted_iota(jnp.int32, (tm, LANES), 1) < HEAD_DIM

    def head_rms(zj, g):
        ss = _dot_exact_rhs(zj * zj, bd)
        return zj * lax.rsqrt(ss * (1.0 / HEAD_DIM) + QK_EPS) * g

    def store_augmented(out_ref, zj, tile, p_ref, ones_ref):
        sl = slice(2 * tile * LANES, (2 * tile + 2) * LANES)
        aug = dd(fcat, p_ref[:, sl]) + ones_ref[:, sl]
        out_ref[0, :, sl] = jnp.concatenate(
            [jnp.where(low, zj, aug[:, :LANES]), jnp.where(low, aug[:, LANES:], zj)],
            axis=1).astype(out_ref.dtype)

    for sec, out_ref in enumerate((q_ref, k_ref, o_ref)):
        for c in range(d // nc):
            col = sec * d + c * nc
            z = dd(h, w_ref[:, col:col + nc])
            for j in range(nc // LANES):
                zj = z[:, j * LANES:(j + 1) * LANES]
                tile = c * (nc // LANES) + j
                if sec == 0:
                    store_augmented(out_ref, head_rms(zj, qg), tile, pq_ref, oq_ref)
                elif sec == 1:
                    store_augmented(out_ref, head_rms(zj, kg), tile, pk_ref, ok_ref)
                else:
                    out_ref[0, :, tile * LANES:(tile + 1) * LANES] = (
                        jax.nn.sigmoid(zj).astype(out_ref.dtype))
    for c in range(d // nc):
        vt_ref[0, c * nc:(c + 1) * nc, :] = lax.dot_general(
            wvt_ref[c * nc:(c + 1) * nc, :], h, (((1,), (1,)), ((), ())),
            preferred_element_type=F32).astype(vt_ref.dtype)


def _fox_in(x, mod, w_qko, w_vt, w_f, b_f, q_g, k_g, tm=512, nc=512):
    b, t, d = x.shape
    n_heads = d // HEAD_DIM
    heads_per_tile = LANES // HEAD_DIM
    qg = jnp.tile(q_g * (HEAD_DIM ** -0.5 * LOG2E), heads_per_tile).reshape(1, LANES)
    kg = jnp.tile(k_g, heads_per_tile).reshape(1, LANES)
    logit_bound = HEAD_DIM * jnp.max(jnp.abs(qg)) * jnp.max(jnp.abs(kg))
    idx = jnp.arange(LANES) // HEAD_DIM
    bd = (idx[:, None] == idx[None, :]).astype(BF16)
    tri = (jnp.arange(tm)[:, None] >= jnp.arange(tm)[None, :]).astype(BF16)
    pq = np.zeros((N_SPLIT * LANES, n_heads * LANES), np.float32)
    pk = np.zeros_like(pq)
    oq = np.zeros((1, n_heads * LANES), np.float32)
    ok = np.zeros_like(oq)
    shift_lane = np.zeros_like(oq)
    for hd in range(n_heads):
        base = hd * LANES + (HEAD_DIM if hd % 2 == 0 else 0)
        for part in range(N_SPLIT):
            pq[part * LANES + hd, base + N_SPLIT + part] = 1.0
            ok[0, base + N_SPLIT + part] = 1.0
            pk[part * LANES + hd, base + part] = -1.0
            oq[0, base + part] = 1.0
        shift_lane[0, base + 2 * N_SPLIT] = 1.0
    pq = jnp.asarray(pq, BF16)
    pk = jnp.asarray(pk, BF16)
    oq = jnp.asarray(oq) - logit_bound * jnp.asarray(shift_lane)
    ok = jnp.asarray(ok + shift_lane)
    act = jax.ShapeDtypeStruct((b, t, d), BF16)
    aug = jax.ShapeDtypeStruct((b, t, n_heads * LANES), BF16)
    row_spec = pl.BlockSpec((1, tm, d), lambda i, j: (i, j, 0))
    aug_spec = pl.BlockSpec((1, tm, n_heads * LANES), lambda i, j: (i, j, 0))
    kern = functools.partial(_fox_in_kernel, d=d, nc=nc)
    consts = (w_qko, w_f, b_f, qg, kg, bd, tri, pq, pk, oq, ok, w_vt)
    q_aug, k_aug, og, vt = pl.pallas_call(
        kern,
        grid=(b, t // tm),
        in_specs=[row_spec, pl.BlockSpec((1, 6, d), lambda i, j: (i, 0, 0))]
        + [_const_spec(c.shape) for c in consts],
        out_specs=[aug_spec, aug_spec, row_spec, pl.BlockSpec((1, d, tm), lambda i, j: (i, 0, j))],
        out_shape=[aug, aug, act, jax.ShapeDtypeStruct((b, d, t), BF16)],
        scratch_shapes=[pltpu.VMEM((8, LANES), F32)],
        compiler_params=_params("parallel", "arbitrary"),
        name="fox_in_proj",
    )(x, mod, *consts)
    return q_aug, k_aug, og, vt, logit_bound


def _attn_kernel(q_ref, k_ref, vt_ref, g_ref, y_ref, *, tq, tk):
    qi = pl.program_id(2)
    heads = LANES // HEAD_DIM
    pair = 2 * tk
    key = lax.broadcasted_iota(jnp.int32, (tk, tq), 0)
    qry = lax.broadcasted_iota(jnp.int32, (tk, tq), 1) + qi * tq
    qs = [q_ref[0, :, hh * LANES:(hh + 1) * LANES] for hh in range(heads)]

    def step(j, carry, masked):
        starts = [pl.multiple_of(j * pair + bb * tk, tk) for bb in range(2)]

        def scores(hh, bb):
            z = _dot_nt(k_ref[0, pl.ds(starts[bb], tk), hh * LANES:(hh + 1) * LANES], qs[hh])
            if masked:
                z = jnp.where(key + starts[bb] <= qry, z, NEG_BIG)
            return z

        def weights(z, m):
            p = jnp.exp2(z - m)
            return p, jnp.sum(p, axis=0, keepdims=True)

        def values(hh, bb, p):
            vt = vt_ref[0, hh * HEAD_DIM:(hh + 1) * HEAD_DIM, pl.ds(starts[bb], tk)]
            return jnp.dot(vt, p.astype(BF16), preferred_element_type=F32)

        colmax = lambda z: jnp.max(z, axis=0, keepdims=True)
        (m0, l0, a0), (m1, l1, a1) = carry
        z00 = scores(0, 0)
        z10 = scores(1, 0)
        m0a = jnp.maximum(m0, colmax(z00))
        z01 = scores(0, 1)
        p00, s00 = weights(z00, m0a)
        m1a = jnp.maximum(m1, colmax(z10))
        z11 = scores(1, 1)
        v00 = values(0, 0, p00)
        p10, s10 = weights(z10, m1a)
        m0b = jnp.maximum(m0a, colmax(z01))
        v10 = values(1, 0, p10)
        p01, s01 = weights(z01, m0b)
        m1b = jnp.maximum(m1a, colmax(z11))
        v01 = values(0, 1, p01)
        p11, s11 = weights(z11, m1b)
        v11 = values(1, 1, p11)

        def merge(m, l, a, ma, mb, sa, sb, va, vb):
            ra = jnp.exp2(m - ma)
            rb = jnp.exp2(ma - mb)
            return mb, rb * (ra * l + sa) + sb, rb * (ra * a + va) + vb

        return (merge(m0, l0, a0, m0a, m0b, s00, s01, v00, v01),
                merge(m1, l1, a1, m1a, m1b, s10, s11, v10, v11))

    init = tuple((jnp.full((1, tq), NEG_BIG, F32), jnp.zeros((1, tq), F32),
                  jnp.zeros((HEAD_DIM, tq), F32)) for _ in range(heads))
    n_full = (qi * tq) // pair
    carry = lax.fori_loop(0, n_full, functools.partial(step, masked=False), init)
    for jm in range(tq // pair):
        carry = step(n_full + jm, carry, True)
    yt = jnp.concatenate([carry[hh][2] / carry[hh][1] for hh in range(heads)], axis=0)
    y_ref[0] = (yt.T * g_ref[0].astype(F32)).astype(y_ref.dtype)


def _attn_bounded_kernel(q_ref, k_ref, vt_ref, g_ref, y_ref, *, tq, tk, long_blocks=4):
    qi = pl.program_id(2)
    heads = LANES // HEAD_DIM
    pair = 2 * tk
    key = lax.broadcasted_iota(jnp.int32, (tk, tq), 0)
    qry = lax.broadcasted_iota(jnp.int32, (tk, tq), 1) + qi * tq
    qs = [q_ref[0, :, hh * LANES:(hh + 1) * LANES] for hh in range(heads)]

    def step(j, carry, masked, nb, base=0):
        chains = [(bb, hh) for bb in range(nb) for hh in range(heads)]
        starts = [pl.multiple_of(base + j * (nb * tk) + bb * tk, tk) for bb in range(nb)]

        def scores(bb, hh):
            z = _dot_nt(k_ref[0, pl.ds(starts[bb], tk), hh * LANES:(hh + 1) * LANES], qs[hh])
            if masked:
                z = jnp.where(key + starts[bb] <= qry, z, NEG_BIG)
            return z

        sums = [[] for _ in range(heads)]
        vals = [[] for _ in range(heads)]
        z_next = scores(*chains[0])
        for c, (bb, hh) in enumerate(chains):
            z = z_next
            if c + 1 < len(chains):
                z_next = scores(*chains[c + 1])
            p = jnp.exp2(z)
            vt = vt_ref[0, hh * HEAD_DIM:(hh + 1) * HEAD_DIM, pl.ds(starts[bb], tk)]
            sums[hh].append(jnp.sum(p, axis=0, keepdims=True))
            vals[hh].append(jnp.dot(vt, p.astype(BF16), preferred_element_type=F32))
        return tuple((carry[hh][0] + functools.reduce(jnp.add, sums[hh]),
                      carry[hh][1] + functools.reduce(jnp.add, vals[hh])) for hh in range(heads))

    init = tuple((jnp.zeros((1, tq), F32), jnp.zeros((HEAD_DIM, tq), F32)) for _ in range(heads))
    n_long = (qi * tq) // (long_blocks * tk)
    carry = lax.fori_loop(0, n_long, functools.partial(step, masked=False, nb=long_blocks), init)
    rest = n_long * (long_blocks * tk)
    n_pairs = (qi * tq - rest) // pair
    carry = lax.fori_loop(0, n_pairs, functools.partial(step, masked=False, nb=2, base=rest), carry)
    for jm in range(tq // pair):
        carry = step(jm, carry, True, nb=2, base=qi * tq)
    yt = jnp.concatenate([carry[hh][1] / carry[hh][0] for hh in range(heads)], axis=0)
    y_ref[0] = (yt.T * g_ref[0].astype(F32)).astype(y_ref.dtype)


def _fox_attention(q_aug, k_aug, vt, og, logit_bound, tq=512, tk=512):
    return lax.cond(2.0 * logit_bound <= SAFE_LOGIT_RANGE,
                    functools.partial(_attention_call, kernel_fn=_attn_bounded_kernel, tq=tq, tk=tk),
                    functools.partial(_attention_call, kernel_fn=_attn_kernel, tq=tq, tk=tk),
                    q_aug, k_aug, vt, og)


def _attention_call(q_aug, k_aug, vt, og, *, kernel_fn, tq, tk):
    b, d, t = vt.shape
    heads = LANES // HEAD_DIM
    blk = pl.BlockSpec((1, tq, LANES), lambda i, p, j: (i, j, p))
    q_blk = pl.BlockSpec((1, tq, heads * LANES), lambda i, p, j: (i, j, p))
    k_full = pl.BlockSpec((1, t, heads * LANES), lambda i, p, j: (i, 0, p))
    vt_full = pl.BlockSpec((1, LANES, t), lambda i, p, j: (i, p, 0))
    return pl.pallas_call(
        functools.partial(kernel_fn, tq=tq, tk=tk),
        grid=(b, d // LANES, t // tq),
        in_specs=[q_blk, k_full, vt_full, blk],
        out_specs=blk,
        out_shape=jax.ShapeDtypeStruct((b, t, d), BF16),
        compiler_params=_params("parallel", "parallel", "arbitrary"),
        name="fox_attention_bounded" if kernel_fn is _attn_bounded_kernel else "fox_attention",
    )(q_aug, k_aug, vt, og)


def _proj_ln_kernel(a_ref, x_ref, mod_ref, w_ref, lng_ref, lnb_ref, o_ref):
    y = jnp.dot(a_ref[0], w_ref[...], preferred_element_type=F32)
    z = DEEPNORM_ALPHA * x_ref[0] + mod_ref[0][2:3] * y
    o_ref[0] = _layer_norm(z, lng_ref[...], lnb_ref[...])


def _proj_ln(a, x, mod, w, ln_g, ln_b, tm=512):
    b, t, d = x.shape
    row_spec = pl.BlockSpec((1, tm, d), lambda i, j: (i, j, 0))
    return pl.pallas_call(
        _proj_ln_kernel,
        grid=(b, t // tm),
        in_specs=[row_spec, row_spec,
                  pl.BlockSpec((1, 6, d), lambda i, j: (i, 0, 0)),
                  _const_spec(w.shape), _const_spec((1, d)), _const_spec((1, d))],
        out_specs=row_spec,
        out_shape=jax.ShapeDtypeStruct((b, t, d), F32),
        compiler_params=_params("parallel", "parallel"),
        name="attn_out_proj_ln",
    )(a, x, mod, w, ln_g.reshape(1, d), ln_b.reshape(1, d))


def _ffn_kernel(x_ref, mod_ref, win_ref, wout_ref, lng_ref, lnb_ref, o_ref, act_ref, *, d_ff, fc):
    x = x_ref[0]
    mod = mod_ref[0]
    h = (x * (1.0 + mod[4:5]) + mod[3:4]).astype(BF16)
    for c in range(d_ff // fc):
        g = jnp.dot(h, win_ref[:, c * fc:(c + 1) * fc], preferred_element_type=F32)
        u = jnp.dot(h, win_ref[:, d_ff + c * fc:d_ff + (c + 1) * fc], preferred_element_type=F32)
        act_ref[:, c * fc:(c + 1) * fc] = (g * jax.nn.sigmoid(g) * u).astype(BF16)
    y = jnp.dot(act_ref[...], wout_ref[...], preferred_element_type=F32)
    z = DEEPNORM_ALPHA * x + mod[5:6] * y
    o_ref[0] = _layer_norm(z, lng_ref[...], lnb_ref[...])


def _ffn(x, mod, w_in, w_out, ln_g, ln_b, tm=512, fc=256):
    b, t, d = x.shape
    d_ff = w_out.shape[0]
    row_spec = pl.BlockSpec((1, tm, d), lambda i, j: (i, j, 0))
    return pl.pallas_call(
        functools.partial(_ffn_kernel, d_ff=d_ff, fc=fc),
        grid=(b, t // tm),
        in_specs=[row_spec,
                  pl.BlockSpec((1, 6, d), lambda i, j: (i, 0, 0)),
                  _const_spec(w_in.shape), _const_spec(w_out.shape),
                  _const_spec((1, d)), _const_spec((1, d))],
        out_specs=row_spec,
        out_shape=jax.ShapeDtypeStruct((b, t, d), F32),
        scratch_shapes=[pltpu.VMEM((tm, d_ff), BF16)],
        compiler_params=_params("parallel", "parallel"),
        name="swiglu_ln",
    )(x, mod, w_in, w_out, ln_g.reshape(1, d), ln_b.reshape(1, d))


def _rwkv_in_kernel(x_ref, xp_ref, mod_ref, mu_ref, wrkv_ref, w1_ref, w2_ref, a1_ref, a2_ref,
                    g1_ref, g2_ref, vec_ref, bd_ref,
                    r_ref, lw_ref, k_ref, v_ref, kk_ref, a_ref, g_ref, *, d):
    t = pl.program_id(1)
    mod = mod_ref[0]
    sc = 1.0 + mod[1:2]
    sh = mod[0:1]
    h = x_ref[0] * sc + sh
    tm = h.shape[0]
    prev = xp_ref[0][7:8, :] * sc + sh
    prev = jnp.where(t == 0, jnp.zeros_like(prev), prev)
    rows = lax.broadcasted_iota(jnp.int32, h.shape, 0)
    hprev = jnp.where(rows == 0, prev, pltpu.roll(h, 1, axis=0))
    dx = hprev - h
    mu = mu_ref[...]
    vec = vec_ref[...]
    w0, a0, k_k, k_a = vec[0:1], vec[1:2], vec[2:3], vec[3:4]

    def mix(n):
        return (h + dx * mu[n:n + 1]).astype(BF16)

    dd = functools.partial(jnp.dot, preferred_element_type=F32)
    r = dd(mix(0), wrkv_ref[0])
    k = dd(mix(1), wrkv_ref[1])
    v = dd(mix(2), wrkv_ref[2])
    ww = w0 + dd(jnp.tanh(dd(mix(3), w1_ref[...])).astype(BF16), w2_ref[...])
    lw = -jnp.exp(-_softplus(-ww) - 0.5)
    a = jax.nn.sigmoid(a0 + dd(dd(mix(4), a1_ref[...]).astype(BF16), a2_ref[...]))
    g = dd(jax.nn.sigmoid(dd(mix(5), g1_ref[...])).astype(BF16), g2_ref[...])
    kk = k * k_k
    k = k * (1.0 + (a - 1.0) * k_a)
    bd = bd_ref[...]
    for j in range(d // LANES):
        sl = slice(j * LANES, (j + 1) * LANES)
        kkj = kk[:, sl]
        ss = _dot_exact_rhs(kkj * kkj, bd)
        kk_ref[0, :, sl] = (kkj / jnp.maximum(jnp.sqrt(ss), 1e-12)).astype(kk_ref.dtype)
    r_ref[0] = r.astype(r_ref.dtype)
    lw_ref[0] = lw
    k_ref[0] = k.astype(k_ref.dtype)
    v_ref[0] = v.astype(v_ref.dtype)
    a_ref[0] = a.astype(a_ref.dtype)
    g_ref[0] = g.astype(g_ref.dtype)


def _rwkv_in(x, mod, mu, w_rkv, w1, w2, a1, a2, g1, g2, vec, tm=512):
    b, t, d = x.shape
    idx = jnp.arange(LANES) // HEAD_DIM
    bd = (idx[:, None] == idx[None, :]).astype(BF16)
    row_spec = pl.BlockSpec((1, tm, d), lambda i, j: (i, j, 0))
    prev_spec = pl.BlockSpec((1, 8, d), lambda i, j: (i, jnp.maximum(j * (tm // 8) - 1, 0), 0))
    act = jax.ShapeDtypeStruct((b, t, d), BF16)
    consts = (mu, w_rkv, w1, w2, a1, a2, g1, g2, vec, bd)
    return pl.pallas_call(
        functools.partial(_rwkv_in_kernel, d=d),
        grid=(b, t // tm),
        in_specs=[row_spec, prev_spec, pl.BlockSpec((1, 6, d), lambda i, j: (i, 0, 0))]
        + [_const_spec(c.shape) for c in consts],
        out_specs=[row_spec] * 7,
        out_shape=[act, jax.ShapeDtypeStruct((b, t, d), F32), act, act, act, act, act],
        compiler_params=_params("parallel", "parallel"),
        name="rwkv_in_proj",
    )(x, x, mod, *consts)


def _rwkv_scan_kernel(r_ref, lw_ref, k_ref, v_ref, kk_ref, a_ref, lev_ref, tri_ref,
                      y_ref, h_ref, yi_ref, rq_ref, m_ref, n_ref, *, groups, steps_per_seq):
    s = pl.program_id(0)
    gr = 2 * CHUNK
    st = 2 * gr
    n_lev = CHUNK.bit_length() - 1

    @pl.when(s == 0)
    def _():
        h_ref[...] = jnp.zeros_like(h_ref)
        yi_ref[...] = jnp.zeros_like(yi_ref)
        rq_ref[...] = jnp.zeros_like(rq_ref)
        m_ref[...] = jnp.zeros_like(m_ref)
        n_ref[...] = jnp.zeros_like(n_ref)

    first_of_seq = lax.rem(jnp.maximum(s - 1, 0), steps_per_seq) == 0
    state = [jnp.where(first_of_seq, 0.0, h_ref[...])]
    pending = list(range(2 * groups))

    def state_steps(count):
        for _ in range(count):
            if pending:
                i = pending.pop(0)
                lo = i * CHUNK
                h = state[0]
                y_ref[0, lo:lo + CHUNK, :] = (yi_ref[lo:lo + CHUNK, :]
                                              + _dot(rq_ref[lo:lo + CHUNK, :], h))
                state[0] = _dot3(m_ref[i], h) + n_ref[i]

    lane = lax.broadcasted_iota(jnp.int32, (gr, LANES), 1)
    head0 = lane < HEAD_DIM
    lev = lev_ref[...]
    tri = tri_ref[...]
    strict = lev >= 0
    srow = lax.broadcasted_iota(jnp.int32, (st, st), 0)
    scol = lax.broadcasted_iota(jnp.int32, (st, st), 1)
    eye_st = srow == scol
    incl = strict | eye_st
    hrow = lax.broadcasted_iota(jnp.int32, (LANES, LANES), 0)
    hcol = lax.broadcasted_iota(jnp.int32, (LANES, LANES), 1)
    same_head = (hrow < HEAD_DIM) == (hcol < HEAD_DIM)
    eye_h = hrow == hcol

    def stack(x):
        zero = jnp.zeros_like(x)
        return jnp.concatenate([jnp.where(head0, x, zero), jnp.where(head0, zero, x)], axis=0)

    def unstack(x):
        return x[:gr] + x[gr:]

    def prep(g):
        rs = slice(g * gr, (g + 1) * gr)
        lw = lw_ref[0, rs, :]
        hi, mid, lo = _split3(lw)
        dd = functools.partial(jnp.dot, preferred_element_type=F32)
        cum = dd(tri, hi) + (dd(tri, mid) + dd(tri, lo))
        clast = jnp.concatenate(
            [jnp.broadcast_to(cum[(j + 1) * CHUNK - 1:(j + 1) * CHUNK], (CHUNK, LANES))
             for j in range(2)], axis=0)
        g_inv = jnp.exp(-cum)
        g_rem = jnp.exp(clast - cum)
        r = r_ref[0, rs, :].astype(F32)
        k = k_ref[0, rs, :].astype(F32)
        v = v_ref[0, rs, :].astype(F32)
        kk = kk_ref[0, rs, :].astype(F32)
        bt = kk * a_ref[0, rs, :].astype(F32)
        a_st = stack(-kk * jnp.exp(cum - lw))
        r_st = stack(r * jnp.exp(cum))
        lhs = jnp.concatenate([a_st, r_st], axis=0)
        rhs = jnp.concatenate([stack(bt * g_inv), stack(k * g_inv)], axis=0)
        aa = _dot_nt(lhs, rhs)
        return dict(
            v=v, v_st=stack(v), a_st=a_st, r_st=r_st, b_rem=bt * g_rem, k_rem=k * g_rem,
            g_last=jnp.exp(clast),
            a_ab=jnp.where(strict, aa[:st, :st], 0.0), a_ak=jnp.where(strict, aa[:st, st:], 0.0),
            a_rb=jnp.where(incl, aa[st:, :st], 0.0), a_rk=jnp.where(incl, aa[st:, st:], 0.0))

    per_stage = -(-2 * groups // (n_lev + 2))
    ps = [prep(g) for g in range(groups)]
    gs = range(groups)
    state_steps(per_stage)

    xs = [jnp.where(eye_st, 1.0, jnp.where(lev == 0, p["a_ab"], 0.0)) for p in ps]
    for level in range(1, n_lev):
        ws = [_dot(jnp.where(lev == level, ps[g]["a_ab"], 0.0), xs[g]) for g in gs]
        xs = [xs[g] + _dot(xs[g], ws[g]) for g in gs]
        state_steps(per_stage)
    res = [jnp.where(eye_st, 1.0, 0.0) - xs[g] + _dot3(ps[g]["a_ab"], xs[g]) for g in gs]
    state_steps(per_stage)
    xs = [xs[g] + _dot(xs[g], res[g]) for g in gs]

    gm = [jnp.concatenate([_dot(ps[g]["a_ak"], ps[g]["v_st"]), ps[g]["a_st"]], axis=1) for g in gs]
    state_steps(2 * groups)
    h_ref[...] = state[0]
    tg = [_dot(xs[g], gm[g]) for g in gs]
    rb = [_dot(ps[g]["a_rb"], tg[g]) for g in gs]
    rk = [_dot(ps[g]["a_rk"], ps[g]["v_st"]) for g in gs]
    for g in gs:
        p = ps[g]
        yi_ref[g * gr:(g + 1) * gr, :] = unstack(rb[g][:, :LANES] + rk[g])
        rq_ref[g * gr:(g + 1) * gr, :] = unstack(p["r_st"] + rb[g][:, LANES:])
        uv = unstack(tg[g][:, :LANES])
        wa = unstack(tg[g][:, LANES:])
        for j in range(2):
            cs = slice(j * CHUNK, (j + 1) * CHUNK)
            lhs_t = jnp.concatenate([p["b_rem"][cs], p["k_rem"][cs]], axis=0)
            rhs_t = jnp.concatenate(
                [jnp.concatenate([uv[cs], wa[cs]], axis=1),
                 jnp.concatenate([p["v"][cs], jnp.zeros_like(p["v"][cs])], axis=1)], axis=0)
            nm = _dot_tn(lhs_t, rhs_t)
            n_ref[2 * g + j] = jnp.where(same_head, nm[:, :LANES], 0.0)
            decay = jnp.broadcast_to(p["g_last"][j * CHUNK:j * CHUNK + 1], (LANES, LANES))
            m_ref[2 * g + j] = (jnp.where(same_head, nm[:, LANES:], 0.0)
                                + jnp.where(eye_h, decay, 0.0))


def _rwkv_scan(r, lw, k, v, kk, a, groups=2):
    b, t, d = r.shape
    gr = 2 * CHUNK
    st = 2 * gr
    rows = gr * groups
    n_hp = d // LANES
    steps_per_seq = t // rows
    n_blocks = b * n_hp * steps_per_seq

    def block_index(blk):
        seq = blk // steps_per_seq
        return seq // n_hp, blk % steps_per_seq, seq % n_hp

    in_blk = pl.BlockSpec((1, rows, LANES), lambda s: block_index(jnp.minimum(s, n_blocks - 1)))
    out_blk = pl.BlockSpec((1, rows, LANES), lambda s: block_index(jnp.maximum(s - 1, 0)))
    idx = jnp.arange(st)
    xor = idx[:, None] ^ idx[None, :]
    same = (idx[:, None] // CHUNK) == (idx[None, :] // CHUNK)
    lower = idx[None, :] < idx[:, None]
    msb = jnp.floor(jnp.log2(jnp.maximum(xor, 1).astype(F32))).astype(jnp.int32)
    lev = jnp.where(same & lower, msb, -1).astype(jnp.int32)
    ti = jnp.arange(gr)
    tri = ((ti[:, None] >= ti[None, :]) & ((ti[:, None] // CHUNK) == (ti[None, :] // CHUNK))).astype(BF16)
    return pl.pallas_call(
        functools.partial(_rwkv_scan_kernel, groups=groups, steps_per_seq=steps_per_seq),
        grid=(n_blocks + 1,),
        in_specs=[in_blk] * 6 + [_const_spec(lev.shape), _const_spec(tri.shape)],
        out_specs=out_blk,
        out_shape=jax.ShapeDtypeStruct((b, t, d), F32),
        scratch_shapes=[pltpu.VMEM((LANES, LANES), F32),
                        pltpu.VMEM((rows, LANES), F32), pltpu.VMEM((rows, LANES), F32),
                        pltpu.VMEM((2 * groups, LANES, LANES), F32),
                        pltpu.VMEM((2 * groups, LANES, LANES), F32)],
        compiler_params=_params("arbitrary"),
        name="rwkv7_chunk_scan",
    )(r, lw, k, v, kk, a, lev, tri)


def _rwkv_out_kernel(y_ref, r_ref, k_ref, v_ref, g_ref, x_ref, mod_ref, vec_ref, bd_ref, w_ref,
                     lng_ref, lnb_ref, o_ref, act_ref, *, d):
    bd = bd_ref[...]
    vec = vec_ref[...]
    inv = 1.0 / HEAD_DIM
    for j in range(d // LANES):
        sl = slice(j * LANES, (j + 1) * LANES)
        y = y_ref[0, :, sl]
        mean = _dot_exact_rhs(y, bd) * inv
        yc = y - mean
        var = _dot_exact_rhs(yc * yc, bd) * inv
        yn = yc * lax.rsqrt(var + GN_EPS) * vec[0:1, sl] + vec[1:2, sl]
        r = r_ref[0, :, sl].astype(F32)
        k = k_ref[0, :, sl].astype(F32)
        bonus = _dot_exact_rhs(r * k * vec[2:3, sl], bd) * v_ref[0, :, sl].astype(F32)
        act_ref[:, sl] = ((yn + bonus) * g_ref[0, :, sl].astype(F32)).astype(BF16)
    out = jnp.dot(act_ref[...], w_ref[...], preferred_element_type=F32)
    z = DEEPNORM_ALPHA * x_ref[0] + mod_ref[0][2:3] * out
    o_ref[0] = _layer_norm(z, lng_ref[...], lnb_ref[...])


def _rwkv_out(y, r, k, v, g, x, mod, vec, w, ln_g, ln_b, tm=512):
    b, t, d = x.shape
    idx = jnp.arange(LANES) // HEAD_DIM
    bd = (idx[:, None] == idx[None, :]).astype(BF16)
    row_spec = pl.BlockSpec((1, tm, d), lambda i, j: (i, j, 0))
    return pl.pallas_call(
        functools.partial(_rwkv_out_kernel, d=d),
        grid=(b, t // tm),
        in_specs=[row_spec] * 6
        + [pl.BlockSpec((1, 6, d), lambda i, j: (i, 0, 0)),
           _const_spec(vec.shape), _const_spec(bd.shape), _const_spec(w.shape),
           _const_spec((1, d)), _const_spec((1, d))],
        out_specs=row_spec,
        out_shape=jax.ShapeDtypeStruct((b, t, d), F32),
        scratch_shapes=[pltpu.VMEM((tm, d), BF16)],
        compiler_params=_params("parallel", "parallel"),
        name="rwkv_out_proj_ln",
    )(y, r, k, v, g, x, mod, vec, bd, w, ln_g.reshape(1, d), ln_b.reshape(1, d))


def _trunk(x, c, ada_w, ada_b, ln_g, ln_b, ffn_w_in, ffn_w_out, fox_w_in, fox_b_f, fox_q_g, fox_k_g, fox_w_o, rwkv_mu, rwkv_w_rkv, rwkv_w0, rwkv_w1, rwkv_w2, rwkv_a0, rwkv_a1, rwkv_a2, rwkv_g1, rwkv_g2, rwkv_k_k, rwkv_k_a, rwkv_r_k, rwkv_lnx_g, rwkv_lnx_b, rwkv_w_o,
           *, tm, tm_rwkv, tq, tk, groups, tn, nc, fc):
    b, t, d = x.shape
    n_heads = d // HEAD_DIM
    mods = _mods(c, ada_w, ada_b, tn=tn)

    w_in = fox_w_in[0]
    f_lo = 3 * d
    w_qko = jnp.concatenate([w_in[:, :2 * d], w_in[:, f_lo + n_heads:]], axis=1).astype(BF16)
    w_vt = w_in[:, 2 * d:f_lo].T.astype(BF16)
    w_f = jnp.pad(w_in[:, f_lo:f_lo + n_heads], ((0, 0), (0, LANES - n_heads))).astype(BF16)
    b_f = jnp.pad(fox_b_f[0], (0, LANES - n_heads)).reshape(1, LANES)
    q, k, og, v, logit_bound = _fox_in(x, mods[0], w_qko, w_vt, w_f, b_f, fox_q_g[0], fox_k_g[0],
                                       tm=tm, nc=nc)
    att = _fox_attention(q, k, v, og, logit_bound, tq=tq, tk=tk)
    x = _proj_ln(att, x, mods[0], fox_w_o[0].astype(BF16), ln_g[0, 0], ln_b[0, 0], tm=tm)
    x = _ffn(x, mods[0], ffn_w_in[0].astype(BF16), ffn_w_out[0].astype(BF16), ln_g[0, 1], ln_b[0, 1],
             tm=tm, fc=fc)

    vec_in = jnp.stack([rwkv_w0[0], rwkv_a0[0], rwkv_k_k[0], rwkv_k_a[0]])
    r, lw, k, v, kk, a, g = _rwkv_in(
        x, mods[1], rwkv_mu[0], rwkv_w_rkv[0].astype(BF16),
        rwkv_w1[0].astype(BF16), rwkv_w2[0].astype(BF16),
        rwkv_a1[0].astype(BF16), rwkv_a2[0].astype(BF16),
        rwkv_g1[0].astype(BF16), rwkv_g2[0].astype(BF16), vec_in, tm=tm_rwkv)
    y = _rwkv_scan(r, lw, k, v, kk, a, groups=groups)
    vec_out = jnp.stack([rwkv_lnx_g[0], rwkv_lnx_b[0], rwkv_r_k[0].reshape(d)])
    x = _rwkv_out(y, r, k, v, g, x, mods[1], vec_out, rwkv_w_o[0].astype(BF16), ln_g[1, 0], ln_b[1, 0],
                  tm=tm)
    x = _ffn(x, mods[1], ffn_w_in[1].astype(BF16), ffn_w_out[1].astype(BF16), ln_g[1, 1], ln_b[1, 1],
             tm=tm, fc=fc)
    return x


def kernel(x, c, ada_w, ada_b, ln_g, ln_b, ffn_w_in, ffn_w_out, fox_w_in, fox_b_f, fox_q_g, fox_k_g, fox_w_o, rwkv_mu, rwkv_w_rkv, rwkv_w0, rwkv_w1, rwkv_w2, rwkv_a0, rwkv_a1, rwkv_a2, rwkv_g1, rwkv_g2, rwkv_k_k, rwkv_k_a, rwkv_r_k, rwkv_lnx_g, rwkv_lnx_b, rwkv_w_o):
    return _trunk(x, c, ada_w, ada_b, ln_g, ln_b, ffn_w_in, ffn_w_out, fox_w_in, fox_b_f, fox_q_g, fox_k_g, fox_w_o, rwkv_mu, rwkv_w_rkv, rwkv_w0, rwkv_w1, rwkv_w2, rwkv_a0, rwkv_a1, rwkv_a2, rwkv_g1, rwkv_g2, rwkv_k_k, rwkv_k_a, rwkv_r_k, rwkv_lnx_g, rwkv_lnx_b, rwkv_w_o,
                  tm=512, tm_rwkv=256, tq=512, tk=256, groups=4, tn=1536, nc=512, fc=256)
```

```python
import functools

import jax
import jax.numpy as jnp
import numpy as np
from jax import lax
from jax.experimental import pallas as pl
from jax.experimental.pallas import tpu as pltpu

F32 = jnp.float32
BF16 = jnp.bfloat16

HEAD_DIM = 64
DEPTH = 2
DEEPNORM_ALPHA = (2 * DEPTH) ** 0.25
LN_EPS = 1e-5
QK_EPS = 1e-6
GN_EPS = HEAD_DIM * 1e-5
LANES = 128
SUM_W = 256
CHUNK = 64
NEG_BIG = -1e30
LOG2E = 1.4426950408889634
SAFE_LOGIT_RANGE = 96.0
N_SPLIT = 3
VMEM_LIMIT = 56 * 1024 * 1024


def _dot(a, b):
    return jnp.dot(a.astype(BF16), b.astype(BF16), preferred_element_type=F32)


def _dot_nt(a, b):
    return lax.dot_general(a.astype(BF16), b.astype(BF16), (((1,), (1,)), ((), ())),
                           preferred_element_type=F32)


def _dot_tn(a, b):
    return lax.dot_general(a.astype(BF16), b.astype(BF16), (((0,), (0,)), ((), ())),
                           preferred_element_type=F32)


def _split2(a):
    hi = a.astype(BF16)
    lo = (a - hi.astype(F32)).astype(BF16)
    return hi, lo


def _split3(a):
    hi = a.astype(BF16)
    r1 = a - hi.astype(F32)
    mid = r1.astype(BF16)
    lo = (r1 - mid.astype(F32)).astype(BF16)
    return hi, mid, lo


def _dot3(a, b):
    ah, al = _split2(a)
    bh, bl = _split2(b)
    d = functools.partial(jnp.dot, preferred_element_type=F32)
    return d(ah, bh) + (d(al, bh) + d(ah, bl))


def _dot_exact_rhs(a, b_exact):
    ah, al = _split2(a)
    d = functools.partial(jnp.dot, preferred_element_type=F32)
    return d(ah, b_exact) + d(al, b_exact)


def _head_sum_matrix():
    idx = jnp.arange(SUM_W) // HEAD_DIM
    return (idx[:, None] == idx[None, :]).astype(BF16)


def _layer_norm(z, g, b):
    mu = jnp.mean(z, axis=-1, keepdims=True)
    zc = z - mu
    var = jnp.mean(zc * zc, axis=-1, keepdims=True)
    return zc * lax.rsqrt(var + LN_EPS) * g + b


def _softplus(z):
    return jnp.maximum(z, 0.0) + jnp.log(1.0 + jnp.exp(-jnp.abs(z)))


def _params(*sem):
    return pltpu.CompilerParams(dimension_semantics=sem, vmem_limit_bytes=VMEM_LIMIT)


def _const_spec(shape):
    nd = len(shape)
    return pl.BlockSpec(shape, lambda *_: (0,) * nd, pipeline_mode=pl.Buffered(1))


def _mods_kernel(c_ref, w_ref, b_ref, o_ref):
    c = c_ref[...]
    ca = c * jax.nn.sigmoid(c)
    o_ref[0] = _dot3(ca, w_ref[0]) + b_ref[0]


def _mods(c, ada_w, ada_b, tn=1536):
    depth, d, n = ada_w.shape
    b = c.shape[0]
    rows = 8
    cp = jnp.pad(c, ((0, rows - b), (0, 0)))
    out = pl.pallas_call(
        _mods_kernel,
        grid=(depth, n // tn),
        in_specs=[pl.BlockSpec((rows, d), lambda l, j: (0, 0)),
                  pl.BlockSpec((1, d, tn), lambda l, j: (l, 0, j)),
                  pl.BlockSpec((1, 1, tn), lambda l, j: (l, 0, j))],
        out_specs=pl.BlockSpec((1, rows, tn), lambda l, j: (l, 0, j)),
        out_shape=jax.ShapeDtypeStruct((depth, rows, n), F32),
        compiler_params=_params("parallel", "parallel"),
        name="adaln_mods",
    )(cp, ada_w, ada_b.reshape(depth, 1, n))
    return out[:, :b].reshape(depth, b, 6, d)


def _fox_in_kernel(x_ref, mod_ref, w_ref, wf_ref, bf_ref, qg_ref, kg_ref, bd_ref, tri_ref,
                   pq_ref, pk_ref, oq_ref, ok_ref, wvt_ref,
                   q_ref, k_ref, o_ref, vt_ref, carry_ref, *, d, nc):
    t = pl.program_id(1)
    x = x_ref[0]
    mod = mod_ref[0]
    h = (x * (1.0 + mod[1:2]) + mod[0:1]).astype(BF16)
    bd = bd_ref[...]
    qg = qg_ref[...]
    kg = kg_ref[...]
    dd = functools.partial(jnp.dot, preferred_element_type=F32)

    fl = dd(h, wf_ref[...]) + bf_ref[...]
    lf = jnp.minimum(fl, 0.0) - jnp.log(1.0 + jnp.exp(-jnp.abs(fl)))
    tri = tri_ref[...]
    hi, mid, lo = _split3(lf)
    cs = dd(tri, hi) + (dd(tri, mid) + dd(tri, lo))

    @pl.when(t == 0)
    def _():
        carry_ref[...] = jnp.zeros_like(carry_ref)

    f = cs + carry_ref[0:1, :]
    tm = f.shape[0]
    carry_ref[...] = jnp.broadcast_to(f[tm - 1:tm, :], carry_ref.shape)
    n_heads = d // HEAD_DIM
    lane = lax.broadcasted_iota(jnp.int32, (tm, LANES), 1)
    parts = [jnp.where(lane < n_heads, p.astype(F32), 0.0) for p in _split3(f * LOG2E)]
    fpack = parts[0]
    for i in range(1, N_SPLIT):
        fpack = fpack + pltpu.roll(parts[i], i * n_heads, axis=1)
    fpack = fpack.astype(BF16)
    low = lane < HEAD_DIM

    def head_rms(zp, g):
        ss = _dot_exact_rhs(zp * zp, bd)
        return zp * lax.rsqrt(ss * (1.0 / HEAD_DIM) + QK_EPS) * g

    def store_augmented(out_ref, zj, tile, p_ref, ones_ref):
        sl = slice(2 * tile * LANES, (2 * tile + 2) * LANES)
        aug = dd(fpack, p_ref[:, sl]) + ones_ref[:, sl]
        out_ref[0, :, sl] = jnp.concatenate(
            [jnp.where(low, zj, aug[:, :LANES]), jnp.where(low, aug[:, LANES:], zj)],
            axis=1).astype(out_ref.dtype)

    for sec, out_ref in enumerate((q_ref, k_ref, o_ref)):
        for c in range(d // nc):
            col = sec * d + c * nc
            z = dd(h, w_ref[:, col:col + nc])
            if sec == 2:
                out_ref[0, :, c * nc:(c + 1) * nc] = jax.nn.sigmoid(z).astype(out_ref.dtype)
                continue
            for pr in range(nc // SUM_W):
                zn = head_rms(z[:, pr * SUM_W:(pr + 1) * SUM_W], qg if sec == 0 else kg)
                for j in range(SUM_W // LANES):
                    tile = (c * nc + pr * SUM_W) // LANES + j
                    store_augmented(out_ref, zn[:, j * LANES:(j + 1) * LANES], tile,
                                    pq_ref if sec == 0 else pk_ref,
                                    oq_ref if sec == 0 else ok_ref)
    for c in range(d // nc):
        vt_ref[0, c * nc:(c + 1) * nc, :] = lax.dot_general(
            wvt_ref[c * nc:(c + 1) * nc, :], h, (((1,), (1,)), ((), ())),
            preferred_element_type=F32).astype(vt_ref.dtype)


def _fox_in(x, mod, w_qko, w_vt, w_f, b_f, q_g, k_g, tm=512, nc=512):
    b, t, d = x.shape
    n_heads = d // HEAD_DIM
    assert N_SPLIT * n_heads <= LANES
    qg = jnp.tile(q_g * (HEAD_DIM ** -0.5 * LOG2E), SUM_W // HEAD_DIM).reshape(1, SUM_W)
    kg = jnp.tile(k_g, SUM_W // HEAD_DIM).reshape(1, SUM_W)
    logit_bound = HEAD_DIM * jnp.max(jnp.abs(qg)) * jnp.max(jnp.abs(kg))
    bd = _head_sum_matrix()
    tri = (jnp.arange(tm)[:, None] >= jnp.arange(tm)[None, :]).astype(BF16)
    pq = np.zeros((LANES, n_heads * LANES), np.float32)
    pk = np.zeros_like(pq)
    oq = np.zeros((1, n_heads * LANES), np.float32)
    ok = np.zeros_like(oq)
    shift_lane = np.zeros_like(oq)
    for hd in range(n_heads):
        base = hd * LANES + (HEAD_DIM if hd % 2 == 0 else 0)
        for part in range(N_SPLIT):
            pq[part * n_heads + hd, base + N_SPLIT + part] = 1.0
            ok[0, base + N_SPLIT + part] = 1.0
            pk[part * n_heads + hd, base + part] = -1.0
            oq[0, base + part] = 1.0
        shift_lane[0, base + 2 * N_SPLIT] = 1.0
    pq = jnp.asarray(pq, BF16)
    pk = jnp.asarray(pk, BF16)
    oq = jnp.asarray(oq) - logit_bound * jnp.asarray(shift_lane)
    ok = jnp.asarray(ok + shift_lane)
    act = jax.ShapeDtypeStruct((b, t, d), BF16)
    aug = jax.ShapeDtypeStruct((b, t, n_heads * LANES), BF16)
    row_spec = pl.BlockSpec((1, tm, d), lambda i, j: (i, j, 0))
    aug_spec = pl.BlockSpec((1, tm, n_heads * LANES), lambda i, j: (i, j, 0))
    kern = functools.partial(_fox_in_kernel, d=d, nc=nc)
    consts = (w_qko, w_f, b_f, qg, kg, bd, tri, pq, pk, oq, ok, w_vt)
    q_aug, k_aug, og, vt = pl.pallas_call(
        kern,
        grid=(b, t // tm),
        in_specs=[row_spec, pl.BlockSpec((1, 6, d), lambda i, j: (i, 0, 0))]
        + [_const_spec(c.shape) for c in consts],
        out_specs=[aug_spec, aug_spec, row_spec, pl.BlockSpec((1, d, tm), lambda i, j: (i, 0, j))],
        out_shape=[aug, aug, act, jax.ShapeDtypeStruct((b, d, t), BF16)],
        scratch_shapes=[pltpu.VMEM((8, LANES), F32)],
        compiler_params=_params("parallel", "arbitrary"),
        name="fox_in_proj",
    )(x, mod, *consts)
    return q_aug, k_aug, og, vt, logit_bound


def _attn_kernel(q_ref, k_ref, vt_ref, g_ref, y_ref, *, tq, tk):
    qi = pl.program_id(2)
    heads = LANES // HEAD_DIM
    pair = 2 * tk
    key = lax.broadcasted_iota(jnp.int32, (tk, tq), 0)
    qry = lax.broadcasted_iota(jnp.int32, (tk, tq), 1) + qi * tq
    qs = [q_ref[0, :, hh * LANES:(hh + 1) * LANES] for hh in range(heads)]

    def step(j, carry, masked):
        starts = [pl.multiple_of(j * pair + bb * tk, tk) for bb in range(2)]

        def scores(hh, bb):
            z = _dot_nt(k_ref[0, pl.ds(starts[bb], tk), hh * LANES:(hh + 1) * LANES], qs[hh])
            if masked:
                z = jnp.where(key + starts[bb] <= qry, z, NEG_BIG)
            return z

        def weights(z, m):
            p = jnp.exp2(z - m)
            return p, jnp.sum(p, axis=0, keepdims=True)

        def values(hh, bb, p):
            vt = vt_ref[0, hh * HEAD_DIM:(hh + 1) * HEAD_DIM, pl.ds(starts[bb], tk)]
            return jnp.dot(vt, p.astype(BF16), preferred_element_type=F32)

        colmax = lambda z: jnp.max(z, axis=0, keepdims=True)
        (m0, l0, a0), (m1, l1, a1) = carry
        z00 = scores(0, 0)
        z10 = scores(1, 0)
        m0a = jnp.maximum(m0, colmax(z00))
        z01 = scores(0, 1)
        p00, s00 = weights(z00, m0a)
        m1a = jnp.maximum(m1, colmax(z10))
        z11 = scores(1, 1)
        v00 = values(0, 0, p00)
        p10, s10 = weights(z10, m1a)
        m0b = jnp.maximum(m0a, colmax(z01))
        v10 = values(1, 0, p10)
        p01, s01 = weights(z01, m0b)
        m1b = jnp.maximum(m1a, colmax(z11))
        v01 = values(0, 1, p01)
        p11, s11 = weights(z11, m1b)
        v11 = values(1, 1, p11)

        def merge(m, l, a, ma, mb, sa, sb, va, vb):
            ra = jnp.exp2(m - ma)
            rb = jnp.exp2(ma - mb)
            return mb, rb * (ra * l + sa) + sb, rb * (ra * a + va) + vb

        return (merge(m0, l0, a0, m0a, m0b, s00, s01, v00, v01),
                merge(m1, l1, a1, m1a, m1b, s10, s11, v10, v11))

    init = tuple((jnp.full((1, tq), NEG_BIG, F32), jnp.zeros((1, tq), F32),
                  jnp.zeros((HEAD_DIM, tq), F32)) for _ in range(heads))
    n_full = (qi * tq) // pair
    carry = lax.fori_loop(0, n_full, functools.partial(step, masked=False), init)
    for jm in range(tq // pair):
        carry = step(n_full + jm, carry, True)
    yt = jnp.concatenate([carry[hh][2] / carry[hh][1] for hh in range(heads)], axis=0)
    y_ref[0] = (yt.T * g_ref[0].astype(F32)).astype(y_ref.dtype)


def _attn_bounded_kernel(q_ref, k_ref, vt_ref, g_ref, y_ref, *, tq, tk, long_blocks=4):
    qi = pl.program_id(2)
    heads = LANES // HEAD_DIM
    pair = 2 * tk
    key = lax.broadcasted_iota(jnp.int32, (tk, tq), 0)
    qry = lax.broadcasted_iota(jnp.int32, (tk, tq), 1) + qi * tq
    qs = [q_ref[0, :, hh * LANES:(hh + 1) * LANES] for hh in range(heads)]

    def step(j, carry, masked, nb, base=0):
        chains = [(bb, hh) for bb in range(nb) for hh in range(heads)]
        starts = [pl.multiple_of(base + j * (nb * tk) + bb * tk, tk) for bb in range(nb)]

        def scores(bb, hh):
            z = _dot_nt(k_ref[0, pl.ds(starts[bb], tk), hh * LANES:(hh + 1) * LANES], qs[hh])
            if masked:
                z = jnp.where(key + starts[bb] <= qry, z, NEG_BIG)
            return z

        sums = [[] for _ in range(heads)]
        vals = [[] for _ in range(heads)]
        z_next = scores(*chains[0])
        for c, (bb, hh) in enumerate(chains):
            z = z_next
            if c + 1 < len(chains):
                z_next = scores(*chains[c + 1])
            p = jnp.exp2(z)
            vt = vt_ref[0, hh * HEAD_DIM:(hh + 1) * HEAD_DIM, pl.ds(starts[bb], tk)]
            sums[hh].append(jnp.sum(p, axis=0, keepdims=True))
            vals[hh].append(jnp.dot(vt, p.astype(BF16), preferred_element_type=F32))
        return tuple((carry[hh][0] + functools.reduce(jnp.add, sums[hh]),
                      carry[hh][1] + functools.reduce(jnp.add, vals[hh])) for hh in range(heads))

    init = tuple((jnp.zeros((1, tq), F32), jnp.zeros((HEAD_DIM, tq), F32)) for _ in range(heads))
    n_long = (qi * tq) // (long_blocks * tk)
    carry = lax.fori_loop(0, n_long, functools.partial(step, masked=False, nb=long_blocks), init)
    rest = n_long * (long_blocks * tk)
    n_pairs = (qi * tq - rest) // pair
    carry = lax.fori_loop(0, n_pairs, functools.partial(step, masked=False, nb=2, base=rest), carry)
    for jm in range(tq // pair):
        carry = step(jm, carry, True, nb=2, base=qi * tq)
    yt = jnp.concatenate([carry[hh][1] / carry[hh][0] for hh in range(heads)], axis=0)
    y_ref[0] = (yt.T * g_ref[0].astype(F32)).astype(y_ref.dtype)


def _fox_attention(q_aug, k_aug, vt, og, logit_bound, tq=512, tk=512):
    return lax.cond(2.0 * logit_bound <= SAFE_LOGIT_RANGE,
                    functools.partial(_attention_call, kernel_fn=_attn_bounded_kernel, tq=tq, tk=tk),
                    functools.partial(_attention_call, kernel_fn=_attn_kernel, tq=tq, tk=tk),
                    q_aug, k_aug, vt, og)


def _attention_call(q_aug, k_aug, vt, og, *, kernel_fn, tq, tk):
    b, d, t = vt.shape
    heads = LANES // HEAD_DIM
    blk = pl.BlockSpec((1, tq, LANES), lambda i, p, j: (i, j, p))
    q_blk = pl.BlockSpec((1, tq, heads * LANES), lambda i, p, j: (i, j, p))
    k_full = pl.BlockSpec((1, t, heads * LANES), lambda i, p, j: (i, 0, p))
    vt_full = pl.BlockSpec((1, LANES, t), lambda i, p, j: (i, p, 0))
    return pl.pallas_call(
        functools.partial(kernel_fn, tq=tq, tk=tk),
        grid=(b, d // LANES, t // tq),
        in_specs=[q_blk, k_full, vt_full, blk],
        out_specs=blk,
        out_shape=jax.ShapeDtypeStruct((b, t, d), BF16),
        compiler_params=_params("parallel", "parallel", "arbitrary"),
        name="fox_attention_bounded" if kernel_fn is _attn_bounded_kernel else "fox_attention",
    )(q_aug, k_aug, vt, og)


def _proj_ln_kernel(a_ref, x_ref, mod_ref, w_ref, lng_ref, lnb_ref, o_ref):
    y = jnp.dot(a_ref[0], w_ref[...], preferred_element_type=F32)
    z = DEEPNORM_ALPHA * x_ref[0] + mod_ref[0][2:3] * y
    o_ref[0] = _layer_norm(z, lng_ref[...], lnb_ref[...])


def _proj_ln(a, x, mod, w, ln_g, ln_b, tm=512):
    b, t, d = x.shape
    row_spec = pl.BlockSpec((1, tm, d), lambda i, j: (i, j, 0))
    return pl.pallas_call(
        _proj_ln_kernel,
        grid=(b, t // tm),
        in_specs=[row_spec, row_spec,
                  pl.BlockSpec((1, 6, d), lambda i, j: (i, 0, 0)),
                  _const_spec(w.shape), _const_spec((1, d)), _const_spec((1, d))],
        out_specs=row_spec,
        out_shape=jax.ShapeDtypeStruct((b, t, d), F32),
        compiler_params=_params("parallel", "parallel"),
        name="attn_out_proj_ln",
    )(a, x, mod, w, ln_g.reshape(1, d), ln_b.reshape(1, d))


def _ffn_kernel(x_ref, mod_ref, win_ref, wout_ref, lng_ref, lnb_ref, o_ref, act_ref, *, d_ff, fc):
    x = x_ref[0]
    mod = mod_ref[0]
    h = (x * (1.0 + mod[4:5]) + mod[3:4]).astype(BF16)
    for c in range(d_ff // fc):
        g = jnp.dot(h, win_ref[:, c * fc:(c + 1) * fc], preferred_element_type=F32)
        u = jnp.dot(h, win_ref[:, d_ff + c * fc:d_ff + (c + 1) * fc], preferred_element_type=F32)
        act_ref[:, c * fc:(c + 1) * fc] = (g * jax.nn.sigmoid(g) * u).astype(BF16)
    y = jnp.dot(act_ref[...], wout_ref[...], preferred_element_type=F32)
    z = DEEPNORM_ALPHA * x + mod[5:6] * y
    o_ref[0] = _layer_norm(z, lng_ref[...], lnb_ref[...])


def _ffn(x, mod, w_in, w_out, ln_g, ln_b, tm=512, fc=256):
    b, t, d = x.shape
    d_ff = w_out.shape[0]
    row_spec = pl.BlockSpec((1, tm, d), lambda i, j: (i, j, 0))
    return pl.pallas_call(
        functools.partial(_ffn_kernel, d_ff=d_ff, fc=fc),
        grid=(b, t // tm),
        in_specs=[row_spec,
                  pl.BlockSpec((1, 6, d), lambda i, j: (i, 0, 0)),
                  _const_spec(w_in.shape), _const_spec(w_out.shape),
                  _const_spec((1, d)), _const_spec((1, d))],
        out_specs=row_spec,
        out_shape=jax.ShapeDtypeStruct((b, t, d), F32),
        scratch_shapes=[pltpu.VMEM((tm, d_ff), BF16)],
        compiler_params=_params("parallel", "parallel"),
        name="swiglu_ln",
    )(x, mod, w_in, w_out, ln_g.reshape(1, d), ln_b.reshape(1, d))


def _rwkv_in_kernel(x_ref, xp_ref, mod_ref, mu_ref, wrkv_ref, w1_ref, w2_ref, a1_ref, a2_ref,
                    g1_ref, g2_ref, vec_ref, bd_ref,
                    r_ref, lw_ref, k_ref, v_ref, kk_ref, a_ref, g_ref, *, d):
    t = pl.program_id(1)
    mod = mod_ref[0]
    sc = 1.0 + mod[1:2]
    sh = mod[0:1]
    h = x_ref[0] * sc + sh
    tm = h.shape[0]
    prev = xp_ref[0][7:8, :] * sc + sh
    prev = jnp.where(t == 0, jnp.zeros_like(prev), prev)
    rows = lax.broadcasted_iota(jnp.int32, h.shape, 0)
    hprev = jnp.where(rows == 0, prev, pltpu.roll(h, 1, axis=0))
    dx = hprev - h
    mu = mu_ref[...]
    vec = vec_ref[...]
    w0, a0, k_k, k_a = vec[0:1], vec[1:2], vec[2:3], vec[3:4]

    def mix(n):
        return (h + dx * mu[n:n + 1]).astype(BF16)

    dd = functools.partial(jnp.dot, preferred_element_type=F32)
    r = dd(mix(0), wrkv_ref[0])
    k = dd(mix(1), wrkv_ref[1])
    v = dd(mix(2), wrkv_ref[2])
    ww = w0 + dd(jnp.tanh(dd(mix(3), w1_ref[...])).astype(BF16), w2_ref[...])
    lw = -jnp.exp(-_softplus(-ww) - 0.5)
    a = jax.nn.sigmoid(a0 + dd(dd(mix(4), a1_ref[...]).astype(BF16), a2_ref[...]))
    g = dd(jax.nn.sigmoid(dd(mix(5), g1_ref[...])).astype(BF16), g2_ref[...])
    kk = k * k_k
    k = k * (1.0 + (a - 1.0) * k_a)
    bd = bd_ref[...]
    for j in range(d // SUM_W):
        sl = slice(j * SUM_W, (j + 1) * SUM_W)
        kkj = kk[:, sl]
        ss = _dot_exact_rhs(kkj * kkj, bd)
        kk_ref[0, :, sl] = (kkj / jnp.maximum(jnp.sqrt(ss), 1e-12)).astype(kk_ref.dtype)
    r_ref[0] = r.astype(r_ref.dtype)
    lw_ref[0] = lw
    k_ref[0] = k.astype(k_ref.dtype)
    v_ref[0] = v.astype(v_ref.dtype)
    a_ref[0] = a.astype(a_ref.dtype)
    g_ref[0] = g.astype(g_ref.dtype)


def _rwkv_in(x, mod, mu, w_rkv, w1, w2, a1, a2, g1, g2, vec, tm=512):
    b, t, d = x.shape
    bd = _head_sum_matrix()
    row_spec = pl.BlockSpec((1, tm, d), lambda i, j: (i, j, 0))
    prev_spec = pl.BlockSpec((1, 8, d), lambda i, j: (i, jnp.maximum(j * (tm // 8) - 1, 0), 0))
    act = jax.ShapeDtypeStruct((b, t, d), BF16)
    consts = (mu, w_rkv, w1, w2, a1, a2, g1, g2, vec, bd)
    return pl.pallas_call(
        functools.partial(_rwkv_in_kernel, d=d),
        grid=(b, t // tm),
        in_specs=[row_spec, prev_spec, pl.BlockSpec((1, 6, d), lambda i, j: (i, 0, 0))]
        + [_const_spec(c.shape) for c in consts],
        out_specs=[row_spec] * 7,
        out_shape=[act, jax.ShapeDtypeStruct((b, t, d), F32), act, act, act, act, act],
        compiler_params=_params("parallel", "parallel"),
        name="rwkv_in_proj",
    )(x, x, mod, *consts)


def _rwkv_scan_kernel(r_ref, lw_ref, k_ref, v_ref, kk_ref, a_ref, lev_ref, tri_ref,
                      y_ref, h_ref, yi_ref, rq_ref, m_ref, n_ref, *, groups, steps_per_seq):
    s = pl.program_id(0)
    gr = 2 * CHUNK
    st = 2 * gr
    n_lev = CHUNK.bit_length() - 1

    @pl.when(s == 0)
    def _():
        h_ref[...] = jnp.zeros_like(h_ref)
        yi_ref[...] = jnp.zeros_like(yi_ref)
        rq_ref[...] = jnp.zeros_like(rq_ref)
        m_ref[...] = jnp.zeros_like(m_ref)
        n_ref[...] = jnp.zeros_like(n_ref)

    first_of_seq = lax.rem(jnp.maximum(s - 1, 0), steps_per_seq) == 0
    state = [jnp.where(first_of_seq, 0.0, h_ref[...])]
    pending = list(range(2 * groups))

    def state_steps(count):
        for _ in range(count):
            if pending:
                i = pending.pop(0)
                lo = i * CHUNK
                h = state[0]
                y_ref[0, lo:lo + CHUNK, :] = (yi_ref[lo:lo + CHUNK, :]
                                              + _dot(rq_ref[lo:lo + CHUNK, :], h))
                state[0] = _dot3(m_ref[i], h) + n_ref[i]

    lane = lax.broadcasted_iota(jnp.int32, (gr, LANES), 1)
    head0 = lane < HEAD_DIM
    lev = lev_ref[...]
    tri = tri_ref[...]
    strict = lev >= 0
    srow = lax.broadcasted_iota(jnp.int32, (st, st), 0)
    scol = lax.broadcasted_iota(jnp.int32, (st, st), 1)
    eye_st = srow == scol
    incl = strict | eye_st
    hrow = lax.broadcasted_iota(jnp.int32, (LANES, LANES), 0)
    hcol = lax.broadcasted_iota(jnp.int32, (LANES, LANES), 1)
    same_head = (hrow < HEAD_DIM) == (hcol < HEAD_DIM)
    eye_h = hrow == hcol

    def stack(x):
        zero = jnp.zeros_like(x)
        return jnp.concatenate([jnp.where(head0, x, zero), jnp.where(head0, zero, x)], axis=0)

    def unstack(x):
        return x[:gr] + x[gr:]

    def log_decay_cumsum(g):
        hi, mid, lo = _split3(lw_ref[0, g * gr:(g + 1) * gr, :])
        dd = functools.partial(jnp.dot, preferred_element_type=F32)
        return dd(tri, hi) + (dd(tri, mid) + dd(tri, lo))

    def prep(g, cum):
        rs = slice(g * gr, (g + 1) * gr)
        lw = lw_ref[0, rs, :]
        clast = jnp.concatenate(
            [jnp.broadcast_to(cum[(j + 1) * CHUNK - 1:(j + 1) * CHUNK], (CHUNK, LANES))
             for j in range(2)], axis=0)
        g_inv = jnp.exp(-cum)
        g_rem = jnp.exp(clast - cum)
        r = r_ref[0, rs, :].astype(F32)
        k = k_ref[0, rs, :].astype(F32)
        v = v_ref[0, rs, :].astype(F32)
        kk = kk_ref[0, rs, :].astype(F32)
        bt = kk * a_ref[0, rs, :].astype(F32)
        a_st = stack(-kk * jnp.exp(cum - lw))
        r_st = stack(r * jnp.exp(cum))
        b_st = stack(bt * g_inv)
        k_st = stack(k * g_inv)
        prods = []
        for hd in range(2):
            hs = slice(hd * gr, (hd + 1) * gr)
            prods.append(_dot_nt(jnp.concatenate([a_st[hs], r_st[hs]], axis=0),
                                 jnp.concatenate([b_st[hs], k_st[hs]], axis=0)))
        zero = jnp.zeros((gr, gr), F32)

        def by_head(rows, cols):
            return jnp.concatenate(
                [jnp.concatenate([prods[0][rows, cols], zero], axis=1),
                 jnp.concatenate([zero, prods[1][rows, cols]], axis=1)], axis=0)

        top, bot = slice(0, gr), slice(gr, st)
        return dict(
            v=v, v_st=stack(v), a_st=a_st, r_st=r_st, b_rem=bt * g_rem, k_rem=k * g_rem,
            g_last=jnp.exp(clast),
            a_ab=jnp.where(strict, by_head(top, top), 0.0),
            a_ak=jnp.where(strict, by_head(top, bot), 0.0),
            a_rb=jnp.where(incl, by_head(bot, top), 0.0),
            a_rk=jnp.where(incl, by_head(bot, bot), 0.0))

    per_stage = -(-2 * groups // n_lev)
    cums = [log_decay_cumsum(g) for g in range(groups)]
    ps = [prep(g, cums[g]) for g in range(groups)]
    gs = range(groups)
    state_steps(per_stage)

    def lower_rows(m, bsz):
        return jnp.concatenate([m[i + bsz:i + 2 * bsz] for i in range(0, st, 2 * bsz)], axis=0)

    def scatter_lower(full, low, bsz):
        parts = []
        for n, i in enumerate(range(0, st, 2 * bsz)):
            parts += [full[i:i + bsz], low[n * bsz:(n + 1) * bsz]]
        return jnp.concatenate(parts, axis=0)

    xs = [jnp.where(eye_st, 1.0, jnp.where(lev == 0, p["a_ab"], 0.0)) for p in ps]
    for level in range(1, n_lev):
        bsz = 1 << level
        a_off = [jnp.where(lev == level, ps[g]["a_ab"], 0.0) for g in gs]
        if bsz % 8:
            ws = [_dot(a_off[g], xs[g]) for g in gs]
            xs = [xs[g] + _dot(xs[g], ws[g]) for g in gs]
        else:
            w_low = [_dot(lower_rows(a_off[g], bsz), xs[g]) for g in gs]
            zero = jnp.zeros((st, st), F32)
            upd = [_dot(lower_rows(xs[g], bsz), scatter_lower(zero, w_low[g], bsz)) for g in gs]
            xs = [scatter_lower(xs[g], lower_rows(xs[g], bsz) + upd[g], bsz) for g in gs]
        state_steps(per_stage)

    gm = [jnp.concatenate([_dot(ps[g]["a_ak"], ps[g]["v_st"]), ps[g]["a_st"]], axis=1) for g in gs]
    state_steps(2 * groups)
    h_ref[...] = state[0]
    tg = [_dot(xs[g], gm[g]) for g in gs]
    rb = [_dot(ps[g]["a_rb"], tg[g]) for g in gs]
    rk = [_dot(ps[g]["a_rk"], ps[g]["v_st"]) for g in gs]
    for g in gs:
        p = ps[g]
        yi_ref[g * gr:(g + 1) * gr, :] = unstack(rb[g][:, :LANES] + rk[g])
        rq_ref[g * gr:(g + 1) * gr, :] = unstack(p["r_st"] + rb[g][:, LANES:])
        uv = unstack(tg[g][:, :LANES])
        wa = unstack(tg[g][:, LANES:])
        for j in range(2):
            cs = slice(j * CHUNK, (j + 1) * CHUNK)
            lhs_t = jnp.concatenate([p["b_rem"][cs], p["k_rem"][cs]], axis=0)
            rhs_t = jnp.concatenate(
                [jnp.concatenate([uv[cs], wa[cs]], axis=1),
                 jnp.concatenate([p["v"][cs], jnp.zeros_like(p["v"][cs])], axis=1)], axis=0)
            nm = _dot_tn(lhs_t, rhs_t)
            n_ref[2 * g + j] = jnp.where(same_head, nm[:, :LANES], 0.0)
            decay = jnp.broadcast_to(p["g_last"][j * CHUNK:j * CHUNK + 1], (LANES, LANES))
            m_ref[2 * g + j] = (jnp.where(same_head, nm[:, LANES:], 0.0)
                                + jnp.where(eye_h, decay, 0.0))


def _rwkv_scan(r, lw, k, v, kk, a, groups=2):
    b, t, d = r.shape
    gr = 2 * CHUNK
    st = 2 * gr
    rows = gr * groups
    n_hp = d // LANES
    steps_per_seq = t // rows
    n_blocks = b * n_hp * steps_per_seq

    def block_index(blk):
        seq = blk // steps_per_seq
        return seq // n_hp, blk % steps_per_seq, seq % n_hp

    in_blk = pl.BlockSpec((1, rows, LANES), lambda s: block_index(jnp.minimum(s, n_blocks - 1)))
    out_blk = pl.BlockSpec((1, rows, LANES), lambda s: block_index(jnp.maximum(s - 1, 0)))
    idx = jnp.arange(st)
    xor = idx[:, None] ^ idx[None, :]
    same = (idx[:, None] // CHUNK) == (idx[None, :] // CHUNK)
    lower = idx[None, :] < idx[:, None]
    msb = jnp.floor(jnp.log2(jnp.maximum(xor, 1).astype(F32))).astype(jnp.int32)
    lev = jnp.where(same & lower, msb, -1).astype(jnp.int32)
    ti = jnp.arange(gr)
    tri = ((ti[:, None] >= ti[None, :]) & ((ti[:, None] // CHUNK) == (ti[None, :] // CHUNK))).astype(BF16)
    return pl.pallas_call(
        functools.partial(_rwkv_scan_kernel, groups=groups, steps_per_seq=steps_per_seq),
        grid=(n_blocks + 1,),
        in_specs=[in_blk] * 6 + [_const_spec(lev.shape), _const_spec(tri.shape)],
        out_specs=out_blk,
        out_shape=jax.ShapeDtypeStruct((b, t, d), F32),
        scratch_shapes=[pltpu.VMEM((LANES, LANES), F32),
                        pltpu.VMEM((rows, LANES), F32), pltpu.VMEM((rows, LANES), F32),
                        pltpu.VMEM((2 * groups, LANES, LANES), F32),
                        pltpu.VMEM((2 * groups, LANES, LANES), F32)],
        compiler_params=_params("arbitrary"),
        name="rwkv7_chunk_scan",
    )(r, lw, k, v, kk, a, lev, tri)


def _rwkv_out_kernel(y_ref, r_ref, k_ref, v_ref, g_ref, x_ref, mod_ref, vec_ref, bd_ref, w_ref,
                     lng_ref, lnb_ref, o_ref, act_ref, *, d):
    bd = bd_ref[...]
    vec = vec_ref[...]
    inv = 1.0 / HEAD_DIM
    for j in range(d // SUM_W):
        sl = slice(j * SUM_W, (j + 1) * SUM_W)
        y = y_ref[0, :, sl]
        mean = _dot_exact_rhs(y, bd) * inv
        yc = y - mean
        var = _dot_exact_rhs(yc * yc, bd) * inv
        yn = yc * lax.rsqrt(var + GN_EPS) * vec[0:1, sl] + vec[1:2, sl]
        r = r_ref[0, :, sl].astype(F32)
        k = k_ref[0, :, sl].astype(F32)
        bonus = _dot_exact_rhs(r * k * vec[2:3, sl], bd) * v_ref[0, :, sl].astype(F32)
        act_ref[:, sl] = ((yn + bonus) * g_ref[0, :, sl].astype(F32)).astype(BF16)
    out = jnp.dot(act_ref[...], w_ref[...], preferred_element_type=F32)
    z = DEEPNORM_ALPHA * x_ref[0] + mod_ref[0][2:3] * out
    o_ref[0] = _layer_norm(z, lng_ref[...], lnb_ref[...])


def _rwkv_out(y, r, k, v, g, x, mod, vec, w, ln_g, ln_b, tm=512):
    b, t, d = x.shape
    bd = _head_sum_matrix()
    row_spec = pl.BlockSpec((1, tm, d), lambda i, j: (i, j, 0))
    return pl.pallas_call(
        functools.partial(_rwkv_out_kernel, d=d),
        grid=(b, t // tm),
        in_specs=[row_spec] * 6
        + [pl.BlockSpec((1, 6, d), lambda i, j: (i, 0, 0)),
           _const_spec(vec.shape), _const_spec(bd.shape), _const_spec(w.shape),
           _const_spec((1, d)), _const_spec((1, d))],
        out_specs=row_spec,
        out_shape=jax.ShapeDtypeStruct((b, t, d), F32),
        scratch_shapes=[pltpu.VMEM((tm, d), BF16)],
        compiler_params=_params("parallel", "parallel"),
        name="rwkv_out_proj_ln",
    )(y, r, k, v, g, x, mod, vec, bd, w, ln_g.reshape(1, d), ln_b.reshape(1, d))


def _trunk(x, c, ada_w, ada_b, ln_g, ln_b, ffn_w_in, ffn_w_out, fox_w_in, fox_b_f, fox_q_g, fox_k_g, fox_w_o, rwkv_mu, rwkv_w_rkv, rwkv_w0, rwkv_w1, rwkv_w2, rwkv_a0, rwkv_a1, rwkv_a2, rwkv_g1, rwkv_g2, rwkv_k_k, rwkv_k_a, rwkv_r_k, rwkv_lnx_g, rwkv_lnx_b, rwkv_w_o,
           *, tm, tm_rwkv, tq, tk, groups, tn, nc, fc):
    b, t, d = x.shape
    n_heads = d // HEAD_DIM
    mods = _mods(c, ada_w, ada_b, tn=tn)

    w_in = fox_w_in[0]
    f_lo = 3 * d
    w_qko = jnp.concatenate([w_in[:, :2 * d], w_in[:, f_lo + n_heads:]], axis=1).astype(BF16)
    w_vt = w_in[:, 2 * d:f_lo].T.astype(BF16)
    w_f = jnp.pad(w_in[:, f_lo:f_lo + n_heads], ((0, 0), (0, LANES - n_heads))).astype(BF16)
    b_f = jnp.pad(fox_b_f[0], (0, LANES - n_heads)).reshape(1, LANES)
    q, k, og, v, logit_bound = _fox_in(x, mods[0], w_qko, w_vt, w_f, b_f, fox_q_g[0], fox_k_g[0],
                                       tm=tm, nc=nc)
    att = _fox_attention(q, k, v, og, logit_bound, tq=tq, tk=tk)
    x = _proj_ln(att, x, mods[0], fox_w_o[0].astype(BF16), ln_g[0, 0], ln_b[0, 0], tm=tm)
    x = _ffn(x, mods[0], ffn_w_in[0].astype(BF16), ffn_w_out[0].astype(BF16), ln_g[0, 1], ln_b[0, 1],
             tm=tm, fc=fc)

    vec_in = jnp.stack([rwkv_w0[0], rwkv_a0[0], rwkv_k_k[0], rwkv_k_a[0]])
    r, lw, k, v, kk, a, g = _rwkv_in(
        x, mods[1], rwkv_mu[0], rwkv_w_rkv[0].astype(BF16),
        rwkv_w1[0].astype(BF16), rwkv_w2[0].astype(BF16),
        rwkv_a1[0].astype(BF16), rwkv_a2[0].astype(BF16),
        rwkv_g1[0].astype(BF16), rwkv_g2[0].astype(BF16), vec_in, tm=tm_rwkv)
    y = _rwkv_scan(r, lw, k, v, kk, a, groups=groups)
    vec_out = jnp.stack([rwkv_lnx_g[0], rwkv_lnx_b[0], rwkv_r_k[0].reshape(d)])
    x = _rwkv_out(y, r, k, v, g, x, mods[1], vec_out, rwkv_w_o[0].astype(BF16), ln_g[1, 0], ln_b[1, 0],
                  tm=tm)
    x = _ffn(x, mods[1], ffn_w_in[1].astype(BF16), ffn_w_out[1].astype(BF16), ln_g[1, 1], ln_b[1, 1],
             tm=tm, fc=fc)
    return x


def kernel(x, c, ada_w, ada_b, ln_g, ln_b, ffn_w_in, ffn_w_out, fox_w_in, fox_b_f, fox_q_g, fox_k_g, fox_w_o, rwkv_mu, rwkv_w_rkv, rwkv_w0, rwkv_w1, rwkv_w2, rwkv_a0, rwkv_a1, rwkv_a2, rwkv_g1, rwkv_g2, rwkv_k_k, rwkv_k_a, rwkv_r_k, rwkv_lnx_g, rwkv_lnx_b, rwkv_w_o):
    return _trunk(x, c, ada_w, ada_b, ln_g, ln_b, ffn_w_in, ffn_w_out, fox_w_in, fox_b_f, fox_q_g, fox_k_g, fox_w_o, rwkv_mu, rwkv_w_rkv, rwkv_w0, rwkv_w1, rwkv_w2, rwkv_a0, rwkv_a1, rwkv_a2, rwkv_g1, rwkv_g2, rwkv_k_k, rwkv_k_a, rwkv_r_k, rwkv_lnx_g, rwkv_lnx_b, rwkv_w_o,
                  tm=512, tm_rwkv=256, tq=512, tk=256, groups=4, tn=1536, nc=512, fc=256)
```

```python
import functools

import jax
import jax.numpy as jnp
import numpy as np
from jax import lax
from jax.experimental import pallas as pl
from jax.experimental.pallas import tpu as pltpu

F32 = jnp.float32
BF16 = jnp.bfloat16

HEAD_DIM = 64
DEPTH = 2
DEEPNORM_ALPHA = (2 * DEPTH) ** 0.25
LN_EPS = 1e-5
QK_EPS = 1e-6
GN_EPS = HEAD_DIM * 1e-5
LANES = 128
SUM_W = 256
CHUNK = 64
NEG_BIG = -1e30
LOG2E = 1.4426950408889634
DECAY_SCALE = -0.6065306597126334
SAFE_LOGIT_RANGE = 96.0
N_SPLIT = 3
VMEM_LIMIT = 56 * 1024 * 1024


def _dot(a, b):
    return jnp.dot(a.astype(BF16), b.astype(BF16), preferred_element_type=F32)


def _dot_nt(a, b):
    return lax.dot_general(a.astype(BF16), b.astype(BF16), (((1,), (1,)), ((), ())),
                           preferred_element_type=F32)


def _dot_tn(a, b):
    return lax.dot_general(a.astype(BF16), b.astype(BF16), (((0,), (0,)), ((), ())),
                           preferred_element_type=F32)


def _split2(a):
    hi = a.astype(BF16)
    lo = (a - hi.astype(F32)).astype(BF16)
    return hi, lo


def _split3(a):
    hi = a.astype(BF16)
    r1 = a - hi.astype(F32)
    mid = r1.astype(BF16)
    lo = (r1 - mid.astype(F32)).astype(BF16)
    return hi, mid, lo


def _dot3(a, b):
    ah, al = _split2(a)
    bh, bl = _split2(b)
    d = functools.partial(jnp.dot, preferred_element_type=F32)
    return d(ah, bh) + (d(al, bh) + d(ah, bl))


def _dot_exact_rhs(a, b_exact):
    ah, al = _split2(a)
    d = functools.partial(jnp.dot, preferred_element_type=F32)
    return d(ah, b_exact) + d(al, b_exact)


def _head_sum_matrix():
    idx = jnp.arange(SUM_W) // HEAD_DIM
    return (idx[:, None] == idx[None, :]).astype(BF16)


def _layer_norm(z, g, b):
    mu = jnp.mean(z, axis=-1, keepdims=True)
    zc = z - mu
    var = jnp.mean(zc * zc, axis=-1, keepdims=True)
    return zc * lax.rsqrt(var + LN_EPS) * g + b


def _head_sums(a, head_sum_matrix):
    return jnp.dot(a.astype(BF16), head_sum_matrix, preferred_element_type=F32)


def _params(*sem):
    return pltpu.CompilerParams(dimension_semantics=sem, vmem_limit_bytes=VMEM_LIMIT)


def _const_spec(shape):
    nd = len(shape)
    return pl.BlockSpec(shape, lambda *_: (0,) * nd, pipeline_mode=pl.Buffered(1))


def _mods_kernel(c_ref, w_ref, b_ref, o_ref):
    c = c_ref[...]
    ca = c * jax.nn.sigmoid(c)
    o_ref[0] = _dot3(ca, w_ref[0]) + b_ref[0]


def _mods(c, ada_w, ada_b, tn=1536):
    depth, d, n = ada_w.shape
    b = c.shape[0]
    rows = 8
    cp = jnp.pad(c, ((0, rows - b), (0, 0)))
    out = pl.pallas_call(
        _mods_kernel,
        grid=(depth, n // tn),
        in_specs=[pl.BlockSpec((rows, d), lambda l, j: (0, 0)),
                  pl.BlockSpec((1, d, tn), lambda l, j: (l, 0, j)),
                  pl.BlockSpec((1, 1, tn), lambda l, j: (l, 0, j))],
        out_specs=pl.BlockSpec((1, rows, tn), lambda l, j: (l, 0, j)),
        out_shape=jax.ShapeDtypeStruct((depth, rows, n), F32),
        compiler_params=_params("parallel", "parallel"),
        name="adaln_mods",
    )(cp, ada_w, ada_b.reshape(depth, 1, n))
    return out[:, :b].reshape(depth, b, 6, d)


def _fox_in_kernel(x_ref, mod_ref, w_ref, wf_ref, bf_ref, qg_ref, kg_ref, bd_ref, tri_ref,
                   pq_ref, pk_ref, oq_ref, ok_ref, wvt_ref,
                   q_ref, k_ref, o_ref, vt_ref, carry_ref, *, d, nc):
    t = pl.program_id(1)
    x = x_ref[0]
    mod = mod_ref[0]
    h = (x * (1.0 + mod[1:2]) + mod[0:1]).astype(BF16)
    bd = bd_ref[...]
    qg = qg_ref[...]
    kg = kg_ref[...]
    dd = functools.partial(jnp.dot, preferred_element_type=F32)

    fl = dd(h, wf_ref[...]) + bf_ref[...]
    lf = jnp.minimum(fl, 0.0) - jnp.log(1.0 + jnp.exp(-jnp.abs(fl)))
    tri = tri_ref[...]
    hi, mid, lo = _split3(lf)
    cs = dd(tri, hi) + (dd(tri, mid) + dd(tri, lo))

    @pl.when(t == 0)
    def _():
        carry_ref[...] = jnp.zeros_like(carry_ref)

    f = cs + carry_ref[0:1, :]
    tm = f.shape[0]
    carry_ref[...] = jnp.broadcast_to(f[tm - 1:tm, :], carry_ref.shape)
    n_heads = d // HEAD_DIM
    lane = lax.broadcasted_iota(jnp.int32, (tm, LANES), 1)
    parts = [jnp.where(lane < n_heads, p.astype(F32), 0.0) for p in _split3(f * LOG2E)]
    fpack = parts[0]
    for i in range(1, N_SPLIT):
        fpack = fpack + pltpu.roll(parts[i], i * n_heads, axis=1)
    fpack = fpack.astype(BF16)
    low = lane < HEAD_DIM

    def head_rms(zp, g):
        ss = _head_sums(zp * zp, bd)
        return zp * lax.rsqrt(ss * (1.0 / HEAD_DIM) + QK_EPS) * g

    def store_augmented(out_ref, zj, tile, p_ref, ones_ref):
        sl = slice(2 * tile * LANES, (2 * tile + 2) * LANES)
        aug = dd(fpack, p_ref[:, sl]) + ones_ref[:, sl]
        out_ref[0, :, sl] = jnp.concatenate(
            [jnp.where(low, zj, aug[:, :LANES]), jnp.where(low, aug[:, LANES:], zj)],
            axis=1).astype(out_ref.dtype)

    for sec, out_ref in enumerate((q_ref, k_ref, o_ref)):
        for c in range(d // nc):
            col = sec * d + c * nc
            z = dd(h, w_ref[:, col:col + nc])
            if sec == 2:
                out_ref[0, :, c * nc:(c + 1) * nc] = jax.nn.sigmoid(z).astype(out_ref.dtype)
                continue
            for pr in range(nc // SUM_W):
                zn = head_rms(z[:, pr * SUM_W:(pr + 1) * SUM_W], qg if sec == 0 else kg)
                for j in range(SUM_W // LANES):
                    tile = (c * nc + pr * SUM_W) // LANES + j
                    store_augmented(out_ref, zn[:, j * LANES:(j + 1) * LANES], tile,
                                    pq_ref if sec == 0 else pk_ref,
                                    oq_ref if sec == 0 else ok_ref)
    for c in range(d // nc):
        vt_ref[0, c * nc:(c + 1) * nc, :] = lax.dot_general(
            wvt_ref[c * nc:(c + 1) * nc, :], h, (((1,), (1,)), ((), ())),
            preferred_element_type=F32).astype(vt_ref.dtype)


def _fox_in(x, mod, w_qko, w_vt, w_f, b_f, q_g, k_g, tm=512, nc=512):
    b, t, d = x.shape
    n_heads = d // HEAD_DIM
    assert N_SPLIT * n_heads <= LANES
    qg = jnp.tile(q_g * (HEAD_DIM ** -0.5 * LOG2E), SUM_W // HEAD_DIM).reshape(1, SUM_W)
    kg = jnp.tile(k_g, SUM_W // HEAD_DIM).reshape(1, SUM_W)
    logit_bound = HEAD_DIM * jnp.max(jnp.abs(qg)) * jnp.max(jnp.abs(kg))
    bd = _head_sum_matrix()
    tri = (jnp.arange(tm)[:, None] >= jnp.arange(tm)[None, :]).astype(BF16)
    pq = np.zeros((LANES, n_heads * LANES), np.float32)
    pk = np.zeros_like(pq)
    oq = np.zeros((1, n_heads * LANES), np.float32)
    ok = np.zeros_like(oq)
    shift_lane = np.zeros_like(oq)
    for hd in range(n_heads):
        base = hd * LANES + (HEAD_DIM if hd % 2 == 0 else 0)
        for part in range(N_SPLIT):
            pq[part * n_heads + hd, base + N_SPLIT + part] = 1.0
            ok[0, base + N_SPLIT + part] = 1.0
            pk[part * n_heads + hd, base + part] = -1.0
            oq[0, base + part] = 1.0
        shift_lane[0, base + 2 * N_SPLIT] = 1.0
    pq = jnp.asarray(pq, BF16)
    pk = jnp.asarray(pk, BF16)
    oq = jnp.asarray(oq) - logit_bound * jnp.asarray(shift_lane)
    ok = jnp.asarray(ok + shift_lane)
    act = jax.ShapeDtypeStruct((b, t, d), BF16)
    aug = jax.ShapeDtypeStruct((b, t, n_heads * LANES), BF16)
    row_spec = pl.BlockSpec((1, tm, d), lambda i, j: (i, j, 0))
    aug_spec = pl.BlockSpec((1, tm, n_heads * LANES), lambda i, j: (i, j, 0))
    kern = functools.partial(_fox_in_kernel, d=d, nc=nc)
    consts = (w_qko, w_f, b_f, qg, kg, bd, tri, pq, pk, oq, ok, w_vt)
    q_aug, k_aug, og, vt = pl.pallas_call(
        kern,
        grid=(b, t // tm),
        in_specs=[row_spec, pl.BlockSpec((1, 6, d), lambda i, j: (i, 0, 0))]
        + [_const_spec(c.shape) for c in consts],
        out_specs=[aug_spec, aug_spec, row_spec, pl.BlockSpec((1, d, tm), lambda i, j: (i, 0, j))],
        out_shape=[aug, aug, act, jax.ShapeDtypeStruct((b, d, t), BF16)],
        scratch_shapes=[pltpu.VMEM((8, LANES), F32)],
        compiler_params=_params("parallel", "arbitrary"),
        name="fox_in_proj",
    )(x, mod, *consts)
    return q_aug, k_aug, og, vt, logit_bound


def _attn_kernel(q_ref, k_ref, vt_ref, g_ref, y_ref, *, tq, tk):
    qi = pl.program_id(2)
    heads = LANES // HEAD_DIM
    pair = 2 * tk
    key = lax.broadcasted_iota(jnp.int32, (tk, tq), 0)
    qry = lax.broadcasted_iota(jnp.int32, (tk, tq), 1) + qi * tq
    qs = [q_ref[0, :, hh * LANES:(hh + 1) * LANES] for hh in range(heads)]

    def step(j, carry, masked):
        starts = [pl.multiple_of(j * pair + bb * tk, tk) for bb in range(2)]

        def scores(hh, bb):
            z = _dot_nt(k_ref[0, pl.ds(starts[bb], tk), hh * LANES:(hh + 1) * LANES], qs[hh])
            if masked:
                z = jnp.where(key + starts[bb] <= qry, z, NEG_BIG)
            return z

        def weights(z, m):
            p = jnp.exp2(z - m)
            return p, jnp.sum(p, axis=0, keepdims=True)

        def values(hh, bb, p):
            vt = vt_ref[0, hh * HEAD_DIM:(hh + 1) * HEAD_DIM, pl.ds(starts[bb], tk)]
            return jnp.dot(vt, p.astype(BF16), preferred_element_type=F32)

        colmax = lambda z: jnp.max(z, axis=0, keepdims=True)
        (m0, l0, a0), (m1, l1, a1) = carry
        z00 = scores(0, 0)
        z10 = scores(1, 0)
        m0a = jnp.maximum(m0, colmax(z00))
        z01 = scores(0, 1)
        p00, s00 = weights(z00, m0a)
        m1a = jnp.maximum(m1, colmax(z10))
        z11 = scores(1, 1)
        v00 = values(0, 0, p00)
        p10, s10 = weights(z10, m1a)
        m0b = jnp.maximum(m0a, colmax(z01))
        v10 = values(1, 0, p10)
        p01, s01 = weights(z01, m0b)
        m1b = jnp.maximum(m1a, colmax(z11))
        v01 = values(0, 1, p01)
        p11, s11 = weights(z11, m1b)
        v11 = values(1, 1, p11)

        def merge(m, l, a, ma, mb, sa, sb, va, vb):
            ra = jnp.exp2(m - ma)
            rb = jnp.exp2(ma - mb)
            return mb, rb * (ra * l + sa) + sb, rb * (ra * a + va) + vb

        return (merge(m0, l0, a0, m0a, m0b, s00, s01, v00, v01),
                merge(m1, l1, a1, m1a, m1b, s10, s11, v10, v11))

    init = tuple((jnp.full((1, tq), NEG_BIG, F32), jnp.zeros((1, tq), F32),
                  jnp.zeros((HEAD_DIM, tq), F32)) for _ in range(heads))
    n_full = (qi * tq) // pair
    carry = lax.fori_loop(0, n_full, functools.partial(step, masked=False), init)
    for jm in range(tq // pair):
        carry = step(n_full + jm, carry, True)
    yt = jnp.concatenate([carry[hh][2] / carry[hh][1] for hh in range(heads)], axis=0)
    y_ref[0] = (yt.T * g_ref[0].astype(F32)).astype(y_ref.dtype)


def _attn_bounded_kernel(q_ref, k_ref, vt_ref, g_ref, y_ref, *, tq, tk,
                         chain_keys=512, long_blocks=2, short_blocks=1):
    del tk
    qi = pl.program_id(2)
    heads = LANES // HEAD_DIM
    tk = min(chain_keys, tq)
    key = lax.broadcasted_iota(jnp.int32, (tk, tq), 0)
    qry = lax.broadcasted_iota(jnp.int32, (tk, tq), 1) + qi * tq
    qs = [q_ref[0, :, hh * LANES:(hh + 1) * LANES] for hh in range(heads)]

    def step(j, carry, masked, nb, base=0):
        chains = [(bb, hh) for bb in range(nb) for hh in range(heads)]
        starts = [pl.multiple_of(base + j * (nb * tk) + bb * tk, tk) for bb in range(nb)]

        def scores(bb, hh):
            z = _dot_nt(k_ref[0, pl.ds(starts[bb], tk), hh * LANES:(hh + 1) * LANES], qs[hh])
            if masked:
                z = jnp.where(key + starts[bb] <= qry, z, NEG_BIG)
            return z

        sums = [[] for _ in range(heads)]
        vals = [[] for _ in range(heads)]
        z_next = scores(*chains[0])
        for c, (bb, hh) in enumerate(chains):
            z = z_next
            if c + 1 < len(chains):
                z_next = scores(*chains[c + 1])
            p = jnp.exp2(z)
            vt = vt_ref[0, hh * HEAD_DIM:(hh + 1) * HEAD_DIM, pl.ds(starts[bb], tk)]
            sums[hh].append(jnp.sum(p, axis=0, keepdims=True))
            vals[hh].append(jnp.dot(vt, p.astype(BF16), preferred_element_type=F32))
        return tuple((carry[hh][0] + functools.reduce(jnp.add, sums[hh]),
                      carry[hh][1] + functools.reduce(jnp.add, vals[hh])) for hh in range(heads))

    init = tuple((jnp.zeros((1, tq), F32), jnp.zeros((HEAD_DIM, tq), F32)) for _ in range(heads))
    n_long = (qi * tq) // (long_blocks * tk)
    carry = lax.fori_loop(0, n_long, functools.partial(step, masked=False, nb=long_blocks), init)
    rest = n_long * (long_blocks * tk)
    n_short = (qi * tq - rest) // (short_blocks * tk)
    carry = lax.fori_loop(0, n_short,
                          functools.partial(step, masked=False, nb=short_blocks, base=rest), carry)
    carry = step(0, carry, True, nb=tq // tk, base=qi * tq)
    yt = jnp.concatenate([carry[hh][1] / carry[hh][0] for hh in range(heads)], axis=0)
    y_ref[0] = (yt.T * g_ref[0].astype(F32)).astype(y_ref.dtype)


def _fox_attention(q_aug, k_aug, vt, og, logit_bound, tq=512, tk=512):
    return lax.cond(2.0 * logit_bound <= SAFE_LOGIT_RANGE,
                    functools.partial(_attention_call, kernel_fn=_attn_bounded_kernel, tq=tq, tk=tk),
                    functools.partial(_attention_call, kernel_fn=_attn_kernel, tq=tq, tk=tk),
                    q_aug, k_aug, vt, og)


def _attention_call(q_aug, k_aug, vt, og, *, kernel_fn, tq, tk):
    b, d, t = vt.shape
    heads = LANES // HEAD_DIM
    blk = pl.BlockSpec((1, tq, LANES), lambda i, p, j: (i, j, p))
    q_blk = pl.BlockSpec((1, tq, heads * LANES), lambda i, p, j: (i, j, p))
    k_full = pl.BlockSpec((1, t, heads * LANES), lambda i, p, j: (i, 0, p))
    vt_full = pl.BlockSpec((1, LANES, t), lambda i, p, j: (i, p, 0))
    return pl.pallas_call(
        functools.partial(kernel_fn, tq=tq, tk=tk),
        grid=(b, d // LANES, t // tq),
        in_specs=[q_blk, k_full, vt_full, blk],
        out_specs=blk,
        out_shape=jax.ShapeDtypeStruct((b, t, d), BF16),
        compiler_params=_params("parallel", "parallel", "arbitrary"),
        name="fox_attention_bounded" if kernel_fn is _attn_bounded_kernel else "fox_attention",
    )(q_aug, k_aug, vt, og)


def _proj_ln_kernel(a_ref, x_ref, mod_ref, w_ref, lng_ref, lnb_ref, o_ref):
    y = jnp.dot(a_ref[0], w_ref[...], preferred_element_type=F32)
    z = DEEPNORM_ALPHA * x_ref[0] + mod_ref[0][2:3] * y
    o_ref[0] = _layer_norm(z, lng_ref[...], lnb_ref[...])


def _proj_ln(a, x, mod, w, ln_g, ln_b, tm=512):
    b, t, d = x.shape
    row_spec = pl.BlockSpec((1, tm, d), lambda i, j: (i, j, 0))
    return pl.pallas_call(
        _proj_ln_kernel,
        grid=(b, t // tm),
        in_specs=[row_spec, row_spec,
                  pl.BlockSpec((1, 6, d), lambda i, j: (i, 0, 0)),
                  _const_spec(w.shape), _const_spec((1, d)), _const_spec((1, d))],
        out_specs=row_spec,
        out_shape=jax.ShapeDtypeStruct((b, t, d), F32),
        compiler_params=_params("parallel", "parallel"),
        name="attn_out_proj_ln",
    )(a, x, mod, w, ln_g.reshape(1, d), ln_b.reshape(1, d))


def _ffn_kernel(x_ref, mod_ref, win_ref, wout_ref, lng_ref, lnb_ref, o_ref, act_ref, *, d_ff, fc):
    x = x_ref[0]
    mod = mod_ref[0]
    h = (x * (1.0 + mod[4:5]) + mod[3:4]).astype(BF16)
    for c in range(d_ff // fc):
        g = jnp.dot(h, win_ref[:, c * fc:(c + 1) * fc], preferred_element_type=F32)
        u = jnp.dot(h, win_ref[:, d_ff + c * fc:d_ff + (c + 1) * fc], preferred_element_type=F32)
        act_ref[:, c * fc:(c + 1) * fc] = (g * jax.nn.sigmoid(g) * u).astype(BF16)
    y = jnp.dot(act_ref[...], wout_ref[...], preferred_element_type=F32)
    z = DEEPNORM_ALPHA * x + mod[5:6] * y
    o_ref[0] = _layer_norm(z, lng_ref[...], lnb_ref[...])


def _ffn(x, mod, w_in, w_out, ln_g, ln_b, tm=512, fc=256):
    b, t, d = x.shape
    d_ff = w_out.shape[0]
    row_spec = pl.BlockSpec((1, tm, d), lambda i, j: (i, j, 0))
    return pl.pallas_call(
        functools.partial(_ffn_kernel, d_ff=d_ff, fc=fc),
        grid=(b, t // tm),
        in_specs=[row_spec,
                  pl.BlockSpec((1, 6, d), lambda i, j: (i, 0, 0)),
                  _const_spec(w_in.shape), _const_spec(w_out.shape),
                  _const_spec((1, d)), _const_spec((1, d))],
        out_specs=row_spec,
        out_shape=jax.ShapeDtypeStruct((b, t, d), F32),
        scratch_shapes=[pltpu.VMEM((tm, d_ff), BF16)],
        compiler_params=_params("parallel", "parallel"),
        name="swiglu_ln",
    )(x, mod, w_in, w_out, ln_g.reshape(1, d), ln_b.reshape(1, d))


def _rwkv_in_kernel(x_ref, xp_ref, mod_ref, mu_ref, wrkv_ref, w1_ref, w2_ref, a1_ref, a2_ref,
                    g1_ref, g2_ref, vec_ref, bd_ref,
                    r_ref, lw_ref, k_ref, v_ref, kk_ref, a_ref, g_ref, *, d):
    t = pl.program_id(1)
    mod = mod_ref[0]
    sc = 1.0 + mod[1:2]
    sh = mod[0:1]
    h = x_ref[0] * sc + sh
    tm = h.shape[0]
    prev = xp_ref[0][7:8, :] * sc + sh
    prev = jnp.where(t == 0, jnp.zeros_like(prev), prev)
    rows = lax.broadcasted_iota(jnp.int32, h.shape, 0)
    hprev = jnp.where(rows == 0, prev, pltpu.roll(h, 1, axis=0))
    dx = hprev - h
    mu = mu_ref[...]
    vec = vec_ref[...]
    w0, a0, k_k, k_a = vec[0:1], vec[1:2], vec[2:3], vec[3:4]

    def mix(n):
        return (h + dx * mu[n:n + 1]).astype(BF16)

    dd = functools.partial(jnp.dot, preferred_element_type=F32)
    r = dd(mix(0), wrkv_ref[0])
    k = dd(mix(1), wrkv_ref[1])
    v = dd(mix(2), wrkv_ref[2])
    ww = w0 + dd(jnp.tanh(dd(mix(3), w1_ref[...])).astype(BF16), w2_ref[...])
    lw = DECAY_SCALE * jax.nn.sigmoid(ww)
    a = jax.nn.sigmoid(a0 + dd(dd(mix(4), a1_ref[...]).astype(BF16), a2_ref[...]))
    g = dd(jax.nn.sigmoid(dd(mix(5), g1_ref[...])).astype(BF16), g2_ref[...])
    kk = k * k_k
    k = k * (1.0 + (a - 1.0) * k_a)
    bd = bd_ref[...]
    for j in range(d // SUM_W):
        sl = slice(j * SUM_W, (j + 1) * SUM_W)
        kkj = kk[:, sl]
        ss = _head_sums(kkj * kkj, bd)
        kk_ref[0, :, sl] = (kkj * jnp.minimum(lax.rsqrt(ss), 1e12)).astype(kk_ref.dtype)
    r_ref[0] = r.astype(r_ref.dtype)
    lw_ref[0] = lw
    k_ref[0] = k.astype(k_ref.dtype)
    v_ref[0] = v.astype(v_ref.dtype)
    a_ref[0] = a.astype(a_ref.dtype)
    g_ref[0] = g.astype(g_ref.dtype)


def _rwkv_in(x, mod, mu, w_rkv, w1, w2, a1, a2, g1, g2, vec, tm=512):
    b, t, d = x.shape
    bd = _head_sum_matrix()
    row_spec = pl.BlockSpec((1, tm, d), lambda i, j: (i, j, 0))
    prev_spec = pl.BlockSpec((1, 8, d), lambda i, j: (i, jnp.maximum(j * (tm // 8) - 1, 0), 0))
    act = jax.ShapeDtypeStruct((b, t, d), BF16)
    consts = (mu, w_rkv, w1, w2, a1, a2, g1, g2, vec, bd)
    return pl.pallas_call(
        functools.partial(_rwkv_in_kernel, d=d),
        grid=(b, t // tm),
        in_specs=[row_spec, prev_spec, pl.BlockSpec((1, 6, d), lambda i, j: (i, 0, 0))]
        + [_const_spec(c.shape) for c in consts],
        out_specs=[row_spec] * 7,
        out_shape=[act, jax.ShapeDtypeStruct((b, t, d), F32), act, act, act, act, act],
        compiler_params=_params("parallel", "parallel"),
        name="rwkv_in_proj",
    )(x, x, mod, *consts)


def _rwkv_scan_kernel(r_ref, lw_ref, k_ref, v_ref, kk_ref, a_ref, lev_ref, tri_ref,
                      y_ref, h_ref, yi_ref, rq_ref, m_ref, n_ref, *, groups, steps_per_seq):
    s = pl.program_id(0)
    gr = 2 * CHUNK
    st = 2 * gr
    n_lev = CHUNK.bit_length() - 1

    @pl.when(s == 0)
    def _():
        h_ref[...] = jnp.zeros_like(h_ref)
        yi_ref[...] = jnp.zeros_like(yi_ref)
        rq_ref[...] = jnp.zeros_like(rq_ref)
        m_ref[...] = jnp.zeros_like(m_ref)
        n_ref[...] = jnp.zeros_like(n_ref)

    first_of_seq = lax.rem(jnp.maximum(s - 1, 0), steps_per_seq) == 0
    state = [jnp.where(first_of_seq, 0.0, h_ref[...])]
    pending = list(range(2 * groups))

    def state_steps(count):
        for _ in range(count):
            if pending:
                i = pending.pop(0)
                lo = i * CHUNK
                h = state[0]
                y_ref[0, lo:lo + CHUNK, :] = (yi_ref[lo:lo + CHUNK, :]
                                              + _dot(rq_ref[lo:lo + CHUNK, :], h))
                state[0] = _dot3(m_ref[i], h) + n_ref[i]

    lane = lax.broadcasted_iota(jnp.int32, (gr, LANES), 1)
    head0 = lane < HEAD_DIM
    lev = lev_ref[...]
    tri = tri_ref[...]
    strict = lev >= 0
    srow = lax.broadcasted_iota(jnp.int32, (st, st), 0)
    scol = lax.broadcasted_iota(jnp.int32, (st, st), 1)
    eye_st = srow == scol
    incl = strict | eye_st
    hrow = lax.broadcasted_iota(jnp.int32, (LANES, LANES), 0)
    hcol = lax.broadcasted_iota(jnp.int32, (LANES, LANES), 1)
    same_head = (hrow < HEAD_DIM) == (hcol < HEAD_DIM)
    eye_h = hrow == hcol

    def stack(x):
        zero = jnp.zeros_like(x)
        return jnp.concatenate([jnp.where(head0, x, zero), jnp.where(head0, zero, x)], axis=0)

    def unstack(x):
        return x[:gr] + x[gr:]

    def log_decay_cumsum(g):
        hi, mid, lo = _split3(lw_ref[0, g * gr:(g + 1) * gr, :])
        dd = functools.partial(jnp.dot, preferred_element_type=F32)
        return dd(tri, hi) + (dd(tri, mid) + dd(tri, lo))

    def prep(g, cum):
        rs = slice(g * gr, (g + 1) * gr)
        lw = lw_ref[0, rs, :]
        clast = jnp.concatenate(
            [jnp.broadcast_to(cum[(j + 1) * CHUNK - 1:(j + 1) * CHUNK], (CHUNK, LANES))
             for j in range(2)], axis=0)
        g_inv = jnp.exp(-cum)
        g_rem = jnp.exp(clast - cum)
        r = r_ref[0, rs, :].astype(F32)
        k = k_ref[0, rs, :].astype(F32)
        v = v_ref[0, rs, :].astype(F32)
        kk = kk_ref[0, rs, :].astype(F32)
        bt = kk * a_ref[0, rs, :].astype(F32)
        a_st = stack(-kk * jnp.exp(cum - lw))
        r_st = stack(r * jnp.exp(cum))
        b_st = stack(bt * g_inv)
        k_st = stack(k * g_inv)
        prods = []
        for hd in range(2):
            hs = slice(hd * gr, (hd + 1) * gr)
            prods.append(_dot_nt(jnp.concatenate([a_st[hs], r_st[hs]], axis=0),
                                 jnp.concatenate([b_st[hs], k_st[hs]], axis=0)))
        zero = jnp.zeros((gr, gr), F32)

        def by_head(rows, cols):
            return jnp.concatenate(
                [jnp.concatenate([prods[0][rows, cols], zero], axis=1),
                 jnp.concatenate([zero, prods[1][rows, cols]], axis=1)], axis=0)

        top, bot = slice(0, gr), slice(gr, st)
        return dict(
            v=v, v_st=stack(v), a_st=a_st, r_st=r_st, b_rem=bt * g_rem, k_rem=k * g_rem,
            g_last=jnp.exp(clast),
            a_ab=jnp.where(strict, by_head(top, top), 0.0),
            a_ak=jnp.where(strict, by_head(top, bot), 0.0),
            a_rb=jnp.where(incl, by_head(bot, top), 0.0),
            a_rk=jnp.where(incl, by_head(bot, bot), 0.0))

    per_stage = -(-2 * groups // n_lev)
    cums = [log_decay_cumsum(g) for g in range(groups)]
    ps = [prep(g, cums[g]) for g in range(groups)]
    gs = range(groups)
    state_steps(per_stage)

    def lower_rows(m, bsz):
        return jnp.concatenate([m[i + bsz:i + 2 * bsz] for i in range(0, st, 2 * bsz)], axis=0)

    def scatter_lower(full, low, bsz):
        parts = []
        for n, i in enumerate(range(0, st, 2 * bsz)):
            parts += [full[i:i + bsz], low[n * bsz:(n + 1) * bsz]]
        return jnp.concatenate(parts, axis=0)

    xs = [jnp.where(eye_st, 1.0, jnp.where(lev == 0, p["a_ab"], 0.0)) for p in ps]
    for level in range(1, n_lev):
        bsz = 1 << level
        a_off = [jnp.where(lev == level, ps[g]["a_ab"], 0.0) for g in gs]
        if bsz % 8:
            ws = [_dot(a_off[g], xs[g]) for g in gs]
            xs = [xs[g] + _dot(xs[g], ws[g]) for g in gs]
        else:
            w_low = [_dot(lower_rows(a_off[g], bsz), xs[g]) for g in gs]
            zero = jnp.zeros((st, st), F32)
            upd = [_dot(lower_rows(xs[g], bsz), scatter_lower(zero, w_low[g], bsz)) for g in gs]
            xs = [scatter_lower(xs[g], lower_rows(xs[g], bsz) + upd[g], bsz) for g in gs]
        state_steps(per_stage)

    gm = [jnp.concatenate([_dot(ps[g]["a_ak"], ps[g]["v_st"]), ps[g]["a_st"]], axis=1) for g in gs]
    state_steps(2 * groups)
    h_ref[...] = state[0]
    tg = [_dot(xs[g], gm[g]) for g in gs]
    rb = [_dot(ps[g]["a_rb"], tg[g]) for g in gs]
    rk = [_dot(ps[g]["a_rk"], ps[g]["v_st"]) for g in gs]
    for g in gs:
        p = ps[g]
        yi_ref[g * gr:(g + 1) * gr, :] = unstack(rb[g][:, :LANES] + rk[g])
        rq_ref[g * gr:(g + 1) * gr, :] = unstack(p["r_st"] + rb[g][:, LANES:])
        uv = unstack(tg[g][:, :LANES])
        wa = unstack(tg[g][:, LANES:])
        for j in range(2):
            cs = slice(j * CHUNK, (j + 1) * CHUNK)
            lhs_t = jnp.concatenate([p["b_rem"][cs], p["k_rem"][cs]], axis=0)
            rhs_t = jnp.concatenate(
                [jnp.concatenate([uv[cs], wa[cs]], axis=1),
                 jnp.concatenate([p["v"][cs], jnp.zeros_like(p["v"][cs])], axis=1)], axis=0)
            nm = _dot_tn(lhs_t, rhs_t)
            n_ref[2 * g + j] = jnp.where(same_head, nm[:, :LANES], 0.0)
            decay = jnp.broadcast_to(p["g_last"][j * CHUNK:j * CHUNK + 1], (LANES, LANES))
            m_ref[2 * g + j] = (jnp.where(same_head, nm[:, LANES:], 0.0)
                                + jnp.where(eye_h, decay, 0.0))


def _rwkv_scan(r, lw, k, v, kk, a, groups=2):
    b, t, d = r.shape
    gr = 2 * CHUNK
    st = 2 * gr
    rows = gr * groups
    n_hp = d // LANES
    steps_per_seq = t // rows
    n_blocks = b * n_hp * steps_per_seq

    def block_index(blk):
        seq = blk // steps_per_seq
        return seq // n_hp, blk % steps_per_seq, seq % n_hp

    in_blk = pl.BlockSpec((1, rows, LANES), lambda s: block_index(jnp.minimum(s, n_blocks - 1)))
    out_blk = pl.BlockSpec((1, rows, LANES), lambda s: block_index(jnp.maximum(s - 1, 0)))
    idx = jnp.arange(st)
    xor = idx[:, None] ^ idx[None, :]
    same = (idx[:, None] // CHUNK) == (idx[None, :] // CHUNK)
    lower = idx[None, :] < idx[:, None]
    msb = jnp.floor(jnp.log2(jnp.maximum(xor, 1).astype(F32))).astype(jnp.int32)
    lev = jnp.where(same & lower, msb, -1).astype(jnp.int32)
    ti = jnp.arange(gr)
    tri = ((ti[:, None] >= ti[None, :]) & ((ti[:, None] // CHUNK) == (ti[None, :] // CHUNK))).astype(BF16)
    return pl.pallas_call(
        functools.partial(_rwkv_scan_kernel, groups=groups, steps_per_seq=steps_per_seq),
        grid=(n_blocks + 1,),
        in_specs=[in_blk] * 6 + [_const_spec(lev.shape), _const_spec(tri.shape)],
        out_specs=out_blk,
        out_shape=jax.ShapeDtypeStruct((b, t, d), F32),
        scratch_shapes=[pltpu.VMEM((LANES, LANES), F32),
                        pltpu.VMEM((rows, LANES), F32), pltpu.VMEM((rows, LANES), F32),
                        pltpu.VMEM((2 * groups, LANES, LANES), F32),
                        pltpu.VMEM((2 * groups, LANES, LANES), F32)],
        compiler_params=_params("arbitrary"),
        name="rwkv7_chunk_scan",
    )(r, lw, k, v, kk, a, lev, tri)


def _rwkv_out_kernel(y_ref, r_ref, k_ref, v_ref, g_ref, x_ref, mod_ref, vec_ref, bd_ref, w_ref,
                     lng_ref, lnb_ref, o_ref, act_ref, *, d):
    bd = bd_ref[...]
    vec = vec_ref[...]
    inv = 1.0 / HEAD_DIM
    for j in range(d // SUM_W):
        sl = slice(j * SUM_W, (j + 1) * SUM_W)
        y = y_ref[0, :, sl]
        mean = _dot_exact_rhs(y, bd) * inv
        yc = y - mean
        var = _head_sums(yc * yc, bd) * inv
        yn = yc * lax.rsqrt(var + GN_EPS) * vec[0:1, sl] + vec[1:2, sl]
        r = r_ref[0, :, sl].astype(F32)
        k = k_ref[0, :, sl].astype(F32)
        bonus = _dot_exact_rhs(r * k * vec[2:3, sl], bd) * v_ref[0, :, sl].astype(F32)
        act_ref[:, sl] = ((yn + bonus) * g_ref[0, :, sl].astype(F32)).astype(BF16)
    out = jnp.dot(act_ref[...], w_ref[...], preferred_element_type=F32)
    z = DEEPNORM_ALPHA * x_ref[0] + mod_ref[0][2:3] * out
    o_ref[0] = _layer_norm(z, lng_ref[...], lnb_ref[...])


def _rwkv_out(y, r, k, v, g, x, mod, vec, w, ln_g, ln_b, tm=512):
    b, t, d = x.shape
    bd = _head_sum_matrix()
    row_spec = pl.BlockSpec((1, tm, d), lambda i, j: (i, j, 0))
    return pl.pallas_call(
        functools.partial(_rwkv_out_kernel, d=d),
        grid=(b, t // tm),
        in_specs=[row_spec] * 6
        + [pl.BlockSpec((1, 6, d), lambda i, j: (i, 0, 0)),
           _const_spec(vec.shape), _const_spec(bd.shape), _const_spec(w.shape),
           _const_spec((1, d)), _const_spec((1, d))],
        out_specs=row_spec,
        out_shape=jax.ShapeDtypeStruct((b, t, d), F32),
        scratch_shapes=[pltpu.VMEM((tm, d), BF16)],
        compiler_params=_params("parallel", "parallel"),
        name="rwkv_out_proj_ln",
    )(y, r, k, v, g, x, mod, vec, bd, w, ln_g.reshape(1, d), ln_b.reshape(1, d))


def _trunk(x, c, ada_w, ada_b, ln_g, ln_b, ffn_w_in, ffn_w_out, fox_w_in, fox_b_f, fox_q_g, fox_k_g, fox_w_o, rwkv_mu, rwkv_w_rkv, rwkv_w0, rwkv_w1, rwkv_w2, rwkv_a0, rwkv_a1, rwkv_a2, rwkv_g1, rwkv_g2, rwkv_k_k, rwkv_k_a, rwkv_r_k, rwkv_lnx_g, rwkv_lnx_b, rwkv_w_o,
           *, tm, tm_rwkv, tq, tk, groups, tn, nc, fc):
    b, t, d = x.shape
    n_heads = d // HEAD_DIM
    mods = _mods(c, ada_w, ada_b, tn=tn)

    w_in = fox_w_in[0]
    f_lo = 3 * d
    w_qko = jnp.concatenate([w_in[:, :2 * d], w_in[:, f_lo + n_heads:]], axis=1).astype(BF16)
    w_vt = w_in[:, 2 * d:f_lo].T.astype(BF16)
    w_f = jnp.pad(w_in[:, f_lo:f_lo + n_heads], ((0, 0), (0, LANES - n_heads))).astype(BF16)
    b_f = jnp.pad(fox_b_f[0], (0, LANES - n_heads)).reshape(1, LANES)
    q, k, og, v, logit_bound = _fox_in(x, mods[0], w_qko, w_vt, w_f, b_f, fox_q_g[0], fox_k_g[0],
                                       tm=tm, nc=nc)
    att = _fox_attention(q, k, v, og, logit_bound, tq=tq, tk=tk)
    x = _proj_ln(att, x, mods[0], fox_w_o[0].astype(BF16), ln_g[0, 0], ln_b[0, 0], tm=tm)
    x = _ffn(x, mods[0], ffn_w_in[0].astype(BF16), ffn_w_out[0].astype(BF16), ln_g[0, 1], ln_b[0, 1],
             tm=tm, fc=fc)

    vec_in = jnp.stack([rwkv_w0[0], rwkv_a0[0], rwkv_k_k[0], rwkv_k_a[0]])
    r, lw, k, v, kk, a, g = _rwkv_in(
        x, mods[1], rwkv_mu[0], rwkv_w_rkv[0].astype(BF16),
        rwkv_w1[0].astype(BF16), rwkv_w2[0].astype(BF16),
        rwkv_a1[0].astype(BF16), rwkv_a2[0].astype(BF16),
        rwkv_g1[0].astype(BF16), rwkv_g2[0].astype(BF16), vec_in, tm=tm_rwkv)
    y = _rwkv_scan(r, lw, k, v, kk, a, groups=groups)
    vec_out = jnp.stack([rwkv_lnx_g[0], rwkv_lnx_b[0], rwkv_r_k[0].reshape(d)])
    x = _rwkv_out(y, r, k, v, g, x, mods[1], vec_out, rwkv_w_o[0].astype(BF16), ln_g[1, 0], ln_b[1, 0],
                  tm=tm)
    x = _ffn(x, mods[1], ffn_w_in[1].astype(BF16), ffn_w_out[1].astype(BF16), ln_g[1, 1], ln_b[1, 1],
             tm=tm, fc=fc)
    return x


def kernel(x, c, ada_w, ada_b, ln_g, ln_b, ffn_w_in, ffn_w_out, fox_w_in, fox_b_f, fox_q_g, fox_k_g, fox_w_o, rwkv_mu, rwkv_w_rkv, rwkv_w0, rwkv_w1, rwkv_w2, rwkv_a0, rwkv_a1, rwkv_a2, rwkv_g1, rwkv_g2, rwkv_k_k, rwkv_k_a, rwkv_r_k, rwkv_lnx_g, rwkv_lnx_b, rwkv_w_o):
    return _trunk(x, c, ada_w, ada_b, ln_g, ln_b, ffn_w_in, ffn_w_out, fox_w_in, fox_b_f, fox_q_g, fox_k_g, fox_w_o, rwkv_mu, rwkv_w_rkv, rwkv_w0, rwkv_w1, rwkv_w2, rwkv_a0, rwkv_a1, rwkv_a2, rwkv_g1, rwkv_g2, rwkv_k_k, rwkv_k_a, rwkv_r_k, rwkv_lnx_g, rwkv_lnx_b, rwkv_w_o,
                  tm=512, tm_rwkv=512, tq=512, tk=256, groups=8, tn=1536, nc=512, fc=256)
```

```python
import functools

import jax
import jax.numpy as jnp
import numpy as np
from jax import lax
from jax.experimental import pallas as pl
from jax.experimental.pallas import tpu as pltpu

F32 = jnp.float32
BF16 = jnp.bfloat16

HEAD_DIM = 64
DEPTH = 2
DEEPNORM_ALPHA = (2 * DEPTH) ** 0.25
LN_EPS = 1e-5
QK_EPS = 1e-6
GN_EPS = HEAD_DIM * 1e-5
LANES = 128
SUM_W = 256
CHUNK = 64
NEG_BIG = -1e30
LOG2E = 1.4426950408889634
DECAY_SCALE = -0.6065306597126334
SAFE_LOGIT_RANGE = 96.0
N_SPLIT = 3
AUG_STRIDE = 8
VMEM_LIMIT = 56 * 1024 * 1024


def _dot(a, b):
    return jnp.dot(a.astype(BF16), b.astype(BF16), preferred_element_type=F32)


def _dot_nt(a, b):
    return lax.dot_general(a.astype(BF16), b.astype(BF16), (((1,), (1,)), ((), ())),
                           preferred_element_type=F32)


def _dot_tn(a, b):
    return lax.dot_general(a.astype(BF16), b.astype(BF16), (((0,), (0,)), ((), ())),
                           preferred_element_type=F32)


def _split2(a):
    hi = a.astype(BF16)
    lo = (a - hi.astype(F32)).astype(BF16)
    return hi, lo


def _split3(a):
    hi = a.astype(BF16)
    r1 = a - hi.astype(F32)
    mid = r1.astype(BF16)
    lo = (r1 - mid.astype(F32)).astype(BF16)
    return hi, mid, lo


def _dot3(a, b):
    ah, al = _split2(a)
    bh, bl = _split2(b)
    d = functools.partial(jnp.dot, preferred_element_type=F32)
    return d(ah, bh) + (d(al, bh) + d(ah, bl))


def _dot_exact_rhs(a, b_exact):
    ah, al = _split2(a)
    d = functools.partial(jnp.dot, preferred_element_type=F32)
    return d(ah, b_exact) + d(al, b_exact)


def _head_sum_matrix():
    idx = jnp.arange(SUM_W) // HEAD_DIM
    return (idx[:, None] == idx[None, :]).astype(BF16)


def _layer_norm(z, g, b):
    mu = jnp.mean(z, axis=-1, keepdims=True)
    zc = z - mu
    var = jnp.mean(zc * zc, axis=-1, keepdims=True)
    return zc * lax.rsqrt(var + LN_EPS) * g + b


def _head_sums(a, head_sum_matrix):
    return jnp.dot(a.astype(BF16), head_sum_matrix, preferred_element_type=F32)


def _params(*sem):
    return pltpu.CompilerParams(dimension_semantics=sem, vmem_limit_bytes=VMEM_LIMIT)


def _const_spec(shape):
    nd = len(shape)
    return pl.BlockSpec(shape, lambda *_: (0,) * nd, pipeline_mode=pl.Buffered(1))


def _mods_kernel(c_ref, w_ref, b_ref, o_ref):
    c = c_ref[...]
    ca = c * jax.nn.sigmoid(c)
    o_ref[0] = _dot3(ca, w_ref[0]) + b_ref[0]


def _mods(c, ada_w, ada_b, tn=1536):
    depth, d, n = ada_w.shape
    b = c.shape[0]
    rows = 8
    cp = jnp.pad(c, ((0, rows - b), (0, 0)))
    out = pl.pallas_call(
        _mods_kernel,
        grid=(depth, n // tn),
        in_specs=[pl.BlockSpec((rows, d), lambda l, j: (0, 0)),
                  pl.BlockSpec((1, d, tn), lambda l, j: (l, 0, j)),
                  pl.BlockSpec((1, 1, tn), lambda l, j: (l, 0, j))],
        out_specs=pl.BlockSpec((1, rows, tn), lambda l, j: (l, 0, j)),
        out_shape=jax.ShapeDtypeStruct((depth, rows, n), F32),
        compiler_params=_params("parallel", "parallel"),
        name="adaln_mods",
    )(cp, ada_w, ada_b.reshape(depth, 1, n))
    return out[:, :b].reshape(depth, b, 6, d)


def _fox_in_kernel(x_ref, mod_ref, w_ref, wf_ref, bf_ref, qg_ref, kg_ref, bd_ref, tri_ref,
                   pq_ref, pk_ref, oq_ref, ok_ref, wvt_ref,
                   q_ref, k_ref, o_ref, vt_ref, carry_ref, *, d, nc):
    t = pl.program_id(1)
    x = x_ref[0]
    mod = mod_ref[0]
    h = (x * (1.0 + mod[1:2]) + mod[0:1]).astype(BF16)
    bd = bd_ref[...]
    qg = qg_ref[...]
    kg = kg_ref[...]
    dd = functools.partial(jnp.dot, preferred_element_type=F32)

    fl = dd(h, wf_ref[...]) + bf_ref[...]
    lf = jnp.minimum(fl, 0.0) - jnp.log(1.0 + jnp.exp(-jnp.abs(fl)))
    tri = tri_ref[...]
    hi, mid, lo = _split3(lf)
    cs = dd(tri, hi) + (dd(tri, mid) + dd(tri, lo))

    @pl.when(t == 0)
    def _():
        carry_ref[...] = jnp.zeros_like(carry_ref)

    f = cs + carry_ref[0:1, :]
    tm = f.shape[0]
    carry_ref[...] = jnp.broadcast_to(f[tm - 1:tm, :], carry_ref.shape)
    n_heads = d // HEAD_DIM
    lane = lax.broadcasted_iota(jnp.int32, (tm, LANES), 1)
    parts = [jnp.where(lane < n_heads, p.astype(F32), 0.0) for p in _split3(f * LOG2E)]
    fpack = parts[0]
    for i in range(1, N_SPLIT):
        fpack = fpack + pltpu.roll(parts[i], i * n_heads, axis=1)
    fpack = fpack.astype(BF16)
    low = lane < HEAD_DIM

    def head_rms(zp, g):
        ss = _head_sums(zp * zp, bd)
        return zp * lax.rsqrt(ss * (1.0 / HEAD_DIM) + QK_EPS) * g

    aug_all = {0: dd(fpack, pq_ref[...]) + oq_ref[...], 1: dd(fpack, pk_ref[...]) + ok_ref[...]}
    aug_even = (lane >= HEAD_DIM) & (lane < HEAD_DIM + AUG_STRIDE)
    aug_odd = lane < AUG_STRIDE

    def store_augmented(out_ref, zj, tile, aug):
        even = pltpu.roll(aug, (HEAD_DIM - AUG_STRIDE * 2 * tile) % LANES, axis=1)
        odd = pltpu.roll(aug, (-AUG_STRIDE * (2 * tile + 1)) % LANES, axis=1)
        sl = slice(2 * tile * LANES, (2 * tile + 2) * LANES)
        out_ref[0, :, sl] = jnp.concatenate(
            [jnp.where(low, zj, jnp.where(aug_even, even, 0.0)),
             jnp.where(low, jnp.where(aug_odd, odd, 0.0), zj)], axis=1).astype(out_ref.dtype)

    for sec, out_ref in enumerate((q_ref, k_ref, o_ref)):
        for c in range(d // nc):
            col = sec * d + c * nc
            z = dd(h, w_ref[:, col:col + nc])
            if sec == 2:
                out_ref[0, :, c * nc:(c + 1) * nc] = jax.nn.sigmoid(z).astype(out_ref.dtype)
                continue
            for pr in range(nc // SUM_W):
                zn = head_rms(z[:, pr * SUM_W:(pr + 1) * SUM_W], qg if sec == 0 else kg)
                for j in range(SUM_W // LANES):
                    tile = (c * nc + pr * SUM_W) // LANES + j
                    store_augmented(out_ref, zn[:, j * LANES:(j + 1) * LANES], tile, aug_all[sec])
    for c in range(d // nc):
        vt_ref[0, c * nc:(c + 1) * nc, :] = lax.dot_general(
            wvt_ref[c * nc:(c + 1) * nc, :], h, (((1,), (1,)), ((), ())),
            preferred_element_type=F32).astype(vt_ref.dtype)


def _fox_in(x, mod, w_qko, w_vt, w_f, b_f, q_g, k_g, tm=512, nc=512):
    b, t, d = x.shape
    n_heads = d // HEAD_DIM
    assert N_SPLIT * n_heads <= LANES
    qg = jnp.tile(q_g * (HEAD_DIM ** -0.5 * LOG2E), SUM_W // HEAD_DIM).reshape(1, SUM_W)
    kg = jnp.tile(k_g, SUM_W // HEAD_DIM).reshape(1, SUM_W)
    logit_bound = HEAD_DIM * jnp.max(jnp.abs(qg)) * jnp.max(jnp.abs(kg))
    bd = _head_sum_matrix()
    tri = (jnp.arange(tm)[:, None] >= jnp.arange(tm)[None, :]).astype(BF16)
    assert AUG_STRIDE * n_heads <= LANES and 2 * N_SPLIT + 1 <= AUG_STRIDE
    pq = np.zeros((LANES, LANES), np.float32)
    pk = np.zeros_like(pq)
    oq = np.zeros((1, LANES), np.float32)
    ok = np.zeros_like(oq)
    shift_lane = np.zeros_like(oq)
    for hd in range(n_heads):
        base = hd * AUG_STRIDE
        for part in range(N_SPLIT):
            pq[part * n_heads + hd, base + N_SPLIT + part] = 1.0
            ok[0, base + N_SPLIT + part] = 1.0
            pk[part * n_heads + hd, base + part] = -1.0
            oq[0, base + part] = 1.0
        shift_lane[0, base + 2 * N_SPLIT] = 1.0
    pq = jnp.asarray(pq, BF16)
    pk = jnp.asarray(pk, BF16)
    oq = jnp.asarray(oq) - logit_bound * jnp.asarray(shift_lane)
    ok = jnp.asarray(ok + shift_lane)
    act = jax.ShapeDtypeStruct((b, t, d), BF16)
    aug = jax.ShapeDtypeStruct((b, t, n_heads * LANES), BF16)
    row_spec = pl.BlockSpec((1, tm, d), lambda i, j: (i, j, 0))
    aug_spec = pl.BlockSpec((1, tm, n_heads * LANES), lambda i, j: (i, j, 0))
    kern = functools.partial(_fox_in_kernel, d=d, nc=nc)
    consts = (w_qko, w_f, b_f, qg, kg, bd, tri, pq, pk, oq, ok, w_vt)
    q_aug, k_aug, og, vt = pl.pallas_call(
        kern,
        grid=(b, t // tm),
        in_specs=[row_spec, pl.BlockSpec((1, 6, d), lambda i, j: (i, 0, 0))]
        + [_const_spec(c.shape) for c in consts],
        out_specs=[aug_spec, aug_spec, row_spec, pl.BlockSpec((1, d, tm), lambda i, j: (i, 0, j))],
        out_shape=[aug, aug, act, jax.ShapeDtypeStruct((b, d, t), BF16)],
        scratch_shapes=[pltpu.VMEM((8, LANES), F32)],
        compiler_params=_params("parallel", "arbitrary"),
        name="fox_in_proj",
    )(x, mod, *consts)
    return q_aug, k_aug, og, vt, logit_bound


def _attn_kernel(q_ref, k_ref, vt_ref, g_ref, y_ref, *, tq, tk):
    qi = pl.program_id(2)
    heads = LANES // HEAD_DIM
    pair = 2 * tk
    key = lax.broadcasted_iota(jnp.int32, (tk, tq), 0)
    qry = lax.broadcasted_iota(jnp.int32, (tk, tq), 1) + qi * tq
    qs = [q_ref[0, :, hh * LANES:(hh + 1) * LANES] for hh in range(heads)]

    def step(j, carry, masked):
        starts = [pl.multiple_of(j * pair + bb * tk, tk) for bb in range(2)]

        def scores(hh, bb):
            z = _dot_nt(k_ref[0, pl.ds(starts[bb], tk), hh * LANES:(hh + 1) * LANES], qs[hh])
            if masked:
                z = jnp.where(key + starts[bb] <= qry, z, NEG_BIG)
            return z

        def weights(z, m):
            p = jnp.exp2(z - m)
            return p, jnp.sum(p, axis=0, keepdims=True)

        def values(hh, bb, p):
            vt = vt_ref[0, hh * HEAD_DIM:(hh + 1) * HEAD_DIM, pl.ds(starts[bb], tk)]
            return jnp.dot(vt, p.astype(BF16), preferred_element_type=F32)

        colmax = lambda z: jnp.max(z, axis=0, keepdims=True)
        (m0, l0, a0), (m1, l1, a1) = carry
        z00 = scores(0, 0)
        z10 = scores(1, 0)
        m0a = jnp.maximum(m0, colmax(z00))
        z01 = scores(0, 1)
        p00, s00 = weights(z00, m0a)
        m1a = jnp.maximum(m1, colmax(z10))
        z11 = scores(1, 1)
        v00 = values(0, 0, p00)
        p10, s10 = weights(z10, m1a)
        m0b = jnp.maximum(m0a, colmax(z01))
        v10 = values(1, 0, p10)
        p01, s01 = weights(z01, m0b)
        m1b = jnp.maximum(m1a, colmax(z11))
        v01 = values(0, 1, p01)
        p11, s11 = weights(z11, m1b)
        v11 = values(1, 1, p11)

        def merge(m, l, a, ma, mb, sa, sb, va, vb):
            ra = jnp.exp2(m - ma)
            rb = jnp.exp2(ma - mb)
            return mb, rb * (ra * l + sa) + sb, rb * (ra * a + va) + vb

        return (merge(m0, l0, a0, m0a, m0b, s00, s01, v00, v01),
                merge(m1, l1, a1, m1a, m1b, s10, s11, v10, v11))

    init = tuple((jnp.full((1, tq), NEG_BIG, F32), jnp.zeros((1, tq), F32),
                  jnp.zeros((HEAD_DIM, tq), F32)) for _ in range(heads))
    n_full = (qi * tq) // pair
    carry = lax.fori_loop(0, n_full, functools.partial(step, masked=False), init)
    for jm in range(tq // pair):
        carry = step(n_full + jm, carry, True)
    yt = jnp.concatenate([carry[hh][2] / carry[hh][1] for hh in range(heads)], axis=0)
    y_ref[0] = (yt.T * g_ref[0].astype(F32)).astype(y_ref.dtype)


def _attn_bounded_kernel(q_ref, k_ref, vt_ref, g_ref, y_ref, *, tq, q_parts=2, step_blocks=(2, 1)):
    qi = pl.program_id(2)
    heads = LANES // HEAD_DIM
    part = tq // q_parts
    slots = [(hh, pi) for hh in range(heads) for pi in range(q_parts)]

    def run_chains(chains, carry):
        def scores(hh, k0, nk, pi, q0, nq, masked):
            row = pi * part + q0
            z = _dot_nt(k_ref[0, pl.ds(k0, nk), hh * LANES:(hh + 1) * LANES],
                        q_ref[0, row:row + nq, hh * LANES:(hh + 1) * LANES])
            if masked:
                key = lax.broadcasted_iota(jnp.int32, (nk, nq), 0) + k0
                qry = lax.broadcasted_iota(jnp.int32, (nk, nq), 1) + (qi * tq + row)
                z = jnp.where(key <= qry, z, NEG_BIG)
            return z

        sums = {s: [] for s in slots}
        vals = {s: [] for s in slots}
        z_next = scores(*chains[0])
        for c, (hh, k0, nk, pi, q0, nq, _) in enumerate(chains):
            z = z_next
            if c + 1 < len(chains):
                z_next = scores(*chains[c + 1])
            p = jnp.exp2(z)
            vt = vt_ref[0, hh * HEAD_DIM:(hh + 1) * HEAD_DIM, pl.ds(k0, nk)]
            s = jnp.sum(p, axis=0, keepdims=True)
            v = jnp.dot(vt, p.astype(BF16), preferred_element_type=F32)
            if nq != part:
                s = jnp.concatenate([jnp.zeros((1, q0), F32), s], axis=1)
                v = jnp.concatenate([jnp.zeros((HEAD_DIM, q0), F32), v], axis=1)
            sums[(hh, pi)].append(s)
            vals[(hh, pi)].append(v)
        out = []
        for n, s in enumerate(slots):
            l, acc = carry[n]
            if sums[s]:
                l = l + functools.reduce(jnp.add, sums[s])
                acc = acc + functools.reduce(jnp.add, vals[s])
            out.append((l, acc))
        return tuple(out)

    def full_step(j, carry, nb, base):
        starts = [pl.multiple_of(base + j * (nb * part) + bb * part, part) for bb in range(nb)]
        return run_chains([(hh, starts[bb], part, pi, 0, part, False)
                           for bb in range(nb) for (hh, pi) in slots], carry)

    carry = tuple((jnp.zeros((1, part), F32), jnp.zeros((HEAD_DIM, part), F32)) for _ in slots)
    done = 0
    for nb in step_blocks:
        n_steps = (qi * tq - done) // (nb * part)
        carry = lax.fori_loop(0, n_steps, functools.partial(full_step, nb=nb, base=done), carry)
        done = done + n_steps * (nb * part)
    half = part // 2
    chains = []
    for kb in range(q_parts):
        k0 = pl.multiple_of(qi * tq + kb * part, part)
        for hh in range(heads):
            chains.append((hh, k0, half, kb, 0, part, True))
            chains.append((hh, pl.multiple_of(k0 + half, half), half, kb, half, half, True))
            chains += [(hh, k0, part, pi, 0, part, False) for pi in range(kb + 1, q_parts)]
    carry = run_chains(chains, carry)
    yt = jnp.concatenate(
        [jnp.concatenate([carry[slots.index((hh, pi))][1] / carry[slots.index((hh, pi))][0]
                          for hh in range(heads)], axis=0) for pi in range(q_parts)], axis=1)
    y_ref[0] = (yt.T * g_ref[0].astype(F32)).astype(y_ref.dtype)


def _fox_attention(q_aug, k_aug, vt, og, logit_bound, tq=1024):
    return lax.cond(
        2.0 * logit_bound <= SAFE_LOGIT_RANGE,
        functools.partial(_attention_call, tq=tq, name="fox_attention_bounded",
                          kernel_fn=functools.partial(_attn_bounded_kernel, tq=tq)),
        functools.partial(_attention_call, tq=tq // 2, name="fox_attention",
                          kernel_fn=functools.partial(_attn_kernel, tq=tq // 2, tk=tq // 4)),
        q_aug, k_aug, vt, og)


def _attention_call(q_aug, k_aug, vt, og, *, kernel_fn, tq, name):
    b, d, t = vt.shape
    heads = LANES // HEAD_DIM
    blk = pl.BlockSpec((1, tq, LANES), lambda i, p, j: (i, j, p))
    q_blk = pl.BlockSpec((1, tq, heads * LANES), lambda i, p, j: (i, j, p))
    k_full = pl.BlockSpec((1, t, heads * LANES), lambda i, p, j: (i, 0, p))
    vt_full = pl.BlockSpec((1, LANES, t), lambda i, p, j: (i, p, 0))
    return pl.pallas_call(
        kernel_fn,
        grid=(b, d // LANES, t // tq),
        in_specs=[q_blk, k_full, vt_full, blk],
        out_specs=blk,
        out_shape=jax.ShapeDtypeStruct((b, t, d), BF16),
        compiler_params=_params("parallel", "parallel", "arbitrary"),
        name=name,
    )(q_aug, k_aug, vt, og)


def _proj_ln_kernel(a_ref, x_ref, mod_ref, w_ref, lng_ref, lnb_ref, o_ref):
    y = jnp.dot(a_ref[0], w_ref[...], preferred_element_type=F32)
    z = DEEPNORM_ALPHA * x_ref[0] + mod_ref[0][2:3] * y
    o_ref[0] = _layer_norm(z, lng_ref[...], lnb_ref[...])


def _proj_ln(a, x, mod, w, ln_g, ln_b, tm=512):
    b, t, d = x.shape
    row_spec = pl.BlockSpec((1, tm, d), lambda i, j: (i, j, 0))
    return pl.pallas_call(
        _proj_ln_kernel,
        grid=(b, t // tm),
        in_specs=[row_spec, row_spec,
                  pl.BlockSpec((1, 6, d), lambda i, j: (i, 0, 0)),
                  _const_spec(w.shape), _const_spec((1, d)), _const_spec((1, d))],
        out_specs=row_spec,
        out_shape=jax.ShapeDtypeStruct((b, t, d), F32),
        compiler_params=_params("parallel", "parallel"),
        name="attn_out_proj_ln",
    )(a, x, mod, w, ln_g.reshape(1, d), ln_b.reshape(1, d))


def _ffn_kernel(x_ref, mod_ref, win_ref, wout_ref, lng_ref, lnb_ref, o_ref, act_ref, *, d_ff, fc):
    x = x_ref[0]
    mod = mod_ref[0]
    h = (x * (1.0 + mod[4:5]) + mod[3:4]).astype(BF16)
    for c in range(d_ff // fc):
        g = jnp.dot(h, win_ref[:, c * fc:(c + 1) * fc], preferred_element_type=F32)
        u = jnp.dot(h, win_ref[:, d_ff + c * fc:d_ff + (c + 1) * fc], preferred_element_type=F32)
        act_ref[:, c * fc:(c + 1) * fc] = (g * jax.nn.sigmoid(g) * u).astype(BF16)
    y = jnp.dot(act_ref[...], wout_ref[...], preferred_element_type=F32)
    z = DEEPNORM_ALPHA * x + mod[5:6] * y
    o_ref[0] = _layer_norm(z, lng_ref[...], lnb_ref[...])


def _ffn(x, mod, w_in, w_out, ln_g, ln_b, tm=512, fc=256):
    b, t, d = x.shape
    d_ff = w_out.shape[0]
    row_spec = pl.BlockSpec((1, tm, d), lambda i, j: (i, j, 0))
    return pl.pallas_call(
        functools.partial(_ffn_kernel, d_ff=d_ff, fc=fc),
        grid=(b, t // tm),
        in_specs=[row_spec,
                  pl.BlockSpec((1, 6, d), lambda i, j: (i, 0, 0)),
                  _const_spec(w_in.shape), _const_spec(w_out.shape),
                  _const_spec((1, d)), _const_spec((1, d))],
        out_specs=row_spec,
        out_shape=jax.ShapeDtypeStruct((b, t, d), F32),
        scratch_shapes=[pltpu.VMEM((tm, d_ff), BF16)],
        compiler_params=_params("parallel", "parallel"),
        name="swiglu_ln",
    )(x, mod, w_in, w_out, ln_g.reshape(1, d), ln_b.reshape(1, d))


def _rwkv_in_kernel(x_ref, xp_ref, mod_ref, mu_ref, wrkv_ref, w1_ref, w2_ref, a1_ref, a2_ref,
                    g1_ref, g2_ref, vec_ref, bd_ref,
                    r_ref, lw_ref, k_ref, v_ref, kk_ref, a_ref, g_ref, *, d):
    t = pl.program_id(1)
    mod = mod_ref[0]
    sc = 1.0 + mod[1:2]
    sh = mod[0:1]
    h = x_ref[0] * sc + sh
    tm = h.shape[0]
    prev = xp_ref[0][7:8, :] * sc + sh
    prev = jnp.where(t == 0, jnp.zeros_like(prev), prev)
    rows = lax.broadcasted_iota(jnp.int32, h.shape, 0)
    hprev = jnp.where(rows == 0, prev, pltpu.roll(h, 1, axis=0))
    dx = hprev - h
    mu = mu_ref[...]
    vec = vec_ref[...]
    w0, a0, k_k, k_a = vec[0:1], vec[1:2], vec[2:3], vec[3:4]

    def mix(n):
        return (h + dx * mu[n:n + 1]).astype(BF16)

    dd = functools.partial(jnp.dot, preferred_element_type=F32)
    r = dd(mix(0), wrkv_ref[0])
    k = dd(mix(1), wrkv_ref[1])
    v = dd(mix(2), wrkv_ref[2])
    ww = w0 + dd(jnp.tanh(dd(mix(3), w1_ref[...])).astype(BF16), w2_ref[...])
    lw = DECAY_SCALE * jax.nn.sigmoid(ww)
    a = jax.nn.sigmoid(a0 + dd(dd(mix(4), a1_ref[...]).astype(BF16), a2_ref[...]))
    g = dd(jax.nn.sigmoid(dd(mix(5), g1_ref[...])).astype(BF16), g2_ref[...])
    kk = k * k_k
    k = k * (1.0 + (a - 1.0) * k_a)
    bd = bd_ref[...]
    for j in range(d // SUM_W):
        sl = slice(j * SUM_W, (j + 1) * SUM_W)
        kkj = kk[:, sl]
        ss = _head_sums(kkj * kkj, bd)
        kk_ref[0, :, sl] = (kkj * jnp.minimum(lax.rsqrt(ss), 1e12)).astype(kk_ref.dtype)
    r_ref[0] = r.astype(r_ref.dtype)
    lw_ref[0] = lw
    k_ref[0] = k.astype(k_ref.dtype)
    v_ref[0] = v.astype(v_ref.dtype)
    a_ref[0] = a.astype(a_ref.dtype)
    g_ref[0] = g.astype(g_ref.dtype)


def _rwkv_in(x, mod, mu, w_rkv, w1, w2, a1, a2, g1, g2, vec, tm=512):
    b, t, d = x.shape
    bd = _head_sum_matrix()
    row_spec = pl.BlockSpec((1, tm, d), lambda i, j: (i, j, 0))
    prev_spec = pl.BlockSpec((1, 8, d), lambda i, j: (i, jnp.maximum(j * (tm // 8) - 1, 0), 0))
    act = jax.ShapeDtypeStruct((b, t, d), BF16)
    consts = (mu, w_rkv, w1, w2, a1, a2, g1, g2, vec, bd)
    return pl.pallas_call(
        functools.partial(_rwkv_in_kernel, d=d),
        grid=(b, t // tm),
        in_specs=[row_spec, prev_spec, pl.BlockSpec((1, 6, d), lambda i, j: (i, 0, 0))]
        + [_const_spec(c.shape) for c in consts],
        out_specs=[row_spec] * 7,
        out_shape=[act, jax.ShapeDtypeStruct((b, t, d), F32), act, act, act, act, act],
        compiler_params=_params("parallel", "parallel"),
        name="rwkv_in_proj",
    )(x, x, mod, *consts)


def _rwkv_scan_kernel(r_ref, lw_ref, k_ref, v_ref, kk_ref, a_ref, lev_ref, tri_ref,
                      y_ref, h_ref, yi_ref, rq_ref, m_ref, n_ref, *, groups, steps_per_seq):
    s = pl.program_id(0)
    gr = 2 * CHUNK
    st = 2 * gr
    n_lev = CHUNK.bit_length() - 1

    @pl.when(s == 0)
    def _():
        h_ref[...] = jnp.zeros_like(h_ref)
        yi_ref[...] = jnp.zeros_like(yi_ref)
        rq_ref[...] = jnp.zeros_like(rq_ref)
        m_ref[...] = jnp.zeros_like(m_ref)
        n_ref[...] = jnp.zeros_like(n_ref)

    first_of_seq = lax.rem(jnp.maximum(s - 1, 0), steps_per_seq) == 0
    state = [jnp.where(first_of_seq, 0.0, h_ref[...])]
    pending = list(range(2 * groups))

    def state_steps(count):
        for _ in range(count):
            if pending:
                i = pending.pop(0)
                lo = i * CHUNK
                h = state[0]
                y_ref[0, lo:lo + CHUNK, :] = (yi_ref[lo:lo + CHUNK, :]
                                              + _dot(rq_ref[lo:lo + CHUNK, :], h))
                state[0] = _dot3(m_ref[i], h) + n_ref[i]

    lane = lax.broadcasted_iota(jnp.int32, (gr, LANES), 1)
    head0 = lane < HEAD_DIM
    lev = lev_ref[...]
    tri = tri_ref[...]
    strict = lev >= 0
    srow = lax.broadcasted_iota(jnp.int32, (st, st), 0)
    scol = lax.broadcasted_iota(jnp.int32, (st, st), 1)
    eye_st = srow == scol
    incl = strict | eye_st
    hrow = lax.broadcasted_iota(jnp.int32, (LANES, LANES), 0)
    hcol = lax.broadcasted_iota(jnp.int32, (LANES, LANES), 1)
    same_head = (hrow < HEAD_DIM) == (hcol < HEAD_DIM)
    eye_h = hrow == hcol

    def stack(x):
        zero = jnp.zeros_like(x)
        return jnp.concatenate([jnp.where(head0, x, zero), jnp.where(head0, zero, x)], axis=0)

    def unstack(x):
        return x[:gr] + x[gr:]

    def log_decay_cumsum(g):
        hi, mid, lo = _split3(lw_ref[0, g * gr:(g + 1) * gr, :])
        dd = functools.partial(jnp.dot, preferred_element_type=F32)
        return dd(tri, hi) + (dd(tri, mid) + dd(tri, lo))

    def prep(g, cum):
        rs = slice(g * gr, (g + 1) * gr)
        lw = lw_ref[0, rs, :]
        clast = jnp.concatenate(
            [jnp.broadcast_to(cum[(j + 1) * CHUNK - 1:(j + 1) * CHUNK], (CHUNK, LANES))
             for j in range(2)], axis=0)
        g_inv = jnp.exp(-cum)
        g_rem = jnp.exp(clast - cum)
        r = r_ref[0, rs, :].astype(F32)
        k = k_ref[0, rs, :].astype(F32)
        v = v_ref[0, rs, :].astype(F32)
        kk = kk_ref[0, rs, :].astype(F32)
        bt = kk * a_ref[0, rs, :].astype(F32)
        a_st = stack(-kk * jnp.exp(cum - lw))
        r_st = stack(r * jnp.exp(cum))
        b_st = stack(bt * g_inv)
        k_st = stack(k * g_inv)
        prods = []
        for hd in range(2):
            hs = slice(hd * gr, (hd + 1) * gr)
            prods.append(_dot_nt(jnp.concatenate([a_st[hs], r_st[hs]], axis=0),
                                 jnp.concatenate([b_st[hs], k_st[hs]], axis=0)))
        zero = jnp.zeros((gr, gr), F32)

        def by_head(rows, cols):
            return jnp.concatenate(
                [jnp.concatenate([prods[0][rows, cols], zero], axis=1),
                 jnp.concatenate([zero, prods[1][rows, cols]], axis=1)], axis=0)

        top, bot = slice(0, gr), slice(gr, st)
        return dict(
            v=v, v_st=stack(v), a_st=a_st, r_st=r_st, b_rem=bt * g_rem, k_rem=k * g_rem,
            g_last=jnp.exp(clast),
            a_ab=jnp.where(strict, by_head(top, top), 0.0),
            a_ak=jnp.where(strict, by_head(top, bot), 0.0),
            a_rb=jnp.where(incl, by_head(bot, top), 0.0),
            a_rk=jnp.where(incl, by_head(bot, bot), 0.0))

    per_stage = -(-2 * groups // n_lev)
    cums = [log_decay_cumsum(g) for g in range(groups)]
    ps = [prep(g, cums[g]) for g in range(groups)]
    gs = range(groups)
    state_steps(per_stage)

    def lower_rows(m, bsz):
        return jnp.concatenate([m[i + bsz:i + 2 * bsz] for i in range(0, st, 2 * bsz)], axis=0)

    def scatter_lower(full, low, bsz):
        parts = []
        for n, i in enumerate(range(0, st, 2 * bsz)):
            parts += [full[i:i + bsz], low[n * bsz:(n + 1) * bsz]]
        return jnp.concatenate(parts, axis=0)

    xs = [jnp.where(eye_st, 1.0, jnp.where(lev == 0, p["a_ab"], 0.0)) for p in ps]
    for level in range(1, n_lev):
        bsz = 1 << level
        a_off = [jnp.where(lev == level, ps[g]["a_ab"], 0.0) for g in gs]
        if bsz % 8:
            ws = [_dot(a_off[g], xs[g]) for g in gs]
            xs = [xs[g] + _dot(xs[g], ws[g]) for g in gs]
        else:
            w_low = [_dot(lower_rows(a_off[g], bsz), xs[g]) for g in gs]
            zero = jnp.zeros((st, st), F32)
            upd = [_dot(lower_rows(xs[g], bsz), scatter_lower(zero, w_low[g], bsz)) for g in gs]
            xs = [scatter_lower(xs[g], lower_rows(xs[g], bsz) + upd[g], bsz) for g in gs]
        state_steps(per_stage)

    gm = [jnp.concatenate([_dot(ps[g]["a_ak"], ps[g]["v_st"]), ps[g]["a_st"]], axis=1) for g in gs]
    state_steps(2 * groups)
    h_ref[...] = state[0]
    tg = [_dot(xs[g], gm[g]) for g in gs]
    rb = [_dot(ps[g]["a_rb"], tg[g]) for g in gs]
    rk = [_dot(ps[g]["a_rk"], ps[g]["v_st"]) for g in gs]
    for g in gs:
        p = ps[g]
        yi_ref[g * gr:(g + 1) * gr, :] = unstack(rb[g][:, :LANES] + rk[g])
        rq_ref[g * gr:(g + 1) * gr, :] = unstack(p["r_st"] + rb[g][:, LANES:])
        uv = unstack(tg[g][:, :LANES])
        wa = unstack(tg[g][:, LANES:])
        for j in range(2):
            cs = slice(j * CHUNK, (j + 1) * CHUNK)
            lhs_t = jnp.concatenate([p["b_rem"][cs], p["k_rem"][cs]], axis=0)
            rhs_t = jnp.concatenate(
                [jnp.concatenate([uv[cs], wa[cs]], axis=1),
                 jnp.concatenate([p["v"][cs], jnp.zeros_like(p["v"][cs])], axis=1)], axis=0)
            nm = _dot_tn(lhs_t, rhs_t)
            n_ref[2 * g + j] = jnp.where(same_head, nm[:, :LANES], 0.0)
            decay = jnp.broadcast_to(p["g_last"][j * CHUNK:j * CHUNK + 1], (LANES, LANES))
            m_ref[2 * g + j] = (jnp.where(same_head, nm[:, LANES:], 0.0)
                                + jnp.where(eye_h, decay, 0.0))


def _rwkv_scan(r, lw, k, v, kk, a, groups=2):
    b, t, d = r.shape
    gr = 2 * CHUNK
    st = 2 * gr
    rows = gr * groups
    n_hp = d // LANES
    steps_per_seq = t // rows
    n_blocks = b * n_hp * steps_per_seq

    def block_index(blk):
        seq = blk // steps_per_seq
        return seq // n_hp, blk % steps_per_seq, seq % n_hp

    in_blk = pl.BlockSpec((1, rows, LANES), lambda s: block_index(jnp.minimum(s, n_blocks - 1)))
    out_blk = pl.BlockSpec((1, rows, LANES), lambda s: block_index(jnp.maximum(s - 1, 0)))
    idx = jnp.arange(st)
    xor = idx[:, None] ^ idx[None, :]
    same = (idx[:, None] // CHUNK) == (idx[None, :] // CHUNK)
    lower = idx[None, :] < idx[:, None]
    msb = jnp.floor(jnp.log2(jnp.maximum(xor, 1).astype(F32))).astype(jnp.int32)
    lev = jnp.where(same & lower, msb, -1).astype(jnp.int32)
    ti = jnp.arange(gr)
    tri = ((ti[:, None] >= ti[None, :]) & ((ti[:, None] // CHUNK) == (ti[None, :] // CHUNK))).astype(BF16)
    return pl.pallas_call(
        functools.partial(_rwkv_scan_kernel, groups=groups, steps_per_seq=steps_per_seq),
        grid=(n_blocks + 1,),
        in_specs=[in_blk] * 6 + [_const_spec(lev.shape), _const_spec(tri.shape)],
        out_specs=out_blk,
        out_shape=jax.ShapeDtypeStruct((b, t, d), F32),
        scratch_shapes=[pltpu.VMEM((LANES, LANES), F32),
                        pltpu.VMEM((rows, LANES), F32), pltpu.VMEM((rows, LANES), F32),
                        pltpu.VMEM((2 * groups, LANES, LANES), F32),
                        pltpu.VMEM((2 * groups, LANES, LANES), F32)],
        compiler_params=_params("arbitrary"),
        name="rwkv7_chunk_scan",
    )(r, lw, k, v, kk, a, lev, tri)


def _rwkv_out_kernel(y_ref, r_ref, k_ref, v_ref, g_ref, x_ref, mod_ref, vec_ref, bd_ref, w_ref,
                     lng_ref, lnb_ref, o_ref, act_ref, *, d):
    bd = bd_ref[...]
    vec = vec_ref[...]
    inv = 1.0 / HEAD_DIM
    for j in range(d // SUM_W):
        sl = slice(j * SUM_W, (j + 1) * SUM_W)
        y = y_ref[0, :, sl]
        mean = _dot_exact_rhs(y, bd) * inv
        yc = y - mean
        var = _head_sums(yc * yc, bd) * inv
        yn = yc * lax.rsqrt(var + GN_EPS) * vec[0:1, sl] + vec[1:2, sl]
        r = r_ref[0, :, sl].astype(F32)
        k = k_ref[0, :, sl].astype(F32)
        bonus = _dot_exact_rhs(r * k * vec[2:3, sl], bd) * v_ref[0, :, sl].astype(F32)
        act_ref[:, sl] = ((yn + bonus) * g_ref[0, :, sl].astype(F32)).astype(BF16)
    out = jnp.dot(act_ref[...], w_ref[...], preferred_element_type=F32)
    z = DEEPNORM_ALPHA * x_ref[0] + mod_ref[0][2:3] * out
    o_ref[0] = _layer_norm(z, lng_ref[...], lnb_ref[...])


def _rwkv_out(y, r, k, v, g, x, mod, vec, w, ln_g, ln_b, tm=512):
    b, t, d = x.shape
    bd = _head_sum_matrix()
    row_spec = pl.BlockSpec((1, tm, d), lambda i, j: (i, j, 0))
    return pl.pallas_call(
        functools.partial(_rwkv_out_kernel, d=d),
        grid=(b, t // tm),
        in_specs=[row_spec] * 6
        + [pl.BlockSpec((1, 6, d), lambda i, j: (i, 0, 0)),
           _const_spec(vec.shape), _const_spec(bd.shape), _const_spec(w.shape),
           _const_spec((1, d)), _const_spec((1, d))],
        out_specs=row_spec,
        out_shape=jax.ShapeDtypeStruct((b, t, d), F32),
        scratch_shapes=[pltpu.VMEM((tm, d), BF16)],
        compiler_params=_params("parallel", "parallel"),
        name="rwkv_out_proj_ln",
    )(y, r, k, v, g, x, mod, vec, bd, w, ln_g.reshape(1, d), ln_b.reshape(1, d))


def _trunk(x, c, ada_w, ada_b, ln_g, ln_b, ffn_w_in, ffn_w_out, fox_w_in, fox_b_f, fox_q_g, fox_k_g, fox_w_o, rwkv_mu, rwkv_w_rkv, rwkv_w0, rwkv_w1, rwkv_w2, rwkv_a0, rwkv_a1, rwkv_a2, rwkv_g1, rwkv_g2, rwkv_k_k, rwkv_k_a, rwkv_r_k, rwkv_lnx_g, rwkv_lnx_b, rwkv_w_o,
           *, tm, tm_rwkv, tq, groups, tn, nc, fc):
    b, t, d = x.shape
    n_heads = d // HEAD_DIM
    mods = _mods(c, ada_w, ada_b, tn=tn)

    w_in = fox_w_in[0]
    f_lo = 3 * d
    w_qko = jnp.concatenate([w_in[:, :2 * d], w_in[:, f_lo + n_heads:]], axis=1).astype(BF16)
    w_vt = w_in[:, 2 * d:f_lo].T.astype(BF16)
    w_f = jnp.pad(w_in[:, f_lo:f_lo + n_heads], ((0, 0), (0, LANES - n_heads))).astype(BF16)
    b_f = jnp.pad(fox_b_f[0], (0, LANES - n_heads)).reshape(1, LANES)
    q, k, og, v, logit_bound = _fox_in(x, mods[0], w_qko, w_vt, w_f, b_f, fox_q_g[0], fox_k_g[0],
                                       tm=tm, nc=nc)
    att = _fox_attention(q, k, v, og, logit_bound, tq=tq)
    x = _proj_ln(att, x, mods[0], fox_w_o[0].astype(BF16), ln_g[0, 0], ln_b[0, 0], tm=tm)
    x = _ffn(x, mods[0], ffn_w_in[0].astype(BF16), ffn_w_out[0].astype(BF16), ln_g[0, 1], ln_b[0, 1],
             tm=tm, fc=fc)

    vec_in = jnp.stack([rwkv_w0[0], rwkv_a0[0], rwkv_k_k[0], rwkv_k_a[0]])
    r, lw, k, v, kk, a, g = _rwkv_in(
        x, mods[1], rwkv_mu[0], rwkv_w_rkv[0].astype(BF16),
        rwkv_w1[0].astype(BF16), rwkv_w2[0].astype(BF16),
        rwkv_a1[0].astype(BF16), rwkv_a2[0].astype(BF16),
        rwkv_g1[0].astype(BF16), rwkv_g2[0].astype(BF16), vec_in, tm=tm_rwkv)
    y = _rwkv_scan(r, lw, k, v, kk, a, groups=groups)
    vec_out = jnp.stack([rwkv_lnx_g[0], rwkv_lnx_b[0], rwkv_r_k[0].reshape(d)])
    x = _rwkv_out(y, r, k, v, g, x, mods[1], vec_out, rwkv_w_o[0].astype(BF16), ln_g[1, 0], ln_b[1, 0],
                  tm=tm)
    x = _ffn(x, mods[1], ffn_w_in[1].astype(BF16), ffn_w_out[1].astype(BF16), ln_g[1, 1], ln_b[1, 1],
             tm=tm, fc=fc)
    return x


def kernel(x, c, ada_w, ada_b, ln_g, ln_b, ffn_w_in, ffn_w_out, fox_w_in, fox_b_f, fox_q_g, fox_k_g, fox_w_o, rwkv_mu, rwkv_w_rkv, rwkv_w0, rwkv_w1, rwkv_w2, rwkv_a0, rwkv_a1, rwkv_a2, rwkv_g1, rwkv_g2, rwkv_k_k, rwkv_k_a, rwkv_r_k, rwkv_lnx_g, rwkv_lnx_b, rwkv_w_o):
    return _trunk(x, c, ada_w, ada_b, ln_g, ln_b, ffn_w_in, ffn_w_out, fox_w_in, fox_b_f, fox_q_g, fox_k_g, fox_w_o, rwkv_mu, rwkv_w_rkv, rwkv_w0, rwkv_w1, rwkv_w2, rwkv_a0, rwkv_a1, rwkv_a2, rwkv_g1, rwkv_g2, rwkv_k_k, rwkv_k_a, rwkv_r_k, rwkv_lnx_g, rwkv_lnx_b, rwkv_w_o,
                  tm=512, tm_rwkv=512, tq=1024, groups=8, tn=1536, nc=512, fc=256)
```

```python
import functools

import jax
import jax.numpy as jnp
import numpy as np
from jax import lax
from jax.experimental import pallas as pl
from jax.experimental.pallas import tpu as pltpu

F32 = jnp.float32
BF16 = jnp.bfloat16

HEAD_DIM = 64
DEPTH = 2
DEEPNORM_ALPHA = (2 * DEPTH) ** 0.25
LN_EPS = 1e-5
QK_EPS = 1e-6
GN_EPS = HEAD_DIM * 1e-5
LANES = 128
SUM_W = 256
CHUNK = 64
NEG_BIG = -1e30
LOG2E = 1.4426950408889634
DECAY_SCALE = -0.6065306597126334
SAFE_LOGIT_RANGE = 96.0
N_SPLIT = 3
AUG_STRIDE = 8
VMEM_LIMIT = 56 * 1024 * 1024


def _dot(a, b):
    return jnp.dot(a.astype(BF16), b.astype(BF16), preferred_element_type=F32)


def _dot_nt(a, b):
    return lax.dot_general(a.astype(BF16), b.astype(BF16), (((1,), (1,)), ((), ())),
                           preferred_element_type=F32)


def _dot_tn(a, b):
    return lax.dot_general(a.astype(BF16), b.astype(BF16), (((0,), (0,)), ((), ())),
                           preferred_element_type=F32)


def _split2(a):
    hi = a.astype(BF16)
    lo = (a - hi.astype(F32)).astype(BF16)
    return hi, lo


def _split3(a):
    hi = a.astype(BF16)
    r1 = a - hi.astype(F32)
    mid = r1.astype(BF16)
    lo = (r1 - mid.astype(F32)).astype(BF16)
    return hi, mid, lo


def _dot3(a, b):
    ah, al = _split2(a)
    bh, bl = _split2(b)
    d = functools.partial(jnp.dot, preferred_element_type=F32)
    return d(ah, bh) + (d(al, bh) + d(ah, bl))


def _dot_exact_rhs(a, b_exact):
    ah, al = _split2(a)
    d = functools.partial(jnp.dot, preferred_element_type=F32)
    return d(ah, b_exact) + d(al, b_exact)


def _head_sum_matrix():
    idx = jnp.arange(SUM_W) // HEAD_DIM
    return (idx[:, None] == idx[None, :]).astype(BF16)


def _layer_norm(z, g, b):
    mu = jnp.mean(z, axis=-1, keepdims=True)
    zc = z - mu
    var = jnp.mean(zc * zc, axis=-1, keepdims=True)
    return zc * lax.rsqrt(var + LN_EPS) * g + b


def _head_sums(a, head_sum_matrix):
    return jnp.dot(a.astype(BF16), head_sum_matrix, preferred_element_type=F32)


def _params(*sem):
    return pltpu.CompilerParams(dimension_semantics=sem, vmem_limit_bytes=VMEM_LIMIT)


def _const_spec(shape):
    nd = len(shape)
    return pl.BlockSpec(shape, lambda *_: (0,) * nd, pipeline_mode=pl.Buffered(1))


def _mods_kernel(c_ref, w_ref, b_ref, o_ref):
    c = c_ref[...]
    ca = c * jax.nn.sigmoid(c)
    o_ref[0] = _dot3(ca, w_ref[0]) + b_ref[0]


def _mods(c, ada_w, ada_b, tn=1536):
    depth, d, n = ada_w.shape
    b = c.shape[0]
    rows = 8
    cp = jnp.pad(c, ((0, rows - b), (0, 0)))
    out = pl.pallas_call(
        _mods_kernel,
        grid=(depth, n // tn),
        in_specs=[pl.BlockSpec((rows, d), lambda l, j: (0, 0)),
                  pl.BlockSpec((1, d, tn), lambda l, j: (l, 0, j)),
                  pl.BlockSpec((1, 1, tn), lambda l, j: (l, 0, j))],
        out_specs=pl.BlockSpec((1, rows, tn), lambda l, j: (l, 0, j)),
        out_shape=jax.ShapeDtypeStruct((depth, rows, n), F32),
        compiler_params=_params("parallel", "parallel"),
        name="adaln_mods",
    )(cp, ada_w, ada_b.reshape(depth, 1, n))
    return out[:, :b].reshape(depth, b, 6, d)


def _fox_in_kernel(x_ref, mod_ref, w_ref, wf_ref, bf_ref, qg_ref, kg_ref, bd_ref, tri_ref,
                   pq_ref, pk_ref, oq_ref, ok_ref, wvt_ref,
                   q_ref, k_ref, o_ref, vt_ref, carry_ref, *, d, nc):
    t = pl.program_id(1)
    x = x_ref[0]
    mod = mod_ref[0]
    h = (x * (1.0 + mod[1:2]) + mod[0:1]).astype(BF16)
    bd = bd_ref[...]
    qg = qg_ref[...]
    kg = kg_ref[...]
    dd = functools.partial(jnp.dot, preferred_element_type=F32)

    fl = dd(h, wf_ref[...]) + bf_ref[...]
    lf = jnp.minimum(fl, 0.0) - jnp.log(1.0 + jnp.exp(-jnp.abs(fl)))
    tri = tri_ref[...]
    hi, mid, lo = _split3(lf)
    cs = dd(tri, hi) + (dd(tri, mid) + dd(tri, lo))

    @pl.when(t == 0)
    def _():
        carry_ref[...] = jnp.zeros_like(carry_ref)

    f = cs + carry_ref[0:1, :]
    tm = f.shape[0]
    carry_ref[...] = jnp.broadcast_to(f[tm - 1:tm, :], carry_ref.shape)
    n_heads = d // HEAD_DIM
    lane = lax.broadcasted_iota(jnp.int32, (tm, LANES), 1)
    parts = [jnp.where(lane < n_heads, p.astype(F32), 0.0) for p in _split3(f * LOG2E)]
    fpack = parts[0]
    for i in range(1, N_SPLIT):
        fpack = fpack + pltpu.roll(parts[i], i * n_heads, axis=1)
    fpack = fpack.astype(BF16)
    low = lane < HEAD_DIM

    def head_rms(zp, g):
        ss = _head_sums(zp * zp, bd)
        return zp * lax.rsqrt(ss * (1.0 / HEAD_DIM) + QK_EPS) * g

    aug_all = {0: dd(fpack, pq_ref[...]) + oq_ref[...], 1: dd(fpack, pk_ref[...]) + ok_ref[...]}
    aug_even = (lane >= HEAD_DIM) & (lane < HEAD_DIM + AUG_STRIDE)
    aug_odd = lane < AUG_STRIDE

    def store_augmented(out_ref, zj, tile, aug):
        even = pltpu.roll(aug, (HEAD_DIM - AUG_STRIDE * 2 * tile) % LANES, axis=1)
        odd = pltpu.roll(aug, (-AUG_STRIDE * (2 * tile + 1)) % LANES, axis=1)
        sl = slice(2 * tile * LANES, (2 * tile + 2) * LANES)
        out_ref[0, :, sl] = jnp.concatenate(
            [jnp.where(low, zj, jnp.where(aug_even, even, 0.0)),
             jnp.where(low, jnp.where(aug_odd, odd, 0.0), zj)], axis=1).astype(out_ref.dtype)

    for sec, out_ref in enumerate((q_ref, k_ref, o_ref)):
        for c in range(d // nc):
            col = sec * d + c * nc
            z = dd(h, w_ref[:, col:col + nc])
            if sec == 2:
                out_ref[0, :, c * nc:(c + 1) * nc] = jax.nn.sigmoid(z).astype(out_ref.dtype)
                continue
            for pr in range(nc // SUM_W):
                zn = head_rms(z[:, pr * SUM_W:(pr + 1) * SUM_W], qg if sec == 0 else kg)
                for j in range(SUM_W // LANES):
                    tile = (c * nc + pr * SUM_W) // LANES + j
                    store_augmented(out_ref, zn[:, j * LANES:(j + 1) * LANES], tile, aug_all[sec])
    for c in range(d // nc):
        vt_ref[0, c * nc:(c + 1) * nc, :] = lax.dot_general(
            wvt_ref[c * nc:(c + 1) * nc, :], h, (((1,), (1,)), ((), ())),
            preferred_element_type=F32).astype(vt_ref.dtype)


def _fox_in(x, mod, w_qko, w_vt, w_f, b_f, q_g, k_g, tm=512, nc=512):
    b, t, d = x.shape
    n_heads = d // HEAD_DIM
    assert N_SPLIT * n_heads <= LANES
    qg = jnp.tile(q_g * (HEAD_DIM ** -0.5 * LOG2E), SUM_W // HEAD_DIM).reshape(1, SUM_W)
    kg = jnp.tile(k_g, SUM_W // HEAD_DIM).reshape(1, SUM_W)
    logit_bound = HEAD_DIM * jnp.max(jnp.abs(qg)) * jnp.max(jnp.abs(kg))
    bd = _head_sum_matrix()
    tri = (jnp.arange(tm)[:, None] >= jnp.arange(tm)[None, :]).astype(BF16)
    assert AUG_STRIDE * n_heads <= LANES and 2 * N_SPLIT + 1 <= AUG_STRIDE
    pq = np.zeros((LANES, LANES), np.float32)
    pk = np.zeros_like(pq)
    oq = np.zeros((1, LANES), np.float32)
    ok = np.zeros_like(oq)
    shift_lane = np.zeros_like(oq)
    for hd in range(n_heads):
        base = hd * AUG_STRIDE
        for part in range(N_SPLIT):
            pq[part * n_heads + hd, base + N_SPLIT + part] = 1.0
            ok[0, base + N_SPLIT + part] = 1.0
            pk[part * n_heads + hd, base + part] = -1.0
            oq[0, base + part] = 1.0
        shift_lane[0, base + 2 * N_SPLIT] = 1.0
    pq = jnp.asarray(pq, BF16)
    pk = jnp.asarray(pk, BF16)
    oq = jnp.asarray(oq) - logit_bound * jnp.asarray(shift_lane)
    ok = jnp.asarray(ok + shift_lane)
    act = jax.ShapeDtypeStruct((b, t, d), BF16)
    aug = jax.ShapeDtypeStruct((b, t, n_heads * LANES), BF16)
    row_spec = pl.BlockSpec((1, tm, d), lambda i, j: (i, j, 0))
    aug_spec = pl.BlockSpec((1, tm, n_heads * LANES), lambda i, j: (i, j, 0))
    kern = functools.partial(_fox_in_kernel, d=d, nc=nc)
    consts = (w_qko, w_f, b_f, qg, kg, bd, tri, pq, pk, oq, ok, w_vt)
    q_aug, k_aug, og, vt = pl.pallas_call(
        kern,
        grid=(b, t // tm),
        in_specs=[row_spec, pl.BlockSpec((1, 6, d), lambda i, j: (i, 0, 0))]
        + [_const_spec(c.shape) for c in consts],
        out_specs=[aug_spec, aug_spec, row_spec, pl.BlockSpec((1, d, tm), lambda i, j: (i, 0, j))],
        out_shape=[aug, aug, act, jax.ShapeDtypeStruct((b, d, t), BF16)],
        scratch_shapes=[pltpu.VMEM((8, LANES), F32)],
        compiler_params=_params("parallel", "arbitrary"),
        name="fox_in_proj",
    )(x, mod, *consts)
    return q_aug, k_aug, og, vt, logit_bound


def _attn_kernel(q_ref, k_ref, vt_ref, g_ref, y_ref, *, tq, tk):
    qi = pl.program_id(2)
    heads = LANES // HEAD_DIM
    pair = 2 * tk
    key = lax.broadcasted_iota(jnp.int32, (tk, tq), 0)
    qry = lax.broadcasted_iota(jnp.int32, (tk, tq), 1) + qi * tq
    qs = [q_ref[0, :, hh * LANES:(hh + 1) * LANES] for hh in range(heads)]

    def step(j, carry, masked):
        starts = [pl.multiple_of(j * pair + bb * tk, tk) for bb in range(2)]

        def scores(hh, bb):
            z = _dot_nt(k_ref[0, pl.ds(starts[bb], tk), hh * LANES:(hh + 1) * LANES], qs[hh])
            if masked:
                z = jnp.where(key + starts[bb] <= qry, z, NEG_BIG)
            return z

        def weights(z, m):
            p = jnp.exp2(z - m)
            return p, jnp.sum(p, axis=0, keepdims=True)

        def values(hh, bb, p):
            vt = vt_ref[0, hh * HEAD_DIM:(hh + 1) * HEAD_DIM, pl.ds(starts[bb], tk)]
            return jnp.dot(vt, p.astype(BF16), preferred_element_type=F32)

        colmax = lambda z: jnp.max(z, axis=0, keepdims=True)
        (m0, l0, a0), (m1, l1, a1) = carry
        z00 = scores(0, 0)
        z10 = scores(1, 0)
        m0a = jnp.maximum(m0, colmax(z00))
        z01 = scores(0, 1)
        p00, s00 = weights(z00, m0a)
        m1a = jnp.maximum(m1, colmax(z10))
        z11 = scores(1, 1)
        v00 = values(0, 0, p00)
        p10, s10 = weights(z10, m1a)
        m0b = jnp.maximum(m0a, colmax(z01))
        v10 = values(1, 0, p10)
        p01, s01 = weights(z01, m0b)
        m1b = jnp.maximum(m1a, colmax(z11))
        v01 = values(0, 1, p01)
        p11, s11 = weights(z11, m1b)
        v11 = values(1, 1, p11)

        def merge(m, l, a, ma, mb, sa, sb, va, vb):
            ra = jnp.exp2(m - ma)
            rb = jnp.exp2(ma - mb)
            return mb, rb * (ra * l + sa) + sb, rb * (ra * a + va) + vb

        return (merge(m0, l0, a0, m0a, m0b, s00, s01, v00, v01),
                merge(m1, l1, a1, m1a, m1b, s10, s11, v10, v11))

    init = tuple((jnp.full((1, tq), NEG_BIG, F32), jnp.zeros((1, tq), F32),
                  jnp.zeros((HEAD_DIM, tq), F32)) for _ in range(heads))
    n_full = (qi * tq) // pair
    carry = lax.fori_loop(0, n_full, functools.partial(step, masked=False), init)
    for jm in range(tq // pair):
        carry = step(n_full + jm, carry, True)
    yt = jnp.concatenate([carry[hh][2] / carry[hh][1] for hh in range(heads)], axis=0)
    y_ref[0] = (yt.T * g_ref[0].astype(F32)).astype(y_ref.dtype)


def _attn_bounded_kernel(q_ref, k_ref, vt_ref, g_ref, y_ref, *, tq, q_parts=2, step_blocks=(2, 1)):
    qi = pl.program_id(2)
    heads = LANES // HEAD_DIM
    part = tq // q_parts
    slots = [(hh, pi) for hh in range(heads) for pi in range(q_parts)]

    def run_chains(chains, carry):
        def scores(hh, k0, nk, pi, q0, nq, masked):
            row = pi * part + q0
            z = _dot_nt(k_ref[0, pl.ds(k0, nk), hh * LANES:(hh + 1) * LANES],
                        q_ref[0, row:row + nq, hh * LANES:(hh + 1) * LANES])
            if masked:
                key = lax.broadcasted_iota(jnp.int32, (nk, nq), 0) + k0
                qry = lax.broadcasted_iota(jnp.int32, (nk, nq), 1) + (qi * tq + row)
                z = jnp.where(key <= qry, z, NEG_BIG)
            return z

        sums = {s: [] for s in slots}
        vals = {s: [] for s in slots}
        z_next = scores(*chains[0])
        for c, (hh, k0, nk, pi, q0, nq, _) in enumerate(chains):
            z = z_next
            if c + 1 < len(chains):
                z_next = scores(*chains[c + 1])
            p = jnp.exp2(z)
            vt = vt_ref[0, hh * HEAD_DIM:(hh + 1) * HEAD_DIM, pl.ds(k0, nk)]
            s = jnp.sum(p, axis=0, keepdims=True)
            v = jnp.dot(vt, p.astype(BF16), preferred_element_type=F32)
            if nq != part:
                s = jnp.concatenate([jnp.zeros((1, q0), F32), s], axis=1)
                v = jnp.concatenate([jnp.zeros((HEAD_DIM, q0), F32), v], axis=1)
            sums[(hh, pi)].append(s)
            vals[(hh, pi)].append(v)
        out = []
        for n, s in enumerate(slots):
            l, acc = carry[n]
            if sums[s]:
                l = l + functools.reduce(jnp.add, sums[s])
                acc = acc + functools.reduce(jnp.add, vals[s])
            out.append((l, acc))
        return tuple(out)

    def full_step(j, carry, nb, base):
        starts = [pl.multiple_of(base + j * (nb * part) + bb * part, part) for bb in range(nb)]
        return run_chains([(hh, starts[bb], part, pi, 0, part, False)
                           for bb in range(nb) for (hh, pi) in slots], carry)

    carry = tuple((jnp.zeros((1, part), F32), jnp.zeros((HEAD_DIM, part), F32)) for _ in slots)
    done = 0
    for nb in step_blocks:
        n_steps = (qi * tq - done) // (nb * part)
        carry = lax.fori_loop(0, n_steps, functools.partial(full_step, nb=nb, base=done), carry)
        done = done + n_steps * (nb * part)
    half = part // 2
    chains = []
    for kb in range(q_parts):
        k0 = pl.multiple_of(qi * tq + kb * part, part)
        for hh in range(heads):
            chains.append((hh, k0, half, kb, 0, part, True))
            chains.append((hh, pl.multiple_of(k0 + half, half), half, kb, half, half, True))
            chains += [(hh, k0, part, pi, 0, part, False) for pi in range(kb + 1, q_parts)]
    carry = run_chains(chains, carry)
    yt = jnp.concatenate(
        [jnp.concatenate([carry[slots.index((hh, pi))][1] / carry[slots.index((hh, pi))][0]
                          for hh in range(heads)], axis=0) for pi in range(q_parts)], axis=1)
    y_ref[0] = (yt.T * g_ref[0].astype(F32)).astype(y_ref.dtype)


def _fox_attention(q_aug, k_aug, vt, og, logit_bound, tq=1024, q_parts=2):
    part = tq // q_parts
    return lax.cond(
        2.0 * logit_bound <= SAFE_LOGIT_RANGE,
        functools.partial(_attention_call, tq=tq, name="fox_attention_bounded",
                          kernel_fn=functools.partial(_attn_bounded_kernel, tq=tq, q_parts=q_parts)),
        functools.partial(_attention_call, tq=part, name="fox_attention",
                          kernel_fn=functools.partial(_attn_kernel, tq=part, tk=part // 2)),
        q_aug, k_aug, vt, og)


def _attention_call(q_aug, k_aug, vt, og, *, kernel_fn, tq, name):
    b, d, t = vt.shape
    heads = LANES // HEAD_DIM
    blk = pl.BlockSpec((1, tq, LANES), lambda i, p, j: (i, j, p))
    q_blk = pl.BlockSpec((1, tq, heads * LANES), lambda i, p, j: (i, j, p))
    k_full = pl.BlockSpec((1, t, heads * LANES), lambda i, p, j: (i, 0, p))
    vt_full = pl.BlockSpec((1, LANES, t), lambda i, p, j: (i, p, 0))
    return pl.pallas_call(
        kernel_fn,
        grid=(b, d // LANES, t // tq),
        in_specs=[q_blk, k_full, vt_full, blk],
        out_specs=blk,
        out_shape=jax.ShapeDtypeStruct((b, t, d), BF16),
        compiler_params=_params("parallel", "parallel", "arbitrary"),
        name=name,
    )(q_aug, k_aug, vt, og)


def _proj_ln_kernel(a_ref, x_ref, mod_ref, w_ref, lng_ref, lnb_ref, o_ref):
    y = jnp.dot(a_ref[0], w_ref[...], preferred_element_type=F32)
    z = DEEPNORM_ALPHA * x_ref[0] + mod_ref[0][2:3] * y
    o_ref[0] = _layer_norm(z, lng_ref[...], lnb_ref[...])


def _proj_ln(a, x, mod, w, ln_g, ln_b, tm=512):
    b, t, d = x.shape
    row_spec = pl.BlockSpec((1, tm, d), lambda i, j: (i, j, 0))
    return pl.pallas_call(
        _proj_ln_kernel,
        grid=(b, t // tm),
        in_specs=[row_spec, row_spec,
                  pl.BlockSpec((1, 6, d), lambda i, j: (i, 0, 0)),
                  _const_spec(w.shape), _const_spec((1, d)), _const_spec((1, d))],
        out_specs=row_spec,
        out_shape=jax.ShapeDtypeStruct((b, t, d), F32),
        compiler_params=_params("parallel", "parallel"),
        name="attn_out_proj_ln",
    )(a, x, mod, w, ln_g.reshape(1, d), ln_b.reshape(1, d))


def _ffn_kernel(x_ref, mod_ref, win_ref, wout_ref, lng_ref, lnb_ref, o_ref, act_ref, *, d_ff, fc):
    x = x_ref[0]
    mod = mod_ref[0]
    h = (x * (1.0 + mod[4:5]) + mod[3:4]).astype(BF16)
    for c in range(d_ff // fc):
        g = jnp.dot(h, win_ref[:, c * fc:(c + 1) * fc], preferred_element_type=F32)
        u = jnp.dot(h, win_ref[:, d_ff + c * fc:d_ff + (c + 1) * fc], preferred_element_type=F32)
        act_ref[:, c * fc:(c + 1) * fc] = (g * jax.nn.sigmoid(g) * u).astype(BF16)
    y = jnp.dot(act_ref[...], wout_ref[...], preferred_element_type=F32)
    z = DEEPNORM_ALPHA * x + mod[5:6] * y
    o_ref[0] = _layer_norm(z, lng_ref[...], lnb_ref[...])


def _ffn(x, mod, w_in, w_out, ln_g, ln_b, tm=512, fc=256):
    b, t, d = x.shape
    d_ff = w_out.shape[0]
    row_spec = pl.BlockSpec((1, tm, d), lambda i, j: (i, j, 0))
    return pl.pallas_call(
        functools.partial(_ffn_kernel, d_ff=d_ff, fc=fc),
        grid=(b, t // tm),
        in_specs=[row_spec,
                  pl.BlockSpec((1, 6, d), lambda i, j: (i, 0, 0)),
                  _const_spec(w_in.shape), _const_spec(w_out.shape),
                  _const_spec((1, d)), _const_spec((1, d))],
        out_specs=row_spec,
        out_shape=jax.ShapeDtypeStruct((b, t, d), F32),
        scratch_shapes=[pltpu.VMEM((tm, d_ff), BF16)],
        compiler_params=_params("parallel", "parallel"),
        name="swiglu_ln",
    )(x, mod, w_in, w_out, ln_g.reshape(1, d), ln_b.reshape(1, d))


def _rwkv_in_kernel(x_ref, xp_ref, mod_ref, mu_ref, wrkv_ref, w1_ref, w2_ref, a1_ref, a2_ref,
                    g1_ref, g2_ref, vec_ref, bd_ref,
                    r_ref, lw_ref, k_ref, v_ref, kk_ref, a_ref, g_ref, *, d):
    t = pl.program_id(1)
    mod = mod_ref[0]
    sc = 1.0 + mod[1:2]
    sh = mod[0:1]
    h = x_ref[0] * sc + sh
    tm = h.shape[0]
    prev = xp_ref[0][7:8, :] * sc + sh
    prev = jnp.where(t == 0, jnp.zeros_like(prev), prev)
    rows = lax.broadcasted_iota(jnp.int32, h.shape, 0)
    hprev = jnp.where(rows == 0, prev, pltpu.roll(h, 1, axis=0))
    dx = hprev - h
    mu = mu_ref[...]
    vec = vec_ref[...]
    w0, a0, k_k, k_a = vec[0:1], vec[1:2], vec[2:3], vec[3:4]

    def mix(n):
        return (h + dx * mu[n:n + 1]).astype(BF16)

    dd = functools.partial(jnp.dot, preferred_element_type=F32)
    r = dd(mix(0), wrkv_ref[0])
    k = dd(mix(1), wrkv_ref[1])
    v = dd(mix(2), wrkv_ref[2])
    ww = w0 + dd(jnp.tanh(dd(mix(3), w1_ref[...])).astype(BF16), w2_ref[...])
    lw = DECAY_SCALE * jax.nn.sigmoid(ww)
    a = jax.nn.sigmoid(a0 + dd(dd(mix(4), a1_ref[...]).astype(BF16), a2_ref[...]))
    g = dd(jax.nn.sigmoid(dd(mix(5), g1_ref[...])).astype(BF16), g2_ref[...])
    kk = k * k_k
    k = k * (1.0 + (a - 1.0) * k_a)
    bd = bd_ref[...]
    for j in range(d // SUM_W):
        sl = slice(j * SUM_W, (j + 1) * SUM_W)
        kkj = kk[:, sl]
        ss = _head_sums(kkj * kkj, bd)
        kk_ref[0, :, sl] = (kkj * jnp.minimum(lax.rsqrt(ss), 1e12)).astype(kk_ref.dtype)
    r_ref[0] = r.astype(r_ref.dtype)
    lw_ref[0] = lw
    k_ref[0] = k.astype(k_ref.dtype)
    v_ref[0] = v.astype(v_ref.dtype)
    a_ref[0] = a.astype(a_ref.dtype)
    g_ref[0] = g.astype(g_ref.dtype)


def _rwkv_in(x, mod, mu, w_rkv, w1, w2, a1, a2, g1, g2, vec, tm=512):
    b, t, d = x.shape
    bd = _head_sum_matrix()
    row_spec = pl.BlockSpec((1, tm, d), lambda i, j: (i, j, 0))
    prev_spec = pl.BlockSpec((1, 8, d), lambda i, j: (i, jnp.maximum(j * (tm // 8) - 1, 0), 0))
    act = jax.ShapeDtypeStruct((b, t, d), BF16)
    consts = (mu, w_rkv, w1, w2, a1, a2, g1, g2, vec, bd)
    return pl.pallas_call(
        functools.partial(_rwkv_in_kernel, d=d),
        grid=(b, t // tm),
        in_specs=[row_spec, prev_spec, pl.BlockSpec((1, 6, d), lambda i, j: (i, 0, 0))]
        + [_const_spec(c.shape) for c in consts],
        out_specs=[row_spec] * 7,
        out_shape=[act, jax.ShapeDtypeStruct((b, t, d), F32), act, act, act, act, act],
        compiler_params=_params("parallel", "parallel"),
        name="rwkv_in_proj",
    )(x, x, mod, *consts)


def _rwkv_scan_kernel(r_ref, lw_ref, k_ref, v_ref, kk_ref, a_ref, lev_ref, tri_ref,
                      y_ref, h_ref, yi_ref, rq_ref, m_ref, n_ref, *, groups, steps_per_seq):
    s = pl.program_id(0)
    gr = 2 * CHUNK
    st = 2 * gr
    n_lev = CHUNK.bit_length() - 1

    @pl.when(s == 0)
    def _():
        h_ref[...] = jnp.zeros_like(h_ref)
        yi_ref[...] = jnp.zeros_like(yi_ref)
        rq_ref[...] = jnp.zeros_like(rq_ref)
        m_ref[...] = jnp.zeros_like(m_ref)
        n_ref[...] = jnp.zeros_like(n_ref)

    first_of_seq = lax.rem(jnp.maximum(s - 1, 0), steps_per_seq) == 0
    state = [jnp.where(first_of_seq, 0.0, h_ref[...])]
    pending = list(range(2 * groups))

    def state_steps(count):
        for _ in range(count):
            if pending:
                i = pending.pop(0)
                lo = i * CHUNK
                h = state[0]
                y_ref[0, lo:lo + CHUNK, :] = (yi_ref[lo:lo + CHUNK, :]
                                              + _dot(rq_ref[lo:lo + CHUNK, :], h))
                state[0] = _dot3(m_ref[i], h) + n_ref[i]

    lane = lax.broadcasted_iota(jnp.int32, (gr, LANES), 1)
    head0 = lane < HEAD_DIM
    lev = lev_ref[...]
    tri = tri_ref[...]
    strict = lev >= 0
    srow = lax.broadcasted_iota(jnp.int32, (st, st), 0)
    scol = lax.broadcasted_iota(jnp.int32, (st, st), 1)
    eye_st = srow == scol
    incl = strict | eye_st
    hrow = lax.broadcasted_iota(jnp.int32, (LANES, LANES), 0)
    hcol = lax.broadcasted_iota(jnp.int32, (LANES, LANES), 1)
    same_head = (hrow < HEAD_DIM) == (hcol < HEAD_DIM)
    eye_h = hrow == hcol

    def stack(x):
        zero = jnp.zeros_like(x)
        return jnp.concatenate([jnp.where(head0, x, zero), jnp.where(head0, zero, x)], axis=0)

    def unstack(x):
        return x[:gr] + x[gr:]

    def log_decay_cumsum(g):
        hi, mid, lo = _split3(lw_ref[0, g * gr:(g + 1) * gr, :])
        dd = functools.partial(jnp.dot, preferred_element_type=F32)
        return dd(tri, hi) + (dd(tri, mid) + dd(tri, lo))

    def prep(g, cum):
        rs = slice(g * gr, (g + 1) * gr)
        lw = lw_ref[0, rs, :]
        clast = jnp.concatenate(
            [jnp.broadcast_to(cum[(j + 1) * CHUNK - 1:(j + 1) * CHUNK], (CHUNK, LANES))
             for j in range(2)], axis=0)
        g_inv = jnp.exp(-cum)
        g_rem = jnp.exp(clast - cum)
        r = r_ref[0, rs, :].astype(F32)
        k = k_ref[0, rs, :].astype(F32)
        v = v_ref[0, rs, :].astype(F32)
        kk = kk_ref[0, rs, :].astype(F32)
        bt = kk * a_ref[0, rs, :].astype(F32)
        a_st = stack(-kk * jnp.exp(cum - lw))
        r_st = stack(r * jnp.exp(cum))
        b_st = stack(bt * g_inv)
        k_st = stack(k * g_inv)
        prods = []
        for hd in range(2):
            hs = slice(hd * gr, (hd + 1) * gr)
            prods.append(_dot_nt(jnp.concatenate([a_st[hs], r_st[hs]], axis=0),
                                 jnp.concatenate([b_st[hs], k_st[hs]], axis=0)))
        zero = jnp.zeros((gr, gr), F32)

        def by_head(rows, cols):
            return jnp.concatenate(
                [jnp.concatenate([prods[0][rows, cols], zero], axis=1),
                 jnp.concatenate([zero, prods[1][rows, cols]], axis=1)], axis=0)

        top, bot = slice(0, gr), slice(gr, st)
        return dict(
            v=v, v_st=stack(v), a_st=a_st, r_st=r_st, b_rem=bt * g_rem, k_rem=k * g_rem,
            g_last=jnp.exp(clast),
            a_ab=jnp.where(strict, by_head(top, top), 0.0),
            a_ak=jnp.where(strict, by_head(top, bot), 0.0),
            a_rb=jnp.where(incl, by_head(bot, top), 0.0),
            a_rk=jnp.where(incl, by_head(bot, bot), 0.0))

    def lower_rows(m, bsz):
        return jnp.concatenate([m[i + bsz:i + 2 * bsz] for i in range(0, st, 2 * bsz)], axis=0)

    def scatter_lower(full, low, bsz):
        parts = []
        for n, i in enumerate(range(0, st, 2 * bsz)):
            parts += [full[i:i + bsz], low[n * bsz:(n + 1) * bsz]]
        return jnp.concatenate(parts, axis=0)

    def precompute(gs):
        cums = {g: log_decay_cumsum(g) for g in gs}
        yield
        ps = {g: prep(g, cums[g]) for g in gs}
        yield
        xs = {g: jnp.where(eye_st, 1.0, jnp.where(lev == 0, ps[g]["a_ab"], 0.0)) for g in gs}
        for level in range(1, n_lev):
            bsz = 1 << level
            a_off = {g: jnp.where(lev == level, ps[g]["a_ab"], 0.0) for g in gs}
            if bsz % 8:
                ws = {g: _dot(a_off[g], xs[g]) for g in gs}
                yield
                xs = {g: xs[g] + _dot(xs[g], ws[g]) for g in gs}
            else:
                w_low = {g: _dot(lower_rows(a_off[g], bsz), xs[g]) for g in gs}
                yield
                zero = jnp.zeros((st, st), F32)
                upd = {g: _dot(lower_rows(xs[g], bsz), scatter_lower(zero, w_low[g], bsz))
                       for g in gs}
                xs = {g: scatter_lower(xs[g], lower_rows(xs[g], bsz) + upd[g], bsz) for g in gs}
            yield
        gm = {g: jnp.concatenate([_dot(ps[g]["a_ak"], ps[g]["v_st"]), ps[g]["a_st"]], axis=1)
              for g in gs}
        yield
        tg = {g: _dot(xs[g], gm[g]) for g in gs}
        yield
        rb = {g: _dot(ps[g]["a_rb"], tg[g]) for g in gs}
        rk = {g: _dot(ps[g]["a_rk"], ps[g]["v_st"]) for g in gs}
        yield
        assert not pending
        for g in gs:
            p = ps[g]
            yi_ref[g * gr:(g + 1) * gr, :] = unstack(rb[g][:, :LANES] + rk[g])
            rq_ref[g * gr:(g + 1) * gr, :] = unstack(p["r_st"] + rb[g][:, LANES:])
            uv = unstack(tg[g][:, :LANES])
            wa = unstack(tg[g][:, LANES:])
            for j in range(2):
                cs = slice(j * CHUNK, (j + 1) * CHUNK)
                lhs_t = jnp.concatenate([p["b_rem"][cs], p["k_rem"][cs]], axis=0)
                rhs_t = jnp.concatenate(
                    [jnp.concatenate([uv[cs], wa[cs]], axis=1),
                     jnp.concatenate([p["v"][cs], jnp.zeros_like(p["v"][cs])], axis=1)], axis=0)
                nm = _dot_tn(lhs_t, rhs_t)
                n_ref[2 * g + j] = jnp.where(same_head, nm[:, :LANES], 0.0)
                decay = jnp.broadcast_to(p["g_last"][j * CHUNK:j * CHUNK + 1], (LANES, LANES))
                m_ref[2 * g + j] = (jnp.where(same_head, nm[:, LANES:], 0.0)
                                    + jnp.where(eye_h, decay, 0.0))

    n_stages = 2 * n_lev + 3
    for stage, _ in enumerate(precompute(range(groups))):
        state_steps(-(-len(pending) // (n_stages - stage)))
        if not pending and state:
            h_ref[...] = state.pop()


def _rwkv_scan(r, lw, k, v, kk, a, groups=2):
    b, t, d = r.shape
    gr = 2 * CHUNK
    st = 2 * gr
    rows = gr * groups
    n_hp = d // LANES
    steps_per_seq = t // rows
    n_blocks = b * n_hp * steps_per_seq

    def block_index(blk):
        seq = blk // steps_per_seq
        return seq // n_hp, blk % steps_per_seq, seq % n_hp

    in_blk = pl.BlockSpec((1, rows, LANES), lambda s: block_index(jnp.minimum(s, n_blocks - 1)))
    out_blk = pl.BlockSpec((1, rows, LANES), lambda s: block_index(jnp.maximum(s - 1, 0)))
    idx = jnp.arange(st)
    xor = idx[:, None] ^ idx[None, :]
    same = (idx[:, None] // CHUNK) == (idx[None, :] // CHUNK)
    lower = idx[None, :] < idx[:, None]
    msb = jnp.floor(jnp.log2(jnp.maximum(xor, 1).astype(F32))).astype(jnp.int32)
    lev = jnp.where(same & lower, msb, -1).astype(jnp.int32)
    ti = jnp.arange(gr)
    tri = ((ti[:, None] >= ti[None, :]) & ((ti[:, None] // CHUNK) == (ti[None, :] // CHUNK))).astype(BF16)
    return pl.pallas_call(
        functools.partial(_rwkv_scan_kernel, groups=groups, steps_per_seq=steps_per_seq),
        grid=(n_blocks + 1,),
        in_specs=[in_blk] * 6 + [_const_spec(lev.shape), _const_spec(tri.shape)],
        out_specs=out_blk,
        out_shape=jax.ShapeDtypeStruct((b, t, d), F32),
        scratch_shapes=[pltpu.VMEM((LANES, LANES), F32),
                        pltpu.VMEM((rows, LANES), F32), pltpu.VMEM((rows, LANES), F32),
                        pltpu.VMEM((2 * groups, LANES, LANES), F32),
                        pltpu.VMEM((2 * groups, LANES, LANES), F32)],
        compiler_params=_params("arbitrary"),
        name="rwkv7_chunk_scan",
    )(r, lw, k, v, kk, a, lev, tri)


def _rwkv_out_kernel(y_ref, r_ref, k_ref, v_ref, g_ref, x_ref, mod_ref, vec_ref, bd_ref, w_ref,
                     lng_ref, lnb_ref, o_ref, act_ref, *, d):
    bd = bd_ref[...]
    vec = vec_ref[...]
    inv = 1.0 / HEAD_DIM
    for j in range(d // SUM_W):
        sl = slice(j * SUM_W, (j + 1) * SUM_W)
        y = y_ref[0, :, sl]
        mean = _dot_exact_rhs(y, bd) * inv
        yc = y - mean
        var = _head_sums(yc * yc, bd) * inv
        yn = yc * lax.rsqrt(var + GN_EPS) * vec[0:1, sl] + vec[1:2, sl]
        r = r_ref[0, :, sl].astype(F32)
        k = k_ref[0, :, sl].astype(F32)
        bonus = _dot_exact_rhs(r * k * vec[2:3, sl], bd) * v_ref[0, :, sl].astype(F32)
        act_ref[:, sl] = ((yn + bonus) * g_ref[0, :, sl].astype(F32)).astype(BF16)
    out = jnp.dot(act_ref[...], w_ref[...], preferred_element_type=F32)
    z = DEEPNORM_ALPHA * x_ref[0] + mod_ref[0][2:3] * out
    o_ref[0] = _layer_norm(z, lng_ref[...], lnb_ref[...])


def _rwkv_out(y, r, k, v, g, x, mod, vec, w, ln_g, ln_b, tm=512):
    b, t, d = x.shape
    bd = _head_sum_matrix()
    row_spec = pl.BlockSpec((1, tm, d), lambda i, j: (i, j, 0))
    return pl.pallas_call(
        functools.partial(_rwkv_out_kernel, d=d),
        grid=(b, t // tm),
        in_specs=[row_spec] * 6
        + [pl.BlockSpec((1, 6, d), lambda i, j: (i, 0, 0)),
           _const_spec(vec.shape), _const_spec(bd.shape), _const_spec(w.shape),
           _const_spec((1, d)), _const_spec((1, d))],
        out_specs=row_spec,
        out_shape=jax.ShapeDtypeStruct((b, t, d), F32),
        scratch_shapes=[pltpu.VMEM((tm, d), BF16)],
        compiler_params=_params("parallel", "parallel"),
        name="rwkv_out_proj_ln",
    )(y, r, k, v, g, x, mod, vec, bd, w, ln_g.reshape(1, d), ln_b.reshape(1, d))


def _trunk(x, c, ada_w, ada_b, ln_g, ln_b, ffn_w_in, ffn_w_out, fox_w_in, fox_b_f, fox_q_g, fox_k_g, fox_w_o, rwkv_mu, rwkv_w_rkv, rwkv_w0, rwkv_w1, rwkv_w2, rwkv_a0, rwkv_a1, rwkv_a2, rwkv_g1, rwkv_g2, rwkv_k_k, rwkv_k_a, rwkv_r_k, rwkv_lnx_g, rwkv_lnx_b, rwkv_w_o,
           *, tm, tm_rwkv, tq, q_parts, groups, tn, nc, fc):
    b, t, d = x.shape
    n_heads = d // HEAD_DIM
    mods = _mods(c, ada_w, ada_b, tn=tn)

    w_in = fox_w_in[0]
    f_lo = 3 * d
    w_qko = jnp.concatenate([w_in[:, :2 * d], w_in[:, f_lo + n_heads:]], axis=1).astype(BF16)
    w_vt = w_in[:, 2 * d:f_lo].T.astype(BF16)
    w_f = jnp.pad(w_in[:, f_lo:f_lo + n_heads], ((0, 0), (0, LANES - n_heads))).astype(BF16)
    b_f = jnp.pad(fox_b_f[0], (0, LANES - n_heads)).reshape(1, LANES)
    q, k, og, v, logit_bound = _fox_in(x, mods[0], w_qko, w_vt, w_f, b_f, fox_q_g[0], fox_k_g[0],
                                       tm=tm, nc=nc)
    att = _fox_attention(q, k, v, og, logit_bound, tq=tq, q_parts=q_parts)
    x = _proj_ln(att, x, mods[0], fox_w_o[0].astype(BF16), ln_g[0, 0], ln_b[0, 0], tm=tm)
    x = _ffn(x, mods[0], ffn_w_in[0].astype(BF16), ffn_w_out[0].astype(BF16), ln_g[0, 1], ln_b[0, 1],
             tm=tm, fc=fc)

    vec_in = jnp.stack([rwkv_w0[0], rwkv_a0[0], rwkv_k_k[0], rwkv_k_a[0]])
    r, lw, k, v, kk, a, g = _rwkv_in(
        x, mods[1], rwkv_mu[0], rwkv_w_rkv[0].astype(BF16),
        rwkv_w1[0].astype(BF16), rwkv_w2[0].astype(BF16),
        rwkv_a1[0].astype(BF16), rwkv_a2[0].astype(BF16),
        rwkv_g1[0].astype(BF16), rwkv_g2[0].astype(BF16), vec_in, tm=tm_rwkv)
    y = _rwkv_scan(r, lw, k, v, kk, a, groups=groups)
    vec_out = jnp.stack([rwkv_lnx_g[0], rwkv_lnx_b[0], rwkv_r_k[0].reshape(d)])
    x = _rwkv_out(y, r, k, v, g, x, mods[1], vec_out, rwkv_w_o[0].astype(BF16), ln_g[1, 0], ln_b[1, 0],
                  tm=tm)
    x = _ffn(x, mods[1], ffn_w_in[1].astype(BF16), ffn_w_out[1].astype(BF16), ln_g[1, 1], ln_b[1, 1],
             tm=tm, fc=fc)
    return x


def kernel(x, c, ada_w, ada_b, ln_g, ln_b, ffn_w_in, ffn_w_out, fox_w_in, fox_b_f, fox_q_g, fox_k_g, fox_w_o, rwkv_mu, rwkv_w_rkv, rwkv_w0, rwkv_w1, rwkv_w2, rwkv_a0, rwkv_a1, rwkv_a2, rwkv_g1, rwkv_g2, rwkv_k_k, rwkv_k_a, rwkv_r_k, rwkv_lnx_g, rwkv_lnx_b, rwkv_w_o):
    return _trunk(x, c, ada_w, ada_b, ln_g, ln_b, ffn_w_in, ffn_w_out, fox_w_in, fox_b_f, fox_q_g, fox_k_g, fox_w_o, rwkv_mu, rwkv_w_rkv, rwkv_w0, rwkv_w1, rwkv_w2, rwkv_a0, rwkv_a1, rwkv_a2, rwkv_g1, rwkv_g2, rwkv_k_k, rwkv_k_a, rwkv_r_k, rwkv_lnx_g, rwkv_lnx_b, rwkv_w_o,
                  tm=512, tm_rwkv=512, tq=2048, q_parts=4, groups=8, tn=1536, nc=512, fc=256)
```

```python
import functools

import jax
import jax.numpy as jnp
import numpy as np
from jax import lax
from jax.experimental import pallas as pl
from jax.experimental.pallas import tpu as pltpu

F32 = jnp.float32
BF16 = jnp.bfloat16

HEAD_DIM = 64
DEPTH = 2
DEEPNORM_ALPHA = (2 * DEPTH) ** 0.25
LN_EPS = 1e-5
QK_EPS = 1e-6
GN_EPS = HEAD_DIM * 1e-5
LANES = 128
SUM_W = 256
CHUNK = 64
NEG_BIG = -1e30
LOG2E = 1.4426950408889634
DECAY_SCALE = -0.6065306597126334
SAFE_LOGIT_RANGE = 96.0
N_SPLIT = 3
LOOKAHEAD = 2
AUG_STRIDE = 8
VMEM_LIMIT = 56 * 1024 * 1024


def _dot(a, b):
    return jnp.dot(a.astype(BF16), b.astype(BF16), preferred_element_type=F32)


def _dot_nt(a, b):
    return lax.dot_general(a.astype(BF16), b.astype(BF16), (((1,), (1,)), ((), ())),
                           preferred_element_type=F32)


def _dot_tn(a, b):
    return lax.dot_general(a.astype(BF16), b.astype(BF16), (((0,), (0,)), ((), ())),
                           preferred_element_type=F32)


def _split2(a):
    hi = a.astype(BF16)
    lo = (a - hi.astype(F32)).astype(BF16)
    return hi, lo


def _split3(a):
    hi = a.astype(BF16)
    r1 = a - hi.astype(F32)
    mid = r1.astype(BF16)
    lo = (r1 - mid.astype(F32)).astype(BF16)
    return hi, mid, lo


def _dot3(a, b):
    ah, al = _split2(a)
    bh, bl = _split2(b)
    d = functools.partial(jnp.dot, preferred_element_type=F32)
    return d(ah, bh) + (d(al, bh) + d(ah, bl))


def _dot_exact_rhs(a, b_exact):
    ah, al = _split2(a)
    d = functools.partial(jnp.dot, preferred_element_type=F32)
    return d(ah, b_exact) + d(al, b_exact)


def _head_sum_matrix():
    idx = jnp.arange(SUM_W) // HEAD_DIM
    return (idx[:, None] == idx[None, :]).astype(BF16)


def _layer_norm(z, g, b):
    mu = jnp.mean(z, axis=-1, keepdims=True)
    zc = z - mu
    var = jnp.mean(zc * zc, axis=-1, keepdims=True)
    return zc * lax.rsqrt(var + LN_EPS) * g + b


def _head_sums(a, head_sum_matrix):
    return jnp.dot(a.astype(BF16), head_sum_matrix, preferred_element_type=F32)


def _params(*sem):
    return pltpu.CompilerParams(dimension_semantics=sem, vmem_limit_bytes=VMEM_LIMIT)


def _const_spec(shape):
    nd = len(shape)
    return pl.BlockSpec(shape, lambda *_: (0,) * nd, pipeline_mode=pl.Buffered(1))


def _mods_kernel(c_ref, w_ref, b_ref, o_ref):
    c = c_ref[...]
    ca = c * jax.nn.sigmoid(c)
    o_ref[0] = _dot3(ca, w_ref[0]) + b_ref[0]


def _mods(c, ada_w, ada_b, tn=1536):
    depth, d, n = ada_w.shape
    b = c.shape[0]
    rows = 8
    cp = jnp.pad(c, ((0, rows - b), (0, 0)))
    out = pl.pallas_call(
        _mods_kernel,
        grid=(depth, n // tn),
        in_specs=[pl.BlockSpec((rows, d), lambda l, j: (0, 0)),
                  pl.BlockSpec((1, d, tn), lambda l, j: (l, 0, j)),
                  pl.BlockSpec((1, 1, tn), lambda l, j: (l, 0, j))],
        out_specs=pl.BlockSpec((1, rows, tn), lambda l, j: (l, 0, j)),
        out_shape=jax.ShapeDtypeStruct((depth, rows, n), F32),
        compiler_params=_params("parallel", "parallel"),
        name="adaln_mods",
    )(cp, ada_w, ada_b.reshape(depth, 1, n))
    return out[:, :b].reshape(depth, b, 6, d)


def _fox_in_kernel(x_ref, mod_ref, w_ref, wf_ref, bf_ref, qg_ref, kg_ref, bd_ref, tri_ref,
                   pq_ref, pk_ref, oq_ref, ok_ref, wvt_ref,
                   q_ref, k_ref, o_ref, vt_ref, carry_ref, *, d, nc):
    t = pl.program_id(1)

    @pl.when(t == 0)
    def _():
        carry_ref[...] = jnp.zeros_like(carry_ref)

    x = x_ref[0]
    mod = mod_ref[0]
    h = (x * (1.0 + mod[1:2]) + mod[0:1]).astype(BF16)
    tm = h.shape[0]
    bd = bd_ref[...]
    dd = functools.partial(jnp.dot, preferred_element_type=F32)
    n_heads = d // HEAD_DIM
    lane = lax.broadcasted_iota(jnp.int32, (tm, LANES), 1)
    low = lane < HEAD_DIM
    aug_even = (lane >= HEAD_DIM) & (lane < HEAD_DIM + AUG_STRIDE)
    aug_odd = lane < AUG_STRIDE
    aug_all = {}

    def forget_gate_stages():
        fl = dd(h, wf_ref[...]) + bf_ref[...]
        yield
        lf = jnp.minimum(fl, 0.0) - jnp.log(1.0 + jnp.exp(-jnp.abs(fl)))
        hi, mid, lo = _split3(lf)
        yield
        tri = tri_ref[...]
        cs = dd(tri, hi) + (dd(tri, mid) + dd(tri, lo))
        yield
        f = cs + carry_ref[0:1, :]
        carry_ref[...] = jnp.broadcast_to(f[tm - 1:tm, :], carry_ref.shape)
        parts = [jnp.where(lane < n_heads, p.astype(F32), 0.0) for p in _split3(f * LOG2E)]
        fpack = parts[0]
        for i in range(1, N_SPLIT):
            fpack = fpack + pltpu.roll(parts[i], i * n_heads, axis=1)
        fpack = fpack.astype(BF16)
        yield
        aug_all["q"] = dd(fpack, pq_ref[...]) + oq_ref[...]
        aug_all["k"] = dd(fpack, pk_ref[...]) + ok_ref[...]

    def head_rms(zp, g):
        ss = _head_sums(zp * zp, bd)
        return zp * lax.rsqrt(ss * (1.0 / HEAD_DIM) + QK_EPS) * g

    def store_augmented(out_ref, zj, tile, aug):
        even = pltpu.roll(aug, (HEAD_DIM - AUG_STRIDE * 2 * tile) % LANES, axis=1)
        odd = pltpu.roll(aug, (-AUG_STRIDE * (2 * tile + 1)) % LANES, axis=1)
        sl = slice(2 * tile * LANES, (2 * tile + 2) * LANES)
        out_ref[0, :, sl] = jnp.concatenate(
            [jnp.where(low, zj, jnp.where(aug_even, even, 0.0)),
             jnp.where(low, jnp.where(aug_odd, odd, 0.0), zj)], axis=1).astype(out_ref.dtype)

    section = {"q": 0, "k": 1, "o": 2}

    def project(kind, c):
        if kind == "v":
            return lax.dot_general(wvt_ref[c * nc:(c + 1) * nc, :], h, (((1,), (1,)), ((), ())),
                                   preferred_element_type=F32)
        col = section[kind] * d + c * nc
        return dd(h, w_ref[:, col:col + nc])

    def finish(kind, c, z):
        if kind == "v":
            vt_ref[0, c * nc:(c + 1) * nc, :] = z.astype(vt_ref.dtype)
        elif kind == "o":
            o_ref[0, :, c * nc:(c + 1) * nc] = jax.nn.sigmoid(z).astype(o_ref.dtype)
        else:
            out_ref, g_ref = (q_ref, qg_ref) if kind == "q" else (k_ref, kg_ref)
            for pr in range(nc // SUM_W):
                zn = head_rms(z[:, pr * SUM_W:(pr + 1) * SUM_W], g_ref[...])
                for j in range(SUM_W // LANES):
                    tile = (c * nc + pr * SUM_W) // LANES + j
                    store_augmented(out_ref, zn[:, j * LANES:(j + 1) * LANES], tile, aug_all[kind])

    jobs = [(kind, c) for kind in ("o", "q", "k", "v") for c in range(d // nc)]
    f_stages = forget_gate_stages()
    z_next = project(*jobs[0])
    for i, job in enumerate(jobs):
        z = z_next
        if i + 1 < len(jobs):
            z_next = project(*jobs[i + 1])
        if job[0] in ("q", "k"):
            for _ in f_stages:
                pass
        finish(*job, z)
        next(f_stages, None)
        next(f_stages, None)


def _fox_in(x, mod, w_qko, w_vt, w_f, b_f, q_g, k_g, tm=512, nc=512):
    b, t, d = x.shape
    n_heads = d // HEAD_DIM
    assert N_SPLIT * n_heads <= LANES
    qg = jnp.tile(q_g * (HEAD_DIM ** -0.5 * LOG2E), SUM_W // HEAD_DIM).reshape(1, SUM_W)
    kg = jnp.tile(k_g, SUM_W // HEAD_DIM).reshape(1, SUM_W)
    logit_bound = HEAD_DIM * jnp.max(jnp.abs(qg)) * jnp.max(jnp.abs(kg))
    bd = _head_sum_matrix()
    tri = (jnp.arange(tm)[:, None] >= jnp.arange(tm)[None, :]).astype(BF16)
    assert AUG_STRIDE * n_heads <= LANES and 2 * N_SPLIT + 1 <= AUG_STRIDE
    pq = np.zeros((LANES, LANES), np.float32)
    pk = np.zeros_like(pq)
    oq = np.zeros((1, LANES), np.float32)
    ok = np.zeros_like(oq)
    shift_lane = np.zeros_like(oq)
    for hd in range(n_heads):
        base = hd * AUG_STRIDE
        for part in range(N_SPLIT):
            pq[part * n_heads + hd, base + N_SPLIT + part] = 1.0
            ok[0, base + N_SPLIT + part] = 1.0
            pk[part * n_heads + hd, base + part] = -1.0
            oq[0, base + part] = 1.0
        shift_lane[0, base + 2 * N_SPLIT] = 1.0
    pq = jnp.asarray(pq, BF16)
    pk = jnp.asarray(pk, BF16)
    oq = jnp.asarray(oq) - logit_bound * jnp.asarray(shift_lane)
    ok = jnp.asarray(ok + shift_lane)
    act = jax.ShapeDtypeStruct((b, t, d), BF16)
    aug = jax.ShapeDtypeStruct((b, t, n_heads * LANES), BF16)
    row_spec = pl.BlockSpec((1, tm, d), lambda i, j: (i, j, 0))
    aug_spec = pl.BlockSpec((1, tm, n_heads * LANES), lambda i, j: (i, j, 0))
    kern = functools.partial(_fox_in_kernel, d=d, nc=nc)
    consts = (w_qko, w_f, b_f, qg, kg, bd, tri, pq, pk, oq, ok, w_vt)
    q_aug, k_aug, og, vt = pl.pallas_call(
        kern,
        grid=(b, t // tm),
        in_specs=[row_spec, pl.BlockSpec((1, 6, d), lambda i, j: (i, 0, 0))]
        + [_const_spec(c.shape) for c in consts],
        out_specs=[aug_spec, aug_spec, row_spec, pl.BlockSpec((1, d, tm), lambda i, j: (i, 0, j))],
        out_shape=[aug, aug, act, jax.ShapeDtypeStruct((b, d, t), BF16)],
        scratch_shapes=[pltpu.VMEM((8, LANES), F32)],
        compiler_params=_params("parallel", "arbitrary"),
        name="fox_in_proj",
    )(x, mod, *consts)
    return q_aug, k_aug, og, vt, logit_bound


def _attn_kernel(q_ref, k_ref, vt_ref, g_ref, y_ref, *, tq, tk):
    qi = pl.program_id(2)
    heads = LANES // HEAD_DIM
    pair = 2 * tk
    key = lax.broadcasted_iota(jnp.int32, (tk, tq), 0)
    qry = lax.broadcasted_iota(jnp.int32, (tk, tq), 1) + qi * tq
    qs = [q_ref[0, :, hh * LANES:(hh + 1) * LANES] for hh in range(heads)]

    def step(j, carry, masked):
        starts = [pl.multiple_of(j * pair + bb * tk, tk) for bb in range(2)]

        def scores(hh, bb):
            z = _dot_nt(k_ref[0, pl.ds(starts[bb], tk), hh * LANES:(hh + 1) * LANES], qs[hh])
            if masked:
                z = jnp.where(key + starts[bb] <= qry, z, NEG_BIG)
            return z

        def weights(z, m):
            p = jnp.exp2(z - m)
            return p, jnp.sum(p, axis=0, keepdims=True)

        def values(hh, bb, p):
            vt = vt_ref[0, hh * HEAD_DIM:(hh + 1) * HEAD_DIM, pl.ds(starts[bb], tk)]
            return jnp.dot(vt, p.astype(BF16), preferred_element_type=F32)

        colmax = lambda z: jnp.max(z, axis=0, keepdims=True)
        (m0, l0, a0), (m1, l1, a1) = carry
        z00 = scores(0, 0)
        z10 = scores(1, 0)
        m0a = jnp.maximum(m0, colmax(z00))
        z01 = scores(0, 1)
        p00, s00 = weights(z00, m0a)
        m1a = jnp.maximum(m1, colmax(z10))
        z11 = scores(1, 1)
        v00 = values(0, 0, p00)
        p10, s10 = weights(z10, m1a)
        m0b = jnp.maximum(m0a, colmax(z01))
        v10 = values(1, 0, p10)
        p01, s01 = weights(z01, m0b)
        m1b = jnp.maximum(m1a, colmax(z11))
        v01 = values(0, 1, p01)
        p11, s11 = weights(z11, m1b)
        v11 = values(1, 1, p11)

        def merge(m, l, a, ma, mb, sa, sb, va, vb):
            ra = jnp.exp2(m - ma)
            rb = jnp.exp2(ma - mb)
            return mb, rb * (ra * l + sa) + sb, rb * (ra * a + va) + vb

        return (merge(m0, l0, a0, m0a, m0b, s00, s01, v00, v01),
                merge(m1, l1, a1, m1a, m1b, s10, s11, v10, v11))

    init = tuple((jnp.full((1, tq), NEG_BIG, F32), jnp.zeros((1, tq), F32),
                  jnp.zeros((HEAD_DIM, tq), F32)) for _ in range(heads))
    n_full = (qi * tq) // pair
    carry = lax.fori_loop(0, n_full, functools.partial(step, masked=False), init)
    for jm in range(tq // pair):
        carry = step(n_full + jm, carry, True)
    yt = jnp.concatenate([carry[hh][2] / carry[hh][1] for hh in range(heads)], axis=0)
    y_ref[0] = (yt.T * g_ref[0].astype(F32)).astype(y_ref.dtype)


def _attn_bounded_kernel(q_ref, k_ref, vt_ref, g_ref, y_ref, *, tq, q_parts=2, step_blocks=(2, 1)):
    qi = pl.program_id(2)
    heads = LANES // HEAD_DIM
    part = tq // q_parts
    slots = [(hh, pi) for hh in range(heads) for pi in range(q_parts)]

    def run_chains(chains, carry):
        def scores(hh, k0, nk, pi, q0, nq, masked):
            row = pi * part + q0
            z = _dot_nt(k_ref[0, pl.ds(k0, nk), hh * LANES:(hh + 1) * LANES],
                        q_ref[0, row:row + nq, hh * LANES:(hh + 1) * LANES])
            if masked:
                key = lax.broadcasted_iota(jnp.int32, (nk, nq), 0) + k0
                qry = lax.broadcasted_iota(jnp.int32, (nk, nq), 1) + (qi * tq + row)
                z = jnp.where(key <= qry, z, NEG_BIG)
            return z

        sums = {s: [] for s in slots}
        vals = {s: [] for s in slots}
        ahead = [scores(*ch) for ch in chains[:LOOKAHEAD]]
        for c, (hh, k0, nk, pi, q0, nq, _) in enumerate(chains):
            z = ahead.pop(0)
            if c + LOOKAHEAD < len(chains):
                ahead.append(scores(*chains[c + LOOKAHEAD]))
            p = jnp.exp2(z)
            vt = vt_ref[0, hh * HEAD_DIM:(hh + 1) * HEAD_DIM, pl.ds(k0, nk)]
            s = jnp.sum(p, axis=0, keepdims=True)
            v = jnp.dot(vt, p.astype(BF16), preferred_element_type=F32)
            if nq != part:
                s = jnp.concatenate([jnp.zeros((1, q0), F32), s], axis=1)
                v = jnp.concatenate([jnp.zeros((HEAD_DIM, q0), F32), v], axis=1)
            sums[(hh, pi)].append(s)
            vals[(hh, pi)].append(v)
        out = []
        for n, s in enumerate(slots):
            l, acc = carry[n]
            if sums[s]:
                l = l + functools.reduce(jnp.add, sums[s])
                acc = acc + functools.reduce(jnp.add, vals[s])
            out.append((l, acc))
        return tuple(out)

    def full_step(j, carry, nb, base):
        starts = [pl.multiple_of(base + j * (nb * part) + bb * part, part) for bb in range(nb)]
        return run_chains([(hh, starts[bb], part, pi, 0, part, False)
                           for bb in range(nb) for (hh, pi) in slots], carry)

    carry = tuple((jnp.zeros((1, part), F32), jnp.zeros((HEAD_DIM, part), F32)) for _ in slots)
    done = 0
    for nb in step_blocks:
        n_steps = (qi * tq - done) // (nb * part)
        carry = lax.fori_loop(0, n_steps, functools.partial(full_step, nb=nb, base=done), carry)
        done = done + n_steps * (nb * part)
    half = part // 2
    chains = []
    for kb in range(q_parts):
        k0 = pl.multiple_of(qi * tq + kb * part, part)
        for hh in range(heads):
            chains.append((hh, k0, half, kb, 0, part, True))
            chains.append((hh, pl.multiple_of(k0 + half, half), half, kb, half, half, True))
            chains += [(hh, k0, part, pi, 0, part, False) for pi in range(kb + 1, q_parts)]
    carry = run_chains(chains, carry)
    yt = jnp.concatenate(
        [jnp.concatenate([carry[slots.index((hh, pi))][1] / carry[slots.index((hh, pi))][0]
                          for hh in range(heads)], axis=0) for pi in range(q_parts)], axis=1)
    y_ref[0] = (yt.T * g_ref[0].astype(F32)).astype(y_ref.dtype)


def _fox_attention(q_aug, k_aug, vt, og, logit_bound, tq=1024, q_parts=2):
    part = tq // q_parts
    return lax.cond(
        2.0 * logit_bound <= SAFE_LOGIT_RANGE,
        functools.partial(_attention_call, tq=tq, name="fox_attention_bounded",
                          kernel_fn=functools.partial(_attn_bounded_kernel, tq=tq, q_parts=q_parts)),
        functools.partial(_attention_call, tq=part, name="fox_attention",
                          kernel_fn=functools.partial(_attn_kernel, tq=part, tk=part // 2)),
        q_aug, k_aug, vt, og)


def _attention_call(q_aug, k_aug, vt, og, *, kernel_fn, tq, name):
    b, d, t = vt.shape
    heads = LANES // HEAD_DIM
    blk = pl.BlockSpec((1, tq, LANES), lambda i, p, j: (i, j, p))
    q_blk = pl.BlockSpec((1, tq, heads * LANES), lambda i, p, j: (i, j, p))
    k_full = pl.BlockSpec((1, t, heads * LANES), lambda i, p, j: (i, 0, p))
    vt_full = pl.BlockSpec((1, LANES, t), lambda i, p, j: (i, p, 0))
    return pl.pallas_call(
        kernel_fn,
        grid=(b, d // LANES, t // tq),
        in_specs=[q_blk, k_full, vt_full, blk],
        out_specs=blk,
        out_shape=jax.ShapeDtypeStruct((b, t, d), BF16),
        compiler_params=_params("parallel", "parallel", "arbitrary"),
        name=name,
    )(q_aug, k_aug, vt, og)


def _proj_ln_kernel(a_ref, x_ref, mod_ref, w_ref, lng_ref, lnb_ref, o_ref):
    y = jnp.dot(a_ref[0], w_ref[...], preferred_element_type=F32)
    z = DEEPNORM_ALPHA * x_ref[0] + mod_ref[0][2:3] * y
    o_ref[0] = _layer_norm(z, lng_ref[...], lnb_ref[...])


def _proj_ln(a, x, mod, w, ln_g, ln_b, tm=512):
    b, t, d = x.shape
    row_spec = pl.BlockSpec((1, tm, d), lambda i, j: (i, j, 0))
    return pl.pallas_call(
        _proj_ln_kernel,
        grid=(b, t // tm),
        in_specs=[row_spec, row_spec,
                  pl.BlockSpec((1, 6, d), lambda i, j: (i, 0, 0)),
                  _const_spec(w.shape), _const_spec((1, d)), _const_spec((1, d))],
        out_specs=row_spec,
        out_shape=jax.ShapeDtypeStruct((b, t, d), F32),
        compiler_params=_params("parallel", "parallel"),
        name="attn_out_proj_ln",
    )(a, x, mod, w, ln_g.reshape(1, d), ln_b.reshape(1, d))


def _ffn_kernel(x_ref, mod_ref, win_ref, wout_ref, lng_ref, lnb_ref, o_ref, act_ref, *, d_ff, fc):
    x = x_ref[0]
    mod = mod_ref[0]
    h = (x * (1.0 + mod[4:5]) + mod[3:4]).astype(BF16)
    for c in range(d_ff // fc):
        g = jnp.dot(h, win_ref[:, c * fc:(c + 1) * fc], preferred_element_type=F32)
        u = jnp.dot(h, win_ref[:, d_ff + c * fc:d_ff + (c + 1) * fc], preferred_element_type=F32)
        act_ref[:, c * fc:(c + 1) * fc] = (g * jax.nn.sigmoid(g) * u).astype(BF16)
    y = jnp.dot(act_ref[...], wout_ref[...], preferred_element_type=F32)
    z = DEEPNORM_ALPHA * x + mod[5:6] * y
    o_ref[0] = _layer_norm(z, lng_ref[...], lnb_ref[...])


def _ffn(x, mod, w_in, w_out, ln_g, ln_b, tm=512, fc=256):
    b, t, d = x.shape
    d_ff = w_out.shape[0]
    row_spec = pl.BlockSpec((1, tm, d), lambda i, j: (i, j, 0))
    return pl.pallas_call(
        functools.partial(_ffn_kernel, d_ff=d_ff, fc=fc),
        grid=(b, t // tm),
        in_specs=[row_spec,
                  pl.BlockSpec((1, 6, d), lambda i, j: (i, 0, 0)),
                  _const_spec(w_in.shape), _const_spec(w_out.shape),
                  _const_spec((1, d)), _const_spec((1, d))],
        out_specs=row_spec,
        out_shape=jax.ShapeDtypeStruct((b, t, d), F32),
        scratch_shapes=[pltpu.VMEM((tm, d_ff), BF16)],
        compiler_params=_params("parallel", "parallel"),
        name="swiglu_ln",
    )(x, mod, w_in, w_out, ln_g.reshape(1, d), ln_b.reshape(1, d))


def _rwkv_in_kernel(x_ref, xp_ref, mod_ref, mu_ref, wrkv_ref, w1_ref, w2_ref, a1_ref, a2_ref,
                    g1_ref, g2_ref, vec_ref, bd_ref,
                    r_ref, lw_ref, k_ref, v_ref, kk_ref, a_ref, g_ref, *, d):
    t = pl.program_id(1)
    mod = mod_ref[0]
    sc = 1.0 + mod[1:2]
    sh = mod[0:1]
    h = x_ref[0] * sc + sh
    tm = h.shape[0]
    prev = xp_ref[0][7:8, :] * sc + sh
    prev = jnp.where(t == 0, jnp.zeros_like(prev), prev)
    rows = lax.broadcasted_iota(jnp.int32, h.shape, 0)
    hprev = jnp.where(rows == 0, prev, pltpu.roll(h, 1, axis=0))
    dx = hprev - h
    mu = mu_ref[...]
    vec = vec_ref[...]
    w0, a0, k_k, k_a = vec[0:1], vec[1:2], vec[2:3], vec[3:4]

    def mix(n):
        return (h + dx * mu[n:n + 1]).astype(BF16)

    dd = functools.partial(jnp.dot, preferred_element_type=F32)
    t_w = dd(mix(3), w1_ref[...])
    t_a = dd(mix(4), a1_ref[...])
    t_g = dd(mix(5), g1_ref[...])
    k = dd(mix(1), wrkv_ref[1])
    a = jax.nn.sigmoid(a0 + dd(t_a.astype(BF16), a2_ref[...]))
    a_ref[0] = a.astype(a_ref.dtype)
    r = dd(mix(0), wrkv_ref[0])
    kk = k * k_k
    k_ref[0] = (k * (1.0 + (a - 1.0) * k_a)).astype(k_ref.dtype)
    bd = bd_ref[...]
    for j in range(d // SUM_W):
        sl = slice(j * SUM_W, (j + 1) * SUM_W)
        kkj = kk[:, sl]
        ss = _head_sums(kkj * kkj, bd)
        kk_ref[0, :, sl] = (kkj * jnp.minimum(lax.rsqrt(ss), 1e12)).astype(kk_ref.dtype)
    v = dd(mix(2), wrkv_ref[2])
    r_ref[0] = r.astype(r_ref.dtype)
    ww = w0 + dd(jnp.tanh(t_w).astype(BF16), w2_ref[...])
    lw_ref[0] = DECAY_SCALE * jax.nn.sigmoid(ww)
    g_ref[0] = dd(jax.nn.sigmoid(t_g).astype(BF16), g2_ref[...]).astype(g_ref.dtype)
    v_ref[0] = v.astype(v_ref.dtype)


def _rwkv_in(x, mod, mu, w_rkv, w1, w2, a1, a2, g1, g2, vec, tm=512):
    b, t, d = x.shape
    bd = _head_sum_matrix()
    row_spec = pl.BlockSpec((1, tm, d), lambda i, j: (i, j, 0))
    prev_spec = pl.BlockSpec((1, 8, d), lambda i, j: (i, jnp.maximum(j * (tm // 8) - 1, 0), 0))
    act = jax.ShapeDtypeStruct((b, t, d), BF16)
    consts = (mu, w_rkv, w1, w2, a1, a2, g1, g2, vec, bd)
    return pl.pallas_call(
        functools.partial(_rwkv_in_kernel, d=d),
        grid=(b, t // tm),
        in_specs=[row_spec, prev_spec, pl.BlockSpec((1, 6, d), lambda i, j: (i, 0, 0))]
        + [_const_spec(c.shape) for c in consts],
        out_specs=[row_spec] * 7,
        out_shape=[act, jax.ShapeDtypeStruct((b, t, d), F32), act, act, act, act, act],
        compiler_params=_params("parallel", "parallel"),
        name="rwkv_in_proj",
    )(x, x, mod, *consts)


def _rwkv_scan_kernel(r_ref, lw_ref, k_ref, v_ref, kk_ref, a_ref, lev_ref, tri_ref,
                      y_ref, h_ref, yi_ref, rq_ref, m_ref, n_ref, *, groups, steps_per_seq):
    s = pl.program_id(0)
    gr = 2 * CHUNK
    st = 2 * gr
    n_lev = CHUNK.bit_length() - 1

    @pl.when(s == 0)
    def _():
        h_ref[...] = jnp.zeros_like(h_ref)
        yi_ref[...] = jnp.zeros_like(yi_ref)
        rq_ref[...] = jnp.zeros_like(rq_ref)
        m_ref[...] = jnp.zeros_like(m_ref)
        n_ref[...] = jnp.zeros_like(n_ref)

    first_of_seq = lax.rem(jnp.maximum(s - 1, 0), steps_per_seq) == 0
    state = [jnp.where(first_of_seq, 0.0, h_ref[...])]
    pending = list(range(2 * groups))

    def state_steps(count):
        for _ in range(count):
            if pending:
                i = pending.pop(0)
                lo = i * CHUNK
                h = state[0]
                y_ref[0, lo:lo + CHUNK, :] = (yi_ref[lo:lo + CHUNK, :]
                                              + _dot(rq_ref[lo:lo + CHUNK, :], h))
                state[0] = _dot3(m_ref[i], h) + n_ref[i]

    lane = lax.broadcasted_iota(jnp.int32, (gr, LANES), 1)
    head0 = lane < HEAD_DIM
    lev = lev_ref[...]
    tri = tri_ref[...]
    strict = lev >= 0
    srow = lax.broadcasted_iota(jnp.int32, (st, st), 0)
    scol = lax.broadcasted_iota(jnp.int32, (st, st), 1)
    eye_st = srow == scol
    incl = strict | eye_st
    hrow = lax.broadcasted_iota(jnp.int32, (LANES, LANES), 0)
    hcol = lax.broadcasted_iota(jnp.int32, (LANES, LANES), 1)
    same_head = (hrow < HEAD_DIM) == (hcol < HEAD_DIM)
    eye_h = hrow == hcol

    def stack(x):
        zero = jnp.zeros_like(x)
        return jnp.concatenate([jnp.where(head0, x, zero), jnp.where(head0, zero, x)], axis=0)

    def unstack(x):
        return x[:gr] + x[gr:]

    def log_decay_cumsum(g):
        hi, mid, lo = _split3(lw_ref[0, g * gr:(g + 1) * gr, :])
        dd = functools.partial(jnp.dot, preferred_element_type=F32)
        return dd(tri, hi) + (dd(tri, mid) + dd(tri, lo))

    def prep(g, cum):
        rs = slice(g * gr, (g + 1) * gr)
        lw = lw_ref[0, rs, :]
        clast = jnp.concatenate(
            [jnp.broadcast_to(cum[(j + 1) * CHUNK - 1:(j + 1) * CHUNK], (CHUNK, LANES))
             for j in range(2)], axis=0)
        g_inv = jnp.exp(-cum)
        g_rem = jnp.exp(clast - cum)
        r = r_ref[0, rs, :].astype(F32)
        k = k_ref[0, rs, :].astype(F32)
        v = v_ref[0, rs, :].astype(F32)
        kk = kk_ref[0, rs, :].astype(F32)
        bt = kk * a_ref[0, rs, :].astype(F32)
        a_st = stack(-kk * jnp.exp(cum - lw))
        r_st = stack(r * jnp.exp(cum))
        b_st = stack(bt * g_inv)
        k_st = stack(k * g_inv)
        prods = []
        for hd in range(2):
            hs = slice(hd * gr, (hd + 1) * gr)
            prods.append(_dot_nt(jnp.concatenate([a_st[hs], r_st[hs]], axis=0),
                                 jnp.concatenate([b_st[hs], k_st[hs]], axis=0)))
        zero = jnp.zeros((gr, gr), F32)

        def by_head(rows, cols):
            return jnp.concatenate(
                [jnp.concatenate([prods[0][rows, cols], zero], axis=1),
                 jnp.concatenate([zero, prods[1][rows, cols]], axis=1)], axis=0)

        top, bot = slice(0, gr), slice(gr, st)
        return dict(
            v=v, v_st=stack(v), a_st=a_st, r_st=r_st, b_rem=bt * g_rem, k_rem=k * g_rem,
            g_last=jnp.exp(clast),
            a_ab=jnp.where(strict, by_head(top, top), 0.0),
            a_ak=jnp.where(strict, by_head(top, bot), 0.0),
            a_rb=jnp.where(incl, by_head(bot, top), 0.0),
            a_rk=jnp.where(incl, by_head(bot, bot), 0.0))

    def lower_rows(m, bsz):
        return jnp.concatenate([m[i + bsz:i + 2 * bsz] for i in range(0, st, 2 * bsz)], axis=0)

    def scatter_lower(full, low, bsz):
        parts = []
        for n, i in enumerate(range(0, st, 2 * bsz)):
            parts += [full[i:i + bsz], low[n * bsz:(n + 1) * bsz]]
        return jnp.concatenate(parts, axis=0)

    def precompute(gs):
        cums = {g: log_decay_cumsum(g) for g in gs}
        yield
        ps = {g: prep(g, cums[g]) for g in gs}
        yield
        xs = {g: jnp.where(eye_st, 1.0, jnp.where(lev == 0, ps[g]["a_ab"], 0.0)) for g in gs}
        for level in range(1, n_lev):
            bsz = 1 << level
            a_off = {g: jnp.where(lev == level, ps[g]["a_ab"], 0.0) for g in gs}
            if bsz % 8:
                ws = {g: _dot(a_off[g], xs[g]) for g in gs}
                yield
                xs = {g: xs[g] + _dot(xs[g], ws[g]) for g in gs}
            else:
                w_low = {g: _dot(lower_rows(a_off[g], bsz), xs[g]) for g in gs}
                yield
                zero = jnp.zeros((st, st), F32)
                upd = {g: _dot(lower_rows(xs[g], bsz), scatter_lower(zero, w_low[g], bsz))
                       for g in gs}
                xs = {g: scatter_lower(xs[g], lower_rows(xs[g], bsz) + upd[g], bsz) for g in gs}
            yield
        gm = {g: jnp.concatenate([_dot(ps[g]["a_ak"], ps[g]["v_st"]), ps[g]["a_st"]], axis=1)
              for g in gs}
        yield
        tg = {g: _dot(xs[g], gm[g]) for g in gs}
        yield
        rb = {g: _dot(ps[g]["a_rb"], tg[g]) for g in gs}
        rk = {g: _dot(ps[g]["a_rk"], ps[g]["v_st"]) for g in gs}
        yield
        assert not pending
        for g in gs:
            p = ps[g]
            yi_ref[g * gr:(g + 1) * gr, :] = unstack(rb[g][:, :LANES] + rk[g])
            rq_ref[g * gr:(g + 1) * gr, :] = unstack(p["r_st"] + rb[g][:, LANES:])
            uv = unstack(tg[g][:, :LANES])
            wa = unstack(tg[g][:, LANES:])
            for j in range(2):
                cs = slice(j * CHUNK, (j + 1) * CHUNK)
                lhs_t = jnp.concatenate([p["b_rem"][cs], p["k_rem"][cs]], axis=0)
                rhs_t = jnp.concatenate(
                    [jnp.concatenate([uv[cs], wa[cs]], axis=1),
                     jnp.concatenate([p["v"][cs], jnp.zeros_like(p["v"][cs])], axis=1)], axis=0)
                nm = _dot_tn(lhs_t, rhs_t)
                n_ref[2 * g + j] = jnp.where(same_head, nm[:, :LANES], 0.0)
                decay = jnp.broadcast_to(p["g_last"][j * CHUNK:j * CHUNK + 1], (LANES, LANES))
                m_ref[2 * g + j] = (jnp.where(same_head, nm[:, LANES:], 0.0)
                                    + jnp.where(eye_h, decay, 0.0))

    n_stages = 2 * n_lev + 3
    for stage, _ in enumerate(precompute(range(groups))):
        state_steps(-(-len(pending) // (n_stages - stage)))
        if not pending and state:
            h_ref[...] = state.pop()


def _rwkv_scan(r, lw, k, v, kk, a, groups=2):
    b, t, d = r.shape
    gr = 2 * CHUNK
    st = 2 * gr
    rows = gr * groups
    n_hp = d // LANES
    steps_per_seq = t // rows
    n_blocks = b * n_hp * steps_per_seq

    def block_index(blk):
        seq = blk // steps_per_seq
        return seq // n_hp, blk % steps_per_seq, seq % n_hp

    in_blk = pl.BlockSpec((1, rows, LANES), lambda s: block_index(jnp.minimum(s, n_blocks - 1)))
    out_blk = pl.BlockSpec((1, rows, LANES), lambda s: block_index(jnp.maximum(s - 1, 0)))
    idx = jnp.arange(st)
    xor = idx[:, None] ^ idx[None, :]
    same = (idx[:, None] // CHUNK) == (idx[None, :] // CHUNK)
    lower = idx[None, :] < idx[:, None]
    msb = jnp.floor(jnp.log2(jnp.maximum(xor, 1).astype(F32))).astype(jnp.int32)
    lev = jnp.where(same & lower, msb, -1).astype(jnp.int32)
    ti = jnp.arange(gr)
    tri = ((ti[:, None] >= ti[None, :]) & ((ti[:, None] // CHUNK) == (ti[None, :] // CHUNK))).astype(BF16)
    return pl.pallas_call(
        functools.partial(_rwkv_scan_kernel, groups=groups, steps_per_seq=steps_per_seq),
        grid=(n_blocks + 1,),
        in_specs=[in_blk] * 6 + [_const_spec(lev.shape), _const_spec(tri.shape)],
        out_specs=out_blk,
        out_shape=jax.ShapeDtypeStruct((b, t, d), F32),
        scratch_shapes=[pltpu.VMEM((LANES, LANES), F32),
                        pltpu.VMEM((rows, LANES), F32), pltpu.VMEM((rows, LANES), F32),
                        pltpu.VMEM((2 * groups, LANES, LANES), F32),
                        pltpu.VMEM((2 * groups, LANES, LANES), F32)],
        compiler_params=_params("arbitrary"),
        name="rwkv7_chunk_scan",
    )(r, lw, k, v, kk, a, lev, tri)


def _rwkv_out_kernel(y_ref, r_ref, k_ref, v_ref, g_ref, x_ref, mod_ref, vec_ref, bd_ref, w_ref,
                     lng_ref, lnb_ref, o_ref, act_ref, *, d):
    bd = bd_ref[...]
    vec = vec_ref[...]
    inv = 1.0 / HEAD_DIM
    for j in range(d // SUM_W):
        sl = slice(j * SUM_W, (j + 1) * SUM_W)
        y = y_ref[0, :, sl]
        mean = _dot_exact_rhs(y, bd) * inv
        yc = y - mean
        var = _head_sums(yc * yc, bd) * inv
        yn = yc * lax.rsqrt(var + GN_EPS) * vec[0:1, sl] + vec[1:2, sl]
        r = r_ref[0, :, sl].astype(F32)
        k = k_ref[0, :, sl].astype(F32)
        bonus = _dot_exact_rhs(r * k * vec[2:3, sl], bd) * v_ref[0, :, sl].astype(F32)
        act_ref[:, sl] = ((yn + bonus) * g_ref[0, :, sl].astype(F32)).astype(BF16)
    out = jnp.dot(act_ref[...], w_ref[...], preferred_element_type=F32)
    z = DEEPNORM_ALPHA * x_ref[0] + mod_ref[0][2:3] * out
    o_ref[0] = _layer_norm(z, lng_ref[...], lnb_ref[...])


def _rwkv_out(y, r, k, v, g, x, mod, vec, w, ln_g, ln_b, tm=512):
    b, t, d = x.shape
    bd = _head_sum_matrix()
    row_spec = pl.BlockSpec((1, tm, d), lambda i, j: (i, j, 0))
    return pl.pallas_call(
        functools.partial(_rwkv_out_kernel, d=d),
        grid=(b, t // tm),
        in_specs=[row_spec] * 6
        + [pl.BlockSpec((1, 6, d), lambda i, j: (i, 0, 0)),
           _const_spec(vec.shape), _const_spec(bd.shape), _const_spec(w.shape),
           _const_spec((1, d)), _const_spec((1, d))],
        out_specs=row_spec,
        out_shape=jax.ShapeDtypeStruct((b, t, d), F32),
        scratch_shapes=[pltpu.VMEM((tm, d), BF16)],
        compiler_params=_params("parallel", "parallel"),
        name="rwkv_out_proj_ln",
    )(y, r, k, v, g, x, mod, vec, bd, w, ln_g.reshape(1, d), ln_b.reshape(1, d))


def _trunk(x, c, ada_w, ada_b, ln_g, ln_b, ffn_w_in, ffn_w_out, fox_w_in, fox_b_f, fox_q_g, fox_k_g, fox_w_o, rwkv_mu, rwkv_w_rkv, rwkv_w0, rwkv_w1, rwkv_w2, rwkv_a0, rwkv_a1, rwkv_a2, rwkv_g1, rwkv_g2, rwkv_k_k, rwkv_k_a, rwkv_r_k, rwkv_lnx_g, rwkv_lnx_b, rwkv_w_o,
           *, tm, tm_rwkv, tq, q_parts, groups, tn, nc, fc):
    b, t, d = x.shape
    n_heads = d // HEAD_DIM
    mods = _mods(c, ada_w, ada_b, tn=tn)

    w_in = fox_w_in[0]
    f_lo = 3 * d
    w_qko = jnp.concatenate([w_in[:, :2 * d], w_in[:, f_lo + n_heads:]], axis=1).astype(BF16)
    w_vt = w_in[:, 2 * d:f_lo].T.astype(BF16)
    w_f = jnp.pad(w_in[:, f_lo:f_lo + n_heads], ((0, 0), (0, LANES - n_heads))).astype(BF16)
    b_f = jnp.pad(fox_b_f[0], (0, LANES - n_heads)).reshape(1, LANES)
    q, k, og, v, logit_bound = _fox_in(x, mods[0], w_qko, w_vt, w_f, b_f, fox_q_g[0], fox_k_g[0],
                                       tm=tm, nc=nc)
    att = _fox_attention(q, k, v, og, logit_bound, tq=tq, q_parts=q_parts)
    x = _proj_ln(att, x, mods[0], fox_w_o[0].astype(BF16), ln_g[0, 0], ln_b[0, 0], tm=tm)
    x = _ffn(x, mods[0], ffn_w_in[0].astype(BF16), ffn_w_out[0].astype(BF16), ln_g[0, 1], ln_b[0, 1],
             tm=tm, fc=fc)

    vec_in = jnp.stack([rwkv_w0[0], rwkv_a0[0], rwkv_k_k[0], rwkv_k_a[0]])
    r, lw, k, v, kk, a, g = _rwkv_in(
        x, mods[1], rwkv_mu[0], rwkv_w_rkv[0].astype(BF16),
        rwkv_w1[0].astype(BF16), rwkv_w2[0].astype(BF16),
        rwkv_a1[0].astype(BF16), rwkv_a2[0].astype(BF16),
        rwkv_g1[0].astype(BF16), rwkv_g2[0].astype(BF16), vec_in, tm=tm_rwkv)
    y = _rwkv_scan(r, lw, k, v, kk, a, groups=groups)
    vec_out = jnp.stack([rwkv_lnx_g[0], rwkv_lnx_b[0], rwkv_r_k[0].reshape(d)])
    x = _rwkv_out(y, r, k, v, g, x, mods[1], vec_out, rwkv_w_o[0].astype(BF16), ln_g[1, 0], ln_b[1, 0],
                  tm=tm)
    x = _ffn(x, mods[1], ffn_w_in[1].astype(BF16), ffn_w_out[1].astype(BF16), ln_g[1, 1], ln_b[1, 1],
             tm=tm, fc=fc)
    return x


def kernel(x, c, ada_w, ada_b, ln_g, ln_b, ffn_w_in, ffn_w_out, fox_w_in, fox_b_f, fox_q_g, fox_k_g, fox_w_o, rwkv_mu, rwkv_w_rkv, rwkv_w0, rwkv_w1, rwkv_w2, rwkv_a0, rwkv_a1, rwkv_a2, rwkv_g1, rwkv_g2, rwkv_k_k, rwkv_k_a, rwkv_r_k, rwkv_lnx_g, rwkv_lnx_b, rwkv_w_o):
    return _trunk(x, c, ada_w, ada_b, ln_g, ln_b, ffn_w_in, ffn_w_out, fox_w_in, fox_b_f, fox_q_g, fox_k_g, fox_w_o, rwkv_mu, rwkv_w_rkv, rwkv_w0, rwkv_w1, rwkv_w2, rwkv_a0, rwkv_a1, rwkv_a2, rwkv_g1, rwkv_g2, rwkv_k_k, rwkv_k_a, rwkv_r_k, rwkv_lnx_g, rwkv_lnx_b, rwkv_w_o,
                  tm=512, tm_rwkv=512, tq=2048, q_parts=4, groups=8, tn=1536, nc=512, fc=256)
```

```python
import functools

import jax
import jax.numpy as jnp
import numpy as np
from jax import lax
from jax.experimental import pallas as pl
from jax.experimental.pallas import tpu as pltpu

F32 = jnp.float32
BF16 = jnp.bfloat16

HEAD_DIM = 64
DEPTH = 2
DEEPNORM_ALPHA = (2 * DEPTH) ** 0.25
LN_EPS = 1e-5
QK_EPS = 1e-6
GN_EPS = HEAD_DIM * 1e-5
LANES = 128
SUM_W = 256
CHUNK = 64
NEG_BIG = -1e30
LOG2E = 1.4426950408889634
DECAY_SCALE = -0.6065306597126334
SAFE_LOGIT_RANGE = 96.0
N_SPLIT = 3
LOOKAHEAD = 2
AUG_STRIDE = 8
VMEM_LIMIT = 56 * 1024 * 1024


def _dot(a, b):
    return jnp.dot(a.astype(BF16), b.astype(BF16), preferred_element_type=F32)


def _dot_nt(a, b):
    return lax.dot_general(a.astype(BF16), b.astype(BF16), (((1,), (1,)), ((), ())),
                           preferred_element_type=F32)


def _dot_tn(a, b):
    return lax.dot_general(a.astype(BF16), b.astype(BF16), (((0,), (0,)), ((), ())),
                           preferred_element_type=F32)


def _split2(a):
    hi = a.astype(BF16)
    lo = (a - hi.astype(F32)).astype(BF16)
    return hi, lo


def _split3(a):
    hi = a.astype(BF16)
    r1 = a - hi.astype(F32)
    mid = r1.astype(BF16)
    lo = (r1 - mid.astype(F32)).astype(BF16)
    return hi, mid, lo


def _dot3(a, b):
    ah, al = _split2(a)
    bh, bl = _split2(b)
    d = functools.partial(jnp.dot, preferred_element_type=F32)
    return d(ah, bh) + (d(al, bh) + d(ah, bl))


def _dot_exact_rhs(a, b_exact):
    ah, al = _split2(a)
    d = functools.partial(jnp.dot, preferred_element_type=F32)
    return d(ah, b_exact) + d(al, b_exact)


def _head_sum_matrix():
    idx = jnp.arange(SUM_W) // HEAD_DIM
    return (idx[:, None] == idx[None, :]).astype(BF16)


def _layer_norm(z, g, b):
    mu = jnp.mean(z, axis=-1, keepdims=True)
    zc = z - mu
    var = jnp.mean(zc * zc, axis=-1, keepdims=True)
    return zc * lax.rsqrt(var + LN_EPS) * g + b


def _head_sums(a, head_sum_matrix):
    return jnp.dot(a.astype(BF16), head_sum_matrix, preferred_element_type=F32)


def _params(*sem):
    return pltpu.CompilerParams(dimension_semantics=sem, vmem_limit_bytes=VMEM_LIMIT)


def _const_spec(shape):
    nd = len(shape)
    return pl.BlockSpec(shape, lambda *_: (0,) * nd, pipeline_mode=pl.Buffered(1))


def _mods_kernel(c_ref, w_ref, b_ref, o_ref):
    c = c_ref[...]
    ca = c * jax.nn.sigmoid(c)
    o_ref[0] = _dot3(ca, w_ref[0]) + b_ref[0]


def _mods(c, ada_w, ada_b, tn=1536):
    depth, d, n = ada_w.shape
    b = c.shape[0]
    rows = 8
    cp = jnp.pad(c, ((0, rows - b), (0, 0)))
    out = pl.pallas_call(
        _mods_kernel,
        grid=(depth, n // tn),
        in_specs=[pl.BlockSpec((rows, d), lambda l, j: (0, 0)),
                  pl.BlockSpec((1, d, tn), lambda l, j: (l, 0, j)),
                  pl.BlockSpec((1, 1, tn), lambda l, j: (l, 0, j))],
        out_specs=pl.BlockSpec((1, rows, tn), lambda l, j: (l, 0, j)),
        out_shape=jax.ShapeDtypeStruct((depth, rows, n), F32),
        compiler_params=_params("parallel", "parallel"),
        name="adaln_mods",
    )(cp, ada_w, ada_b.reshape(depth, 1, n))
    return out[:, :b].reshape(depth, b, 6, d)


def _fox_in_kernel(x_ref, mod_ref, wqk_ref, wo_ref, wf_ref, bf_ref, qg_ref, kg_ref, bd_ref, tri_ref,
                   pq_ref, pk_ref, oq_ref, ok_ref, wvt_ref,
                   q_ref, k_ref, o_ref, vt_ref, carry_ref, *, d, nc):
    t = pl.program_id(1)

    @pl.when(t == 0)
    def _():
        carry_ref[...] = jnp.zeros_like(carry_ref)

    x = x_ref[0]
    mod = mod_ref[0]
    h = (x * (1.0 + mod[1:2]) + mod[0:1]).astype(BF16)
    tm = h.shape[0]
    bd = bd_ref[...]
    dd = functools.partial(jnp.dot, preferred_element_type=F32)
    n_heads = d // HEAD_DIM
    lane = lax.broadcasted_iota(jnp.int32, (tm, LANES), 1)
    low = lane < HEAD_DIM
    aug_even = (lane >= HEAD_DIM) & (lane < HEAD_DIM + AUG_STRIDE)
    aug_odd = lane < AUG_STRIDE
    aug_all = {}

    def forget_gate_stages():
        fl = dd(h, wf_ref[...]) + bf_ref[...]
        yield
        lf = jnp.minimum(fl, 0.0) - jnp.log(1.0 + jnp.exp(-jnp.abs(fl)))
        hi, mid, lo = _split3(lf)
        yield
        tri = tri_ref[...]
        cs = dd(tri, hi) + (dd(tri, mid) + dd(tri, lo))
        yield
        f = cs + carry_ref[0:1, :]
        carry_ref[...] = jnp.broadcast_to(f[tm - 1:tm, :], carry_ref.shape)
        parts = [jnp.where(lane < n_heads, p.astype(F32), 0.0) for p in _split3(f * LOG2E)]
        fpack = parts[0]
        for i in range(1, N_SPLIT):
            fpack = fpack + pltpu.roll(parts[i], i * n_heads, axis=1)
        fpack = fpack.astype(BF16)
        yield
        aug_all["q"] = dd(fpack, pq_ref[...]) + oq_ref[...]
        aug_all["k"] = dd(fpack, pk_ref[...]) + ok_ref[...]

    def head_rms(zp, g):
        ss = _head_sums(zp * zp, bd)
        return zp * lax.rsqrt(ss * (1.0 / HEAD_DIM) + QK_EPS) * g

    def store_augmented(out_ref, zj, tile, aug):
        even = pltpu.roll(aug, (HEAD_DIM - AUG_STRIDE * 2 * tile) % LANES, axis=1)
        odd = pltpu.roll(aug, (-AUG_STRIDE * (2 * tile + 1)) % LANES, axis=1)
        sl = slice(2 * tile * LANES, (2 * tile + 2) * LANES)
        out_ref[0, :, sl] = jnp.concatenate(
            [jnp.where(low, zj, jnp.where(aug_even, even, 0.0)),
             jnp.where(low, jnp.where(aug_odd, odd, 0.0), zj)], axis=1).astype(out_ref.dtype)

    def project(kind, c):
        if kind == "v":
            return lax.dot_general(wvt_ref[c * nc:(c + 1) * nc, :], h, (((1,), (1,)), ((), ())),
                                   preferred_element_type=F32)
        if kind == "o":
            return dd(h, wo_ref[:, c * nc:(c + 1) * nc])
        col = (0 if kind == "q" else d) + c * nc
        return dd(h, wqk_ref[:, col:col + nc])

    def finish(kind, c, z):
        if kind == "v":
            vt_ref[0, c * nc:(c + 1) * nc, :] = z.astype(vt_ref.dtype)
        elif kind == "o":
            o_ref[0, :, c * nc:(c + 1) * nc] = jax.nn.sigmoid(z).astype(o_ref.dtype)
        else:
            out_ref, g_ref = (q_ref, qg_ref) if kind == "q" else (k_ref, kg_ref)
            for pr in range(nc // SUM_W):
                zn = head_rms(z[:, pr * SUM_W:(pr + 1) * SUM_W], g_ref[...])
                for j in range(SUM_W // LANES):
                    tile = (c * nc + pr * SUM_W) // LANES + j
                    store_augmented(out_ref, zn[:, j * LANES:(j + 1) * LANES], tile, aug_all[kind])

    jobs = [(kind, c) for kind in ("o", "q", "k", "v") for c in range(d // nc)]
    f_stages = forget_gate_stages()
    z_next = project(*jobs[0])
    for i, job in enumerate(jobs):
        z = z_next
        if i + 1 < len(jobs):
            z_next = project(*jobs[i + 1])
        if job[0] in ("q", "k"):
            for _ in f_stages:
                pass
        finish(*job, z)
        next(f_stages, None)
        next(f_stages, None)


def _fox_in(x, mod, w_qk, w_o, w_vt, w_f, b_f, q_g, k_g, tm=512, nc=512):
    b, t, d = x.shape
    n_heads = d // HEAD_DIM
    assert N_SPLIT * n_heads <= LANES
    qg = jnp.tile(q_g * (HEAD_DIM ** -0.5 * LOG2E), SUM_W // HEAD_DIM).reshape(1, SUM_W)
    kg = jnp.tile(k_g, SUM_W // HEAD_DIM).reshape(1, SUM_W)
    logit_bound = HEAD_DIM * jnp.max(jnp.abs(qg)) * jnp.max(jnp.abs(kg))
    bd = _head_sum_matrix()
    tri = (jnp.arange(tm)[:, None] >= jnp.arange(tm)[None, :]).astype(BF16)
    assert AUG_STRIDE * n_heads <= LANES and 2 * N_SPLIT + 1 <= AUG_STRIDE
    pq = np.zeros((LANES, LANES), np.float32)
    pk = np.zeros_like(pq)
    oq = np.zeros((1, LANES), np.float32)
    ok = np.zeros_like(oq)
    shift_lane = np.zeros_like(oq)
    for hd in range(n_heads):
        base = hd * AUG_STRIDE
        for part in range(N_SPLIT):
            pq[part * n_heads + hd, base + N_SPLIT + part] = 1.0
            ok[0, base + N_SPLIT + part] = 1.0
            pk[part * n_heads + hd, base + part] = -1.0
            oq[0, base + part] = 1.0
        shift_lane[0, base + 2 * N_SPLIT] = 1.0
    pq = jnp.asarray(pq, BF16)
    pk = jnp.asarray(pk, BF16)
    oq = jnp.asarray(oq) - logit_bound * jnp.asarray(shift_lane)
    ok = jnp.asarray(ok + shift_lane)
    act = jax.ShapeDtypeStruct((b, t, d), BF16)
    aug = jax.ShapeDtypeStruct((b, t, n_heads * LANES), BF16)
    row_spec = pl.BlockSpec((1, tm, d), lambda i, j: (i, j, 0))
    aug_spec = pl.BlockSpec((1, tm, n_heads * LANES), lambda i, j: (i, j, 0))
    kern = functools.partial(_fox_in_kernel, d=d, nc=nc)
    consts = (w_qk, w_o, w_f, b_f, qg, kg, bd, tri, pq, pk, oq, ok, w_vt)
    q_aug, k_aug, og, vt = pl.pallas_call(
        kern,
        grid=(b, t // tm),
        in_specs=[row_spec, pl.BlockSpec((1, 6, d), lambda i, j: (i, 0, 0))]
        + [_const_spec(c.shape) for c in consts],
        out_specs=[aug_spec, aug_spec, row_spec, pl.BlockSpec((1, d, tm), lambda i, j: (i, 0, j))],
        out_shape=[aug, aug, act, jax.ShapeDtypeStruct((b, d, t), BF16)],
        scratch_shapes=[pltpu.VMEM((8, LANES), F32)],
        compiler_params=_params("parallel", "arbitrary"),
        name="fox_in_proj",
    )(x, mod, *consts)
    return q_aug, k_aug, og, vt, logit_bound


def _attn_kernel(q_ref, k_ref, vt_ref, g_ref, y_ref, *, tq, tk):
    qi = pl.program_id(2)
    heads = LANES // HEAD_DIM
    pair = 2 * tk
    key = lax.broadcasted_iota(jnp.int32, (tk, tq), 0)
    qry = lax.broadcasted_iota(jnp.int32, (tk, tq), 1) + qi * tq
    qs = [q_ref[0, :, hh * LANES:(hh + 1) * LANES] for hh in range(heads)]

    def step(j, carry, masked):
        starts = [pl.multiple_of(j * pair + bb * tk, tk) for bb in range(2)]

        def scores(hh, bb):
            z = _dot_nt(k_ref[0, pl.ds(starts[bb], tk), hh * LANES:(hh + 1) * LANES], qs[hh])
            if masked:
                z = jnp.where(key + starts[bb] <= qry, z, NEG_BIG)
            return z

        def weights(z, m):
            p = jnp.exp2(z - m)
            return p, jnp.sum(p, axis=0, keepdims=True)

        def values(hh, bb, p):
            vt = vt_ref[0, hh * HEAD_DIM:(hh + 1) * HEAD_DIM, pl.ds(starts[bb], tk)]
            return jnp.dot(vt, p.astype(BF16), preferred_element_type=F32)

        colmax = lambda z: jnp.max(z, axis=0, keepdims=True)
        (m0, l0, a0), (m1, l1, a1) = carry
        z00 = scores(0, 0)
        z10 = scores(1, 0)
        m0a = jnp.maximum(m0, colmax(z00))
        z01 = scores(0, 1)
        p00, s00 = weights(z00, m0a)
        m1a = jnp.maximum(m1, colmax(z10))
        z11 = scores(1, 1)
        v00 = values(0, 0, p00)
        p10, s10 = weights(z10, m1a)
        m0b = jnp.maximum(m0a, colmax(z01))
        v10 = values(1, 0, p10)
        p01, s01 = weights(z01, m0b)
        m1b = jnp.maximum(m1a, colmax(z11))
        v01 = values(0, 1, p01)
        p11, s11 = weights(z11, m1b)
        v11 = values(1, 1, p11)

        def merge(m, l, a, ma, mb, sa, sb, va, vb):
            ra = jnp.exp2(m - ma)
            rb = jnp.exp2(ma - mb)
            return mb, rb * (ra * l + sa) + sb, rb * (ra * a + va) + vb

        return (merge(m0, l0, a0, m0a, m0b, s00, s01, v00, v01),
                merge(m1, l1, a1, m1a, m1b, s10, s11, v10, v11))

    init = tuple((jnp.full((1, tq), NEG_BIG, F32), jnp.zeros((1, tq), F32),
                  jnp.zeros((HEAD_DIM, tq), F32)) for _ in range(heads))
    n_full = (qi * tq) // pair
    carry = lax.fori_loop(0, n_full, functools.partial(step, masked=False), init)
    for jm in range(tq // pair):
        carry = step(n_full + jm, carry, True)
    yt = jnp.concatenate([carry[hh][2] / carry[hh][1] for hh in range(heads)], axis=0)
    y_ref[0] = (yt.T * g_ref[0].astype(F32)).astype(y_ref.dtype)


def _attn_bounded_kernel(q_ref, k_ref, vt_ref, g_ref, y_ref, *, tq, q_parts=2, step_blocks=(2, 1)):
    qi = pl.program_id(2)
    heads = LANES // HEAD_DIM
    part = tq // q_parts
    slots = [(hh, pi) for hh in range(heads) for pi in range(q_parts)]

    def run_chains(chains, carry):
        def scores(hh, k0, nk, pi, q0, nq, masked):
            row = pi * part + q0
            z = _dot_nt(k_ref[0, pl.ds(k0, nk), hh * LANES:(hh + 1) * LANES],
                        q_ref[0, row:row + nq, hh * LANES:(hh + 1) * LANES])
            if masked:
                key = lax.broadcasted_iota(jnp.int32, (nk, nq), 0) + k0
                qry = lax.broadcasted_iota(jnp.int32, (nk, nq), 1) + (qi * tq + row)
                z = jnp.where(key <= qry, z, NEG_BIG)
            return z

        def total(s):
            l, acc = carry[slots.index(s)]
            if sums[s]:
                l = l + functools.reduce(jnp.add, sums[s])
                acc = acc + functools.reduce(jnp.add, vals[s])
            return l, acc

        def write_part(pi):
            done_parts = [total((hh, pi)) for hh in range(heads)]
            yt = jnp.concatenate([acc / l for l, acc in done_parts], axis=0)
            rows = slice(pi * part, (pi + 1) * part)
            y_ref[0, rows, :] = (yt.T * g_ref[0, rows, :].astype(F32)).astype(y_ref.dtype)

        sums = {s: [] for s in slots}
        vals = {s: [] for s in slots}
        work = [ch for ch in chains if not isinstance(ch, int)]
        ahead = [scores(*ch) for ch in work[:LOOKAHEAD]]
        c = -1
        for ch in chains:
            if isinstance(ch, int):
                write_part(ch)
                continue
            c += 1
            hh, k0, nk, pi, q0, nq, _ = ch
            z = ahead.pop(0)
            if c + LOOKAHEAD < len(work):
                ahead.append(scores(*work[c + LOOKAHEAD]))
            p = jnp.exp2(z)
            vt = vt_ref[0, hh * HEAD_DIM:(hh + 1) * HEAD_DIM, pl.ds(k0, nk)]
            s = jnp.sum(p, axis=0, keepdims=True)
            v = jnp.dot(vt, p.astype(BF16), preferred_element_type=F32)
            if nq != part:
                s = jnp.concatenate([jnp.zeros((1, q0), F32), s], axis=1)
                v = jnp.concatenate([jnp.zeros((HEAD_DIM, q0), F32), v], axis=1)
            sums[(hh, pi)].append(s)
            vals[(hh, pi)].append(v)
        return tuple(total(s) for s in slots)

    def full_step(j, carry, nb, base):
        starts = [pl.multiple_of(base + j * (nb * part) + bb * part, part) for bb in range(nb)]
        return run_chains([(hh, starts[bb], part, pi, 0, part, False)
                           for bb in range(nb) for (hh, pi) in slots], carry)

    carry = tuple((jnp.zeros((1, part), F32), jnp.zeros((HEAD_DIM, part), F32)) for _ in slots)
    done = 0
    for nb in step_blocks:
        n_steps = (qi * tq - done) // (nb * part)
        carry = lax.fori_loop(0, n_steps, functools.partial(full_step, nb=nb, base=done), carry)
        done = done + n_steps * (nb * part)
    half = part // 2
    chains = []
    for kb in range(q_parts):
        k0 = pl.multiple_of(qi * tq + kb * part, part)
        for hh in range(heads):
            chains.append((hh, k0, half, kb, 0, part, True))
            chains.append((hh, pl.multiple_of(k0 + half, half), half, kb, half, half, True))
            chains += [(hh, k0, part, pi, 0, part, False) for pi in range(kb + 1, q_parts)]
        chains.append(kb)
    run_chains(chains, carry)


def _fox_attention(q_aug, k_aug, vt, og, logit_bound, tq=1024, q_parts=2):
    part = tq // q_parts
    return lax.cond(
        2.0 * logit_bound <= SAFE_LOGIT_RANGE,
        functools.partial(_attention_call, tq=tq, name="fox_attention_bounded",
                          kernel_fn=functools.partial(_attn_bounded_kernel, tq=tq, q_parts=q_parts)),
        functools.partial(_attention_call, tq=part, name="fox_attention",
                          kernel_fn=functools.partial(_attn_kernel, tq=part, tk=part // 2)),
        q_aug, k_aug, vt, og)


def _attention_call(q_aug, k_aug, vt, og, *, kernel_fn, tq, name):
    b, d, t = vt.shape
    heads = LANES // HEAD_DIM
    blk = pl.BlockSpec((1, tq, LANES), lambda i, p, j: (i, j, p))
    q_blk = pl.BlockSpec((1, tq, heads * LANES), lambda i, p, j: (i, j, p))
    k_full = pl.BlockSpec((1, t, heads * LANES), lambda i, p, j: (i, 0, p))
    vt_full = pl.BlockSpec((1, LANES, t), lambda i, p, j: (i, p, 0))
    return pl.pallas_call(
        kernel_fn,
        grid=(b, d // LANES, t // tq),
        in_specs=[q_blk, k_full, vt_full, blk],
        out_specs=blk,
        out_shape=jax.ShapeDtypeStruct((b, t, d), BF16),
        compiler_params=_params("parallel", "parallel", "arbitrary"),
        name=name,
    )(q_aug, k_aug, vt, og)


def _proj_ln_kernel(a_ref, x_ref, mod_ref, w_ref, lng_ref, lnb_ref, o_ref):
    y = jnp.dot(a_ref[0], w_ref[...], preferred_element_type=F32)
    z = DEEPNORM_ALPHA * x_ref[0] + mod_ref[0][2:3] * y
    o_ref[0] = _layer_norm(z, lng_ref[...], lnb_ref[...])


def _proj_ln(a, x, mod, w, ln_g, ln_b, tm=512):
    b, t, d = x.shape
    row_spec = pl.BlockSpec((1, tm, d), lambda i, j: (i, j, 0))
    return pl.pallas_call(
        _proj_ln_kernel,
        grid=(b, t // tm),
        in_specs=[row_spec, row_spec,
                  pl.BlockSpec((1, 6, d), lambda i, j: (i, 0, 0)),
                  _const_spec(w.shape), _const_spec((1, d)), _const_spec((1, d))],
        out_specs=row_spec,
        out_shape=jax.ShapeDtypeStruct((b, t, d), F32),
        compiler_params=_params("parallel", "parallel"),
        name="attn_out_proj_ln",
    )(a, x, mod, w, ln_g.reshape(1, d), ln_b.reshape(1, d))


def _ffn_kernel(x_ref, mod_ref, win_ref, wout_ref, lng_ref, lnb_ref, o_ref, act_ref, *, d_ff, fc):
    x = x_ref[0]
    mod = mod_ref[0]
    h = (x * (1.0 + mod[4:5]) + mod[3:4]).astype(BF16)
    for c in range(d_ff // fc):
        g = jnp.dot(h, win_ref[:, c * fc:(c + 1) * fc], preferred_element_type=F32)
        u = jnp.dot(h, win_ref[:, d_ff + c * fc:d_ff + (c + 1) * fc], preferred_element_type=F32)
        act_ref[:, c * fc:(c + 1) * fc] = (g * jax.nn.sigmoid(g) * u).astype(BF16)
    y = jnp.dot(act_ref[...], wout_ref[...], preferred_element_type=F32)
    z = DEEPNORM_ALPHA * x + mod[5:6] * y
    o_ref[0] = _layer_norm(z, lng_ref[...], lnb_ref[...])


def _ffn(x, mod, w_in, w_out, ln_g, ln_b, tm=512, fc=256):
    b, t, d = x.shape
    d_ff = w_out.shape[0]
    row_spec = pl.BlockSpec((1, tm, d), lambda i, j: (i, j, 0))
    return pl.pallas_call(
        functools.partial(_ffn_kernel, d_ff=d_ff, fc=fc),
        grid=(b, t // tm),
        in_specs=[row_spec,
                  pl.BlockSpec((1, 6, d), lambda i, j: (i, 0, 0)),
                  _const_spec(w_in.shape), _const_spec(w_out.shape),
                  _const_spec((1, d)), _const_spec((1, d))],
        out_specs=row_spec,
        out_shape=jax.ShapeDtypeStruct((b, t, d), F32),
        scratch_shapes=[pltpu.VMEM((tm, d_ff), BF16)],
        compiler_params=_params("parallel", "parallel"),
        name="swiglu_ln",
    )(x, mod, w_in, w_out, ln_g.reshape(1, d), ln_b.reshape(1, d))


def _rwkv_in_kernel(x_ref, xp_ref, mod_ref, mu_ref, wrkv_ref, w1_ref, w2_ref, a1_ref, a2_ref,
                    g1_ref, g2_ref, vec_ref, bd_ref,
                    r_ref, lw_ref, k_ref, v_ref, kk_ref, a_ref, g_ref, *, d):
    t = pl.program_id(1)
    mod = mod_ref[0]
    sc = 1.0 + mod[1:2]
    sh = mod[0:1]
    h = x_ref[0] * sc + sh
    tm = h.shape[0]
    prev = xp_ref[0][7:8, :] * sc + sh
    prev = jnp.where(t == 0, jnp.zeros_like(prev), prev)
    rows = lax.broadcasted_iota(jnp.int32, h.shape, 0)
    hprev = jnp.where(rows == 0, prev, pltpu.roll(h, 1, axis=0))
    dx = hprev - h
    mu = mu_ref[...]
    vec = vec_ref[...]
    w0, a0, k_k, k_a = vec[0:1], vec[1:2], vec[2:3], vec[3:4]

    def mix(n):
        return (h + dx * mu[n:n + 1]).astype(BF16)

    dd = functools.partial(jnp.dot, preferred_element_type=F32)
    t_w = dd(mix(3), w1_ref[...])
    t_a = dd(mix(4), a1_ref[...])
    t_g = dd(mix(5), g1_ref[...])
    k = dd(mix(1), wrkv_ref[1])
    a = jax.nn.sigmoid(a0 + dd(t_a.astype(BF16), a2_ref[...]))
    a_ref[0] = a.astype(a_ref.dtype)
    r = dd(mix(0), wrkv_ref[0])
    kk = k * k_k
    k_ref[0] = (k * (1.0 + (a - 1.0) * k_a)).astype(k_ref.dtype)
    bd = bd_ref[...]
    for j in range(d // SUM_W):
        sl = slice(j * SUM_W, (j + 1) * SUM_W)
        kkj = kk[:, sl]
        ss = _head_sums(kkj * kkj, bd)
        kk_ref[0, :, sl] = (kkj * jnp.minimum(lax.rsqrt(ss), 1e12)).astype(kk_ref.dtype)
    v = dd(mix(2), wrkv_ref[2])
    r_ref[0] = r.astype(r_ref.dtype)
    ww = w0 + dd(jnp.tanh(t_w).astype(BF16), w2_ref[...])
    lw_ref[0] = DECAY_SCALE * jax.nn.sigmoid(ww)
    g_ref[0] = dd(jax.nn.sigmoid(t_g).astype(BF16), g2_ref[...]).astype(g_ref.dtype)
    v_ref[0] = v.astype(v_ref.dtype)


def _rwkv_in(x, mod, mu, w_rkv, w1, w2, a1, a2, g1, g2, vec, tm=512):
    b, t, d = x.shape
    bd = _head_sum_matrix()
    row_spec = pl.BlockSpec((1, tm, d), lambda i, j: (i, j, 0))
    prev_spec = pl.BlockSpec((1, 8, d), lambda i, j: (i, jnp.maximum(j * (tm // 8) - 1, 0), 0))
    act = jax.ShapeDtypeStruct((b, t, d), BF16)
    consts = (mu, w_rkv, w1, w2, a1, a2, g1, g2, vec, bd)
    return pl.pallas_call(
        functools.partial(_rwkv_in_kernel, d=d),
        grid=(b, t // tm),
        in_specs=[row_spec, prev_spec, pl.BlockSpec((1, 6, d), lambda i, j: (i, 0, 0))]
        + [_const_spec(c.shape) for c in consts],
        out_specs=[row_spec] * 7,
        out_shape=[act, jax.ShapeDtypeStruct((b, t, d), F32), act, act, act, act, act],
        compiler_params=_params("parallel", "parallel"),
        name="rwkv_in_proj",
    )(x, x, mod, *consts)


def _rwkv_scan_kernel(r_ref, lw_ref, k_ref, v_ref, kk_ref, a_ref, lev_ref, tri_ref,
                      y_ref, h_ref, yi_ref, rq_ref, m_ref, n_ref, *, groups, steps_per_seq):
    s = pl.program_id(0)
    gr = 2 * CHUNK
    st = 2 * gr
    n_lev = CHUNK.bit_length() - 1

    @pl.when(s == 0)
    def _():
        h_ref[...] = jnp.zeros_like(h_ref)
        yi_ref[...] = jnp.zeros_like(yi_ref)
        rq_ref[...] = jnp.zeros_like(rq_ref)
        m_ref[...] = jnp.zeros_like(m_ref)
        n_ref[...] = jnp.zeros_like(n_ref)

    first_of_seq = lax.rem(jnp.maximum(s - 1, 0), steps_per_seq) == 0
    state = [jnp.where(first_of_seq, 0.0, h_ref[...])]
    pending = list(range(2 * groups))

    def state_steps(count):
        for _ in range(count):
            if pending:
                i = pending.pop(0)
                lo = i * CHUNK
                h = state[0]
                y_ref[0, lo:lo + CHUNK, :] = (yi_ref[lo:lo + CHUNK, :]
                                              + _dot(rq_ref[lo:lo + CHUNK, :], h))
                state[0] = _dot3(m_ref[i], h) + n_ref[i]

    lane = lax.broadcasted_iota(jnp.int32, (gr, LANES), 1)
    head0 = lane < HEAD_DIM
    lev = lev_ref[...]
    tri = tri_ref[...]
    strict = lev >= 0
    srow = lax.broadcasted_iota(jnp.int32, (st, st), 0)
    scol = lax.broadcasted_iota(jnp.int32, (st, st), 1)
    eye_st = srow == scol
    incl = strict | eye_st
    hrow = lax.broadcasted_iota(jnp.int32, (LANES, LANES), 0)
    hcol = lax.broadcasted_iota(jnp.int32, (LANES, LANES), 1)
    same_head = (hrow < HEAD_DIM) == (hcol < HEAD_DIM)
    eye_h = hrow == hcol

    def stack(x):
        zero = jnp.zeros_like(x)
        return jnp.concatenate([jnp.where(head0, x, zero), jnp.where(head0, zero, x)], axis=0)

    def unstack(x):
        return x[:gr] + x[gr:]

    def log_decay_cumsum(g):
        hi, mid, lo = _split3(lw_ref[0, g * gr:(g + 1) * gr, :])
        dd = functools.partial(jnp.dot, preferred_element_type=F32)
        return dd(tri, hi) + (dd(tri, mid) + dd(tri, lo))

    def prep(g, cum):
        rs = slice(g * gr, (g + 1) * gr)
        lw = lw_ref[0, rs, :]
        clast = jnp.concatenate(
            [jnp.broadcast_to(cum[(j + 1) * CHUNK - 1:(j + 1) * CHUNK], (CHUNK, LANES))
             for j in range(2)], axis=0)
        g_inv = jnp.exp(-cum)
        g_rem = jnp.exp(clast - cum)
        r = r_ref[0, rs, :].astype(F32)
        k = k_ref[0, rs, :].astype(F32)
        v = v_ref[0, rs, :].astype(F32)
        kk = kk_ref[0, rs, :].astype(F32)
        bt = kk * a_ref[0, rs, :].astype(F32)
        a_st = stack(-kk * jnp.exp(cum - lw))
        r_st = stack(r * jnp.exp(cum))
        b_st = stack(bt * g_inv)
        k_st = stack(k * g_inv)
        prods = []
        for hd in range(2):
            hs = slice(hd * gr, (hd + 1) * gr)
            prods.append(_dot_nt(jnp.concatenate([a_st[hs], r_st[hs]], axis=0),
                                 jnp.concatenate([b_st[hs], k_st[hs]], axis=0)))
        zero = jnp.zeros((gr, gr), F32)

        def by_head(rows, cols):
            return jnp.concatenate(
                [jnp.concatenate([prods[0][rows, cols], zero], axis=1),
                 jnp.concatenate([zero, prods[1][rows, cols]], axis=1)], axis=0)

        top, bot = slice(0, gr), slice(gr, st)
        return dict(
            v=v, v_st=stack(v), a_st=a_st, r_st=r_st, b_rem=bt * g_rem, k_rem=k * g_rem,
            g_last=jnp.exp(clast),
            a_ab=jnp.where(strict, by_head(top, top), 0.0),
            a_ak=jnp.where(strict, by_head(top, bot), 0.0),
            a_rb=jnp.where(incl, by_head(bot, top), 0.0),
            a_rk=jnp.where(incl, by_head(bot, bot), 0.0))

    def lower_rows(m, bsz):
        return jnp.concatenate([m[i + bsz:i + 2 * bsz] for i in range(0, st, 2 * bsz)], axis=0)

    def scatter_lower(full, low, bsz):
        parts = []
        for n, i in enumerate(range(0, st, 2 * bsz)):
            parts += [full[i:i + bsz], low[n * bsz:(n + 1) * bsz]]
        return jnp.concatenate(parts, axis=0)

    def precompute(gs):
        cums = {g: log_decay_cumsum(g) for g in gs}
        yield
        ps = {g: prep(g, cums[g]) for g in gs}
        yield
        xs = {g: jnp.where(eye_st, 1.0, jnp.where(lev == 0, ps[g]["a_ab"], 0.0)) for g in gs}
        for level in range(1, n_lev):
            bsz = 1 << level
            a_off = {g: jnp.where(lev == level, ps[g]["a_ab"], 0.0) for g in gs}
            if bsz % 8:
                ws = {g: _dot(a_off[g], xs[g]) for g in gs}
                yield
                xs = {g: xs[g] + _dot(xs[g], ws[g]) for g in gs}
            else:
                w_low = {g: _dot(lower_rows(a_off[g], bsz), xs[g]) for g in gs}
                yield
                zero = jnp.zeros((st, st), F32)
                upd = {g: _dot(lower_rows(xs[g], bsz), scatter_lower(zero, w_low[g], bsz))
                       for g in gs}
                xs = {g: scatter_lower(xs[g], lower_rows(xs[g], bsz) + upd[g], bsz) for g in gs}
            yield
        gm = {g: jnp.concatenate([_dot(ps[g]["a_ak"], ps[g]["v_st"]), ps[g]["a_st"]], axis=1)
              for g in gs}
        yield
        tg = {g: _dot(xs[g], gm[g]) for g in gs}
        yield
        rb = {g: _dot(ps[g]["a_rb"], tg[g]) for g in gs}
        rk = {g: _dot(ps[g]["a_rk"], ps[g]["v_st"]) for g in gs}
        yield
        assert not pending
        for g in gs:
            p = ps[g]
            yi_ref[g * gr:(g + 1) * gr, :] = unstack(rb[g][:, :LANES] + rk[g])
            rq_ref[g * gr:(g + 1) * gr, :] = unstack(p["r_st"] + rb[g][:, LANES:])
            uv = unstack(tg[g][:, :LANES])
            wa = unstack(tg[g][:, LANES:])
            for j in range(2):
                cs = slice(j * CHUNK, (j + 1) * CHUNK)
                lhs_t = jnp.concatenate([p["b_rem"][cs], p["k_rem"][cs]], axis=0)
                rhs_t = jnp.concatenate(
                    [jnp.concatenate([uv[cs], wa[cs]], axis=1),
                     jnp.concatenate([p["v"][cs], jnp.zeros_like(p["v"][cs])], axis=1)], axis=0)
                nm = _dot_tn(lhs_t, rhs_t)
                n_ref[2 * g + j] = jnp.where(same_head, nm[:, :LANES], 0.0)
                decay = jnp.broadcast_to(p["g_last"][j * CHUNK:j * CHUNK + 1], (LANES, LANES))
                m_ref[2 * g + j] = (jnp.where(same_head, nm[:, LANES:], 0.0)
                                    + jnp.where(eye_h, decay, 0.0))

    n_stages = 2 * n_lev + 3
    for stage, _ in enumerate(precompute(range(groups))):
        state_steps(-(-len(pending) // (n_stages - stage)))
        if not pending and state:
            h_ref[...] = state.pop()


def _rwkv_scan(r, lw, k, v, kk, a, groups=2):
    b, t, d = r.shape
    gr = 2 * CHUNK
    st = 2 * gr
    rows = gr * groups
    n_hp = d // LANES
    steps_per_seq = t // rows
    n_blocks = b * n_hp * steps_per_seq

    def block_index(blk):
        seq = blk // steps_per_seq
        return seq // n_hp, blk % steps_per_seq, seq % n_hp

    in_blk = pl.BlockSpec((1, rows, LANES), lambda s: block_index(jnp.minimum(s, n_blocks - 1)))
    out_blk = pl.BlockSpec((1, rows, LANES), lambda s: block_index(jnp.maximum(s - 1, 0)))
    idx = jnp.arange(st)
    xor = idx[:, None] ^ idx[None, :]
    same = (idx[:, None] // CHUNK) == (idx[None, :] // CHUNK)
    lower = idx[None, :] < idx[:, None]
    msb = jnp.floor(jnp.log2(jnp.maximum(xor, 1).astype(F32))).astype(jnp.int32)
    lev = jnp.where(same & lower, msb, -1).astype(jnp.int32)
    ti = jnp.arange(gr)
    tri = ((ti[:, None] >= ti[None, :]) & ((ti[:, None] // CHUNK) == (ti[None, :] // CHUNK))).astype(BF16)
    return pl.pallas_call(
        functools.partial(_rwkv_scan_kernel, groups=groups, steps_per_seq=steps_per_seq),
        grid=(n_blocks + 1,),
        in_specs=[in_blk] * 6 + [_const_spec(lev.shape), _const_spec(tri.shape)],
        out_specs=out_blk,
        out_shape=jax.ShapeDtypeStruct((b, t, d), F32),
        scratch_shapes=[pltpu.VMEM((LANES, LANES), F32),
                        pltpu.VMEM((rows, LANES), F32), pltpu.VMEM((rows, LANES), F32),
                        pltpu.VMEM((2 * groups, LANES, LANES), F32),
                        pltpu.VMEM((2 * groups, LANES, LANES), F32)],
        compiler_params=_params("arbitrary"),
        name="rwkv7_chunk_scan",
    )(r, lw, k, v, kk, a, lev, tri)


def _rwkv_out_kernel(y_ref, r_ref, k_ref, v_ref, g_ref, x_ref, mod_ref, vec_ref, bd_ref, w_ref,
                     lng_ref, lnb_ref, o_ref, act_ref, *, d):
    bd = bd_ref[...]
    vec = vec_ref[...]
    inv = 1.0 / HEAD_DIM
    for j in range(d // SUM_W):
        sl = slice(j * SUM_W, (j + 1) * SUM_W)
        y = y_ref[0, :, sl]
        mean = _dot_exact_rhs(y, bd) * inv
        yc = y - mean
        var = _head_sums(yc * yc, bd) * inv
        yn = yc * lax.rsqrt(var + GN_EPS) * vec[0:1, sl] + vec[1:2, sl]
        r = r_ref[0, :, sl].astype(F32)
        k = k_ref[0, :, sl].astype(F32)
        bonus = _dot_exact_rhs(r * k * vec[2:3, sl], bd) * v_ref[0, :, sl].astype(F32)
        act_ref[:, sl] = ((yn + bonus) * g_ref[0, :, sl].astype(F32)).astype(BF16)
    out = jnp.dot(act_ref[...], w_ref[...], preferred_element_type=F32)
    z = DEEPNORM_ALPHA * x_ref[0] + mod_ref[0][2:3] * out
    o_ref[0] = _layer_norm(z, lng_ref[...], lnb_ref[...])


def _rwkv_out(y, r, k, v, g, x, mod, vec, w, ln_g, ln_b, tm=512):
    b, t, d = x.shape
    bd = _head_sum_matrix()
    row_spec = pl.BlockSpec((1, tm, d), lambda i, j: (i, j, 0))
    return pl.pallas_call(
        functools.partial(_rwkv_out_kernel, d=d),
        grid=(b, t // tm),
        in_specs=[row_spec] * 6
        + [pl.BlockSpec((1, 6, d), lambda i, j: (i, 0, 0)),
           _const_spec(vec.shape), _const_spec(bd.shape), _const_spec(w.shape),
           _const_spec((1, d)), _const_spec((1, d))],
        out_specs=row_spec,
        out_shape=jax.ShapeDtypeStruct((b, t, d), F32),
        scratch_shapes=[pltpu.VMEM((tm, d), BF16)],
        compiler_params=_params("parallel", "parallel"),
        name="rwkv_out_proj_ln",
    )(y, r, k, v, g, x, mod, vec, bd, w, ln_g.reshape(1, d), ln_b.reshape(1, d))


def _trunk(x, c, ada_w, ada_b, ln_g, ln_b, ffn_w_in, ffn_w_out, fox_w_in, fox_b_f, fox_q_g, fox_k_g, fox_w_o, rwkv_mu, rwkv_w_rkv, rwkv_w0, rwkv_w1, rwkv_w2, rwkv_a0, rwkv_a1, rwkv_a2, rwkv_g1, rwkv_g2, rwkv_k_k, rwkv_k_a, rwkv_r_k, rwkv_lnx_g, rwkv_lnx_b, rwkv_w_o,
           *, tm, tm_rwkv, tq, q_parts, groups, tn, nc, fc):
    b, t, d = x.shape
    n_heads = d // HEAD_DIM
    mods = _mods(c, ada_w, ada_b, tn=tn)

    w_in = fox_w_in[0]
    f_lo = 3 * d
    w_qk = w_in[:, :2 * d].astype(BF16)
    w_o = w_in[:, f_lo + n_heads:].astype(BF16)
    w_vt = w_in[:, 2 * d:f_lo].T.astype(BF16)
    w_f = jnp.pad(w_in[:, f_lo:f_lo + n_heads], ((0, 0), (0, LANES - n_heads))).astype(BF16)
    b_f = jnp.pad(fox_b_f[0], (0, LANES - n_heads)).reshape(1, LANES)
    q, k, og, v, logit_bound = _fox_in(x, mods[0], w_qk, w_o, w_vt, w_f, b_f, fox_q_g[0], fox_k_g[0],
                                       tm=tm, nc=nc)
    att = _fox_attention(q, k, v, og, logit_bound, tq=tq, q_parts=q_parts)
    x = _proj_ln(att, x, mods[0], fox_w_o[0].astype(BF16), ln_g[0, 0], ln_b[0, 0], tm=tm)
    x = _ffn(x, mods[0], ffn_w_in[0].astype(BF16), ffn_w_out[0].astype(BF16), ln_g[0, 1], ln_b[0, 1],
             tm=tm, fc=fc)

    vec_in = jnp.stack([rwkv_w0[0], rwkv_a0[0], rwkv_k_k[0], rwkv_k_a[0]])
    r, lw, k, v, kk, a, g = _rwkv_in(
        x, mods[1], rwkv_mu[0], rwkv_w_rkv[0].astype(BF16),
        rwkv_w1[0].astype(BF16), rwkv_w2[0].astype(BF16),
        rwkv_a1[0].astype(BF16), rwkv_a2[0].astype(BF16),
        rwkv_g1[0].astype(BF16), rwkv_g2[0].astype(BF16), vec_in, tm=tm_rwkv)
    y = _rwkv_scan(r, lw, k, v, kk, a, groups=groups)
    vec_out = jnp.stack([rwkv_lnx_g[0], rwkv_lnx_b[0], rwkv_r_k[0].reshape(d)])
    x = _rwkv_out(y, r, k, v, g, x, mods[1], vec_out, rwkv_w_o[0].astype(BF16), ln_g[1, 0], ln_b[1, 0],
                  tm=tm)
    x = _ffn(x, mods[1], ffn_w_in[1].astype(BF16), ffn_w_out[1].astype(BF16), ln_g[1, 1], ln_b[1, 1],
             tm=tm, fc=fc)
    return x


def kernel(x, c, ada_w, ada_b, ln_g, ln_b, ffn_w_in, ffn_w_out, fox_w_in, fox_b_f, fox_q_g, fox_k_g, fox_w_o, rwkv_mu, rwkv_w_rkv, rwkv_w0, rwkv_w1, rwkv_w2, rwkv_a0, rwkv_a1, rwkv_a2, rwkv_g1, rwkv_g2, rwkv_k_k, rwkv_k_a, rwkv_r_k, rwkv_lnx_g, rwkv_lnx_b, rwkv_w_o):
    return _trunk(x, c, ada_w, ada_b, ln_g, ln_b, ffn_w_in, ffn_w_out, fox_w_in, fox_b_f, fox_q_g, fox_k_g, fox_w_o, rwkv_mu, rwkv_w_rkv, rwkv_w0, rwkv_w1, rwkv_w2, rwkv_a0, rwkv_a1, rwkv_a2, rwkv_g1, rwkv_g2, rwkv_k_k, rwkv_k_a, rwkv_r_k, rwkv_lnx_g, rwkv_lnx_b, rwkv_w_o,
                  tm=512, tm_rwkv=512, tq=2048, q_parts=4, groups=8, tn=1536, nc=512, fc=256)
```

```python
import functools

import jax
import jax.numpy as jnp
import numpy as np
from jax import lax
from jax.experimental import pallas as pl
from jax.experimental.pallas import tpu as pltpu

F32 = jnp.float32
BF16 = jnp.bfloat16

HEAD_DIM = 64
DEPTH = 2
DEEPNORM_ALPHA = (2 * DEPTH) ** 0.25
LN_EPS = 1e-5
QK_EPS = 1e-6
GN_EPS = HEAD_DIM * 1e-5
LANES = 128
SUM_W = 256
CHUNK = 64
NEG_BIG = -1e30
LOG2E = 1.4426950408889634
DECAY_SCALE = -0.6065306597126334
SAFE_LOGIT_RANGE = 96.0
N_SPLIT = 3
LOOKAHEAD = 2
AUG_STRIDE = 8
VMEM_LIMIT = 56 * 1024 * 1024


def _dot(a, b):
    return jnp.dot(a.astype(BF16), b.astype(BF16), preferred_element_type=F32)


def _dot_nt(a, b):
    return lax.dot_general(a.astype(BF16), b.astype(BF16), (((1,), (1,)), ((), ())),
                           preferred_element_type=F32)


def _dot_tn(a, b):
    return lax.dot_general(a.astype(BF16), b.astype(BF16), (((0,), (0,)), ((), ())),
                           preferred_element_type=F32)


def _split2(a):
    hi = a.astype(BF16)
    lo = (a - hi.astype(F32)).astype(BF16)
    return hi, lo


def _split3(a):
    hi = a.astype(BF16)
    r1 = a - hi.astype(F32)
    mid = r1.astype(BF16)
    lo = (r1 - mid.astype(F32)).astype(BF16)
    return hi, mid, lo


def _dot3(a, b):
    ah, al = _split2(a)
    bh, bl = _split2(b)
    d = functools.partial(jnp.dot, preferred_element_type=F32)
    return d(ah, bh) + (d(al, bh) + d(ah, bl))


def _dot_exact_rhs(a, b_exact):
    ah, al = _split2(a)
    d = functools.partial(jnp.dot, preferred_element_type=F32)
    return d(ah, b_exact) + d(al, b_exact)


def _head_sum_matrix():
    idx = jnp.arange(SUM_W) // HEAD_DIM
    return (idx[:, None] == idx[None, :]).astype(BF16)


def _layer_norm(z, g, b):
    mu = jnp.mean(z, axis=-1, keepdims=True)
    zc = z - mu
    var = jnp.mean(zc * zc, axis=-1, keepdims=True)
    return zc * lax.rsqrt(var + LN_EPS) * g + b


def _head_sums(a, head_sum_matrix):
    return jnp.dot(a.astype(BF16), head_sum_matrix, preferred_element_type=F32)


def _params(*sem):
    return pltpu.CompilerParams(dimension_semantics=sem, vmem_limit_bytes=VMEM_LIMIT)


def _const_spec(shape):
    nd = len(shape)
    return pl.BlockSpec(shape, lambda *_: (0,) * nd, pipeline_mode=pl.Buffered(1))


def _mods_kernel(c_ref, w_ref, b_ref, o_ref):
    c = c_ref[...]
    ca = c * jax.nn.sigmoid(c)
    o_ref[0] = _dot3(ca, w_ref[0]) + b_ref[0]


def _mods(c, ada_w, ada_b, tn=1536):
    depth, d, n = ada_w.shape
    b = c.shape[0]
    rows = 8
    cp = jnp.pad(c, ((0, rows - b), (0, 0)))
    out = pl.pallas_call(
        _mods_kernel,
        grid=(depth, n // tn),
        in_specs=[pl.BlockSpec((rows, d), lambda l, j: (0, 0)),
                  pl.BlockSpec((1, d, tn), lambda l, j: (l, 0, j)),
                  pl.BlockSpec((1, 1, tn), lambda l, j: (l, 0, j))],
        out_specs=pl.BlockSpec((1, rows, tn), lambda l, j: (l, 0, j)),
        out_shape=jax.ShapeDtypeStruct((depth, rows, n), F32),
        compiler_params=_params("parallel", "parallel"),
        name="adaln_mods",
    )(cp, ada_w, ada_b.reshape(depth, 1, n))
    return out[:, :b].reshape(depth, b, 6, d)


def _fox_in_kernel(x_ref, mod_ref, wqk_ref, wo_ref, wf_ref, bf_ref, qg_ref, kg_ref, bd_ref, tri_ref,
                   pq_ref, pk_ref, oq_ref, ok_ref, wvt_ref,
                   q_ref, k_ref, o_ref, vt_ref, carry_ref, *, d, nc):
    t = pl.program_id(1)

    @pl.when(t == 0)
    def _():
        carry_ref[...] = jnp.zeros_like(carry_ref)

    x = x_ref[0]
    mod = mod_ref[0]
    h = (x * (1.0 + mod[1:2]) + mod[0:1]).astype(BF16)
    tm = h.shape[0]
    bd = bd_ref[...]
    dd = functools.partial(jnp.dot, preferred_element_type=F32)
    n_heads = d // HEAD_DIM
    lane = lax.broadcasted_iota(jnp.int32, (tm, LANES), 1)
    low = lane < HEAD_DIM
    aug_even = (lane >= HEAD_DIM) & (lane < HEAD_DIM + AUG_STRIDE)
    aug_odd = lane < AUG_STRIDE
    aug_all = {}

    def forget_gate_stages():
        fl = dd(h, wf_ref[...]) + bf_ref[...]
        yield
        lf = jnp.minimum(fl, 0.0) - jnp.log(1.0 + jnp.exp(-jnp.abs(fl)))
        hi, mid, lo = _split3(lf)
        yield
        tri = tri_ref[...]
        cs = dd(tri, hi) + (dd(tri, mid) + dd(tri, lo))
        yield
        f = cs + carry_ref[0:1, :]
        carry_ref[...] = jnp.broadcast_to(f[tm - 1:tm, :], carry_ref.shape)
        parts = [jnp.where(lane < n_heads, p.astype(F32), 0.0) for p in _split3(f * LOG2E)]
        fpack = parts[0]
        for i in range(1, N_SPLIT):
            fpack = fpack + pltpu.roll(parts[i], i * n_heads, axis=1)
        fpack = fpack.astype(BF16)
        yield
        aug_all["q"] = dd(fpack, pq_ref[...]) + oq_ref[...]
        aug_all["k"] = dd(fpack, pk_ref[...]) + ok_ref[...]

    def head_rms(zp, g):
        ss = _head_sums(zp * zp, bd)
        return zp * lax.rsqrt(ss * (1.0 / HEAD_DIM) + QK_EPS) * g

    def store_augmented(out_ref, zj, tile, aug):
        even = pltpu.roll(aug, (HEAD_DIM - AUG_STRIDE * 2 * tile) % LANES, axis=1)
        odd = pltpu.roll(aug, (-AUG_STRIDE * (2 * tile + 1)) % LANES, axis=1)
        sl = slice(2 * tile * LANES, (2 * tile + 2) * LANES)
        out_ref[0, :, sl] = jnp.concatenate(
            [jnp.where(low, zj, jnp.where(aug_even, even, 0.0)),
             jnp.where(low, jnp.where(aug_odd, odd, 0.0), zj)], axis=1).astype(out_ref.dtype)

    def project(kind, c):
        if kind == "v":
            return lax.dot_general(wvt_ref[c * nc:(c + 1) * nc, :], h, (((1,), (1,)), ((), ())),
                                   preferred_element_type=F32)
        if kind == "o":
            return dd(h, wo_ref[:, c * nc:(c + 1) * nc])
        col = (0 if kind == "q" else d) + c * nc
        return dd(h, wqk_ref[:, col:col + nc])

    def finish(kind, c, z):
        if kind == "v":
            vt_ref[0, c * nc:(c + 1) * nc, :] = z.astype(vt_ref.dtype)
        elif kind == "o":
            o_ref[0, :, c * nc:(c + 1) * nc] = jax.nn.sigmoid(z).astype(o_ref.dtype)
        else:
            out_ref, g_ref = (q_ref, qg_ref) if kind == "q" else (k_ref, kg_ref)
            for pr in range(nc // SUM_W):
                zn = head_rms(z[:, pr * SUM_W:(pr + 1) * SUM_W], g_ref[...])
                for j in range(SUM_W // LANES):
                    tile = (c * nc + pr * SUM_W) // LANES + j
                    store_augmented(out_ref, zn[:, j * LANES:(j + 1) * LANES], tile, aug_all[kind])

    jobs = [(kind, c) for kind in ("o", "q", "k", "v") for c in range(d // nc)]
    f_stages = forget_gate_stages()
    z_next = project(*jobs[0])
    for i, job in enumerate(jobs):
        z = z_next
        if i + 1 < len(jobs):
            z_next = project(*jobs[i + 1])
        if job[0] in ("q", "k"):
            for _ in f_stages:
                pass
        finish(*job, z)
        next(f_stages, None)
        next(f_stages, None)


def _fox_in(x, mod, w_qk, w_o, w_vt, w_f, b_f, q_g, k_g, tm=512, nc=512):
    b, t, d = x.shape
    n_heads = d // HEAD_DIM
    assert N_SPLIT * n_heads <= LANES
    qg = jnp.tile(q_g * (HEAD_DIM ** -0.5 * LOG2E), SUM_W // HEAD_DIM).reshape(1, SUM_W)
    kg = jnp.tile(k_g, SUM_W // HEAD_DIM).reshape(1, SUM_W)
    logit_bound = HEAD_DIM * jnp.max(jnp.abs(qg)) * jnp.max(jnp.abs(kg))
    bd = _head_sum_matrix()
    tri = (jnp.arange(tm)[:, None] >= jnp.arange(tm)[None, :]).astype(BF16)
    assert AUG_STRIDE * n_heads <= LANES and 2 * N_SPLIT + 1 <= AUG_STRIDE
    pq = np.zeros((LANES, LANES), np.float32)
    pk = np.zeros_like(pq)
    oq = np.zeros((1, LANES), np.float32)
    ok = np.zeros_like(oq)
    shift_lane = np.zeros_like(oq)
    for hd in range(n_heads):
        base = hd * AUG_STRIDE
        for part in range(N_SPLIT):
            pq[part * n_heads + hd, base + N_SPLIT + part] = 1.0
            ok[0, base + N_SPLIT + part] = 1.0
            pk[part * n_heads + hd, base + part] = -1.0
            oq[0, base + part] = 1.0
        shift_lane[0, base + 2 * N_SPLIT] = 1.0
    pq = jnp.asarray(pq, BF16)
    pk = jnp.asarray(pk, BF16)
    oq = jnp.asarray(oq) - logit_bound * jnp.asarray(shift_lane)
    ok = jnp.asarray(ok + shift_lane)
    act = jax.ShapeDtypeStruct((b, t, d), BF16)
    aug = jax.ShapeDtypeStruct((b, t, n_heads * LANES), BF16)
    row_spec = pl.BlockSpec((1, tm, d), lambda i, j: (i, j, 0))
    aug_spec = pl.BlockSpec((1, tm, n_heads * LANES), lambda i, j: (i, j, 0))
    kern = functools.partial(_fox_in_kernel, d=d, nc=nc)
    consts = (w_qk, w_o, w_f, b_f, qg, kg, bd, tri, pq, pk, oq, ok, w_vt)
    q_aug, k_aug, og, vt = pl.pallas_call(
        kern,
        grid=(b, t // tm),
        in_specs=[row_spec, pl.BlockSpec((1, 6, d), lambda i, j: (i, 0, 0))]
        + [_const_spec(c.shape) for c in consts],
        out_specs=[aug_spec, aug_spec, row_spec, pl.BlockSpec((1, d, tm), lambda i, j: (i, 0, j))],
        out_shape=[aug, aug, act, jax.ShapeDtypeStruct((b, d, t), BF16)],
        scratch_shapes=[pltpu.VMEM((8, LANES), F32)],
        compiler_params=_params("parallel", "arbitrary"),
        name="fox_in_proj",
    )(x, mod, *consts)
    return q_aug, k_aug, og, vt, logit_bound


def _attn_kernel(q_ref, k_ref, vt_ref, g_ref, y_ref, *, tq, tk):
    qi = pl.program_id(2)
    heads = LANES // HEAD_DIM
    pair = 2 * tk
    key = lax.broadcasted_iota(jnp.int32, (tk, tq), 0)
    qry = lax.broadcasted_iota(jnp.int32, (tk, tq), 1) + qi * tq
    qs = [q_ref[0, :, hh * LANES:(hh + 1) * LANES] for hh in range(heads)]

    def step(j, carry, masked):
        starts = [pl.multiple_of(j * pair + bb * tk, tk) for bb in range(2)]

        def scores(hh, bb):
            z = _dot_nt(k_ref[0, pl.ds(starts[bb], tk), hh * LANES:(hh + 1) * LANES], qs[hh])
            if masked:
                z = jnp.where(key + starts[bb] <= qry, z, NEG_BIG)
            return z

        def weights(z, m):
            p = jnp.exp2(z - m)
            return p, jnp.sum(p, axis=0, keepdims=True)

        def values(hh, bb, p):
            vt = vt_ref[0, hh * HEAD_DIM:(hh + 1) * HEAD_DIM, pl.ds(starts[bb], tk)]
            return jnp.dot(vt, p.astype(BF16), preferred_element_type=F32)

        colmax = lambda z: jnp.max(z, axis=0, keepdims=True)
        (m0, l0, a0), (m1, l1, a1) = carry
        z00 = scores(0, 0)
        z10 = scores(1, 0)
        m0a = jnp.maximum(m0, colmax(z00))
        z01 = scores(0, 1)
        p00, s00 = weights(z00, m0a)
        m1a = jnp.maximum(m1, colmax(z10))
        z11 = scores(1, 1)
        v00 = values(0, 0, p00)
        p10, s10 = weights(z10, m1a)
        m0b = jnp.maximum(m0a, colmax(z01))
        v10 = values(1, 0, p10)
        p01, s01 = weights(z01, m0b)
        m1b = jnp.maximum(m1a, colmax(z11))
        v01 = values(0, 1, p01)
        p11, s11 = weights(z11, m1b)
        v11 = values(1, 1, p11)

        def merge(m, l, a, ma, mb, sa, sb, va, vb):
            ra = jnp.exp2(m - ma)
            rb = jnp.exp2(ma - mb)
            return mb, rb * (ra * l + sa) + sb, rb * (ra * a + va) + vb

        return (merge(m0, l0, a0, m0a, m0b, s00, s01, v00, v01),
                merge(m1, l1, a1, m1a, m1b, s10, s11, v10, v11))

    init = tuple((jnp.full((1, tq), NEG_BIG, F32), jnp.zeros((1, tq), F32),
                  jnp.zeros((HEAD_DIM, tq), F32)) for _ in range(heads))
    n_full = (qi * tq) // pair
    carry = lax.fori_loop(0, n_full, functools.partial(step, masked=False), init)
    for jm in range(tq // pair):
        carry = step(n_full + jm, carry, True)
    yt = jnp.concatenate([carry[hh][2] / carry[hh][1] for hh in range(heads)], axis=0)
    y_ref[0] = (yt.T * g_ref[0].astype(F32)).astype(y_ref.dtype)


def _attn_bounded_kernel(q_ref, k_ref, vt_ref, g_ref, y_ref, *, tq, q_parts=2, step_blocks=(2, 1)):
    qi = pl.program_id(2)
    heads = LANES // HEAD_DIM
    part = tq // q_parts
    slots = [(hh, pi) for hh in range(heads) for pi in range(q_parts)]

    def run_chains(chains, carry):
        def scores(hh, k0, nk, pi, q0, nq, masked):
            row = pi * part + q0
            z = _dot_nt(k_ref[0, pl.ds(k0, nk), hh * LANES:(hh + 1) * LANES],
                        q_ref[0, row:row + nq, hh * LANES:(hh + 1) * LANES])
            if masked:
                key = lax.broadcasted_iota(jnp.int32, (nk, nq), 0) + k0
                qry = lax.broadcasted_iota(jnp.int32, (nk, nq), 1) + (qi * tq + row)
                z = jnp.where(key <= qry, z, NEG_BIG)
            return z

        def total(s):
            l, acc = carry[slots.index(s)]
            if sums[s]:
                l = l + functools.reduce(jnp.add, sums[s])
                acc = acc + functools.reduce(jnp.add, vals[s])
            return l, acc

        def write_part(pi):
            done_parts = [total((hh, pi)) for hh in range(heads)]
            yt = jnp.concatenate([acc / l for l, acc in done_parts], axis=0)
            rows = slice(pi * part, (pi + 1) * part)
            y_ref[0, rows, :] = (yt.T * g_ref[0, rows, :].astype(F32)).astype(y_ref.dtype)

        sums = {s: [] for s in slots}
        vals = {s: [] for s in slots}
        work = [ch for ch in chains if not isinstance(ch, int)]
        ahead = [scores(*ch) for ch in work[:LOOKAHEAD]]
        c = -1
        for ch in chains:
            if isinstance(ch, int):
                write_part(ch)
                continue
            c += 1
            hh, k0, nk, pi, q0, nq, _ = ch
            z = ahead.pop(0)
            if c + LOOKAHEAD < len(work):
                ahead.append(scores(*work[c + LOOKAHEAD]))
            p = jnp.exp2(z)
            vt = vt_ref[0, hh * HEAD_DIM:(hh + 1) * HEAD_DIM, pl.ds(k0, nk)]
            s = jnp.sum(p, axis=0, keepdims=True)
            v = jnp.dot(vt, p.astype(BF16), preferred_element_type=F32)
            if nq != part:
                s = jnp.concatenate([jnp.zeros((1, q0), F32), s], axis=1)
                v = jnp.concatenate([jnp.zeros((HEAD_DIM, q0), F32), v], axis=1)
            sums[(hh, pi)].append(s)
            vals[(hh, pi)].append(v)
        return tuple(total(s) for s in slots)

    def full_step(j, carry, nb, base):
        starts = [pl.multiple_of(base + j * (nb * part) + bb * part, part) for bb in range(nb)]
        return run_chains([(hh, starts[bb], part, pi, 0, part, False)
                           for bb in range(nb) for (hh, pi) in slots], carry)

    carry = tuple((jnp.zeros((1, part), F32), jnp.zeros((HEAD_DIM, part), F32)) for _ in slots)
    done = 0
    for nb in step_blocks:
        n_steps = (qi * tq - done) // (nb * part)
        carry = lax.fori_loop(0, n_steps, functools.partial(full_step, nb=nb, base=done), carry)
        done = done + n_steps * (nb * part)
    half = part // 2
    chains = []
    for kb in range(q_parts):
        k0 = pl.multiple_of(qi * tq + kb * part, part)
        for hh in range(heads):
            chains.append((hh, k0, half, kb, 0, part, True))
            chains.append((hh, pl.multiple_of(k0 + half, half), half, kb, half, half, True))
            chains += [(hh, k0, part, pi, 0, part, False) for pi in range(kb + 1, q_parts)]
        chains.append(kb)
    run_chains(chains, carry)


def _fox_attention(q_aug, k_aug, vt, og, logit_bound, tq=1024, q_parts=2):
    part = tq // q_parts
    return lax.cond(
        2.0 * logit_bound <= SAFE_LOGIT_RANGE,
        functools.partial(_attention_call, tq=tq, name="fox_attention_bounded",
                          kernel_fn=functools.partial(_attn_bounded_kernel, tq=tq, q_parts=q_parts)),
        functools.partial(_attention_call, tq=part, name="fox_attention",
                          kernel_fn=functools.partial(_attn_kernel, tq=part, tk=part // 2)),
        q_aug, k_aug, vt, og)


def _attention_call(q_aug, k_aug, vt, og, *, kernel_fn, tq, name):
    b, d, t = vt.shape
    heads = LANES // HEAD_DIM
    blk = pl.BlockSpec((1, tq, LANES), lambda i, p, j: (i, j, p))
    q_blk = pl.BlockSpec((1, tq, heads * LANES), lambda i, p, j: (i, j, p))
    k_full = pl.BlockSpec((1, t, heads * LANES), lambda i, p, j: (i, 0, p))
    vt_full = pl.BlockSpec((1, LANES, t), lambda i, p, j: (i, p, 0))
    return pl.pallas_call(
        kernel_fn,
        grid=(b, d // LANES, t // tq),
        in_specs=[q_blk, k_full, vt_full, blk],
        out_specs=blk,
        out_shape=jax.ShapeDtypeStruct((b, t, d), BF16),
        compiler_params=_params("parallel", "parallel", "arbitrary"),
        name=name,
    )(q_aug, k_aug, vt, og)


def _proj_ln_kernel(a_ref, x_ref, mod_ref, w_ref, lng_ref, lnb_ref, o_ref):
    y = jnp.dot(a_ref[0], w_ref[...], preferred_element_type=F32)
    z = DEEPNORM_ALPHA * x_ref[0] + mod_ref[0][2:3] * y
    o_ref[0] = _layer_norm(z, lng_ref[...], lnb_ref[...])


def _proj_ln(a, x, mod, w, ln_g, ln_b, tm=512):
    b, t, d = x.shape
    row_spec = pl.BlockSpec((1, tm, d), lambda i, j: (i, j, 0))
    return pl.pallas_call(
        _proj_ln_kernel,
        grid=(b, t // tm),
        in_specs=[row_spec, row_spec,
                  pl.BlockSpec((1, 6, d), lambda i, j: (i, 0, 0)),
                  _const_spec(w.shape), _const_spec((1, d)), _const_spec((1, d))],
        out_specs=row_spec,
        out_shape=jax.ShapeDtypeStruct((b, t, d), F32),
        compiler_params=_params("parallel", "parallel"),
        name="attn_out_proj_ln",
    )(a, x, mod, w, ln_g.reshape(1, d), ln_b.reshape(1, d))


def _ffn_kernel(x_ref, mod_ref, win_ref, wout_ref, lng_ref, lnb_ref, o_ref, act_ref, *, d_ff, fc):
    x = x_ref[0]
    mod = mod_ref[0]
    h = (x * (1.0 + mod[4:5]) + mod[3:4]).astype(BF16)
    for c in range(d_ff // fc):
        g = jnp.dot(h, win_ref[:, c * fc:(c + 1) * fc], preferred_element_type=F32)
        u = jnp.dot(h, win_ref[:, d_ff + c * fc:d_ff + (c + 1) * fc], preferred_element_type=F32)
        act_ref[:, c * fc:(c + 1) * fc] = (g * jax.nn.sigmoid(g) * u).astype(BF16)
    y = jnp.dot(act_ref[...], wout_ref[...], preferred_element_type=F32)
    z = DEEPNORM_ALPHA * x + mod[5:6] * y
    o_ref[0] = _layer_norm(z, lng_ref[...], lnb_ref[...])


def _ffn(x, mod, w_in, w_out, layer, ln_g, ln_b, tm=512, fc=256):
    b, t, d = x.shape
    d_ff = w_out.shape[1]
    row_spec = pl.BlockSpec((1, tm, d), lambda i, j: (i, j, 0))

    def layer_spec(w):
        return pl.BlockSpec((None,) + w.shape[1:], lambda i, j: (layer, 0, 0),
                            pipeline_mode=pl.Buffered(1))

    return pl.pallas_call(
        functools.partial(_ffn_kernel, d_ff=d_ff, fc=fc),
        grid=(b, t // tm),
        in_specs=[row_spec,
                  pl.BlockSpec((1, 6, d), lambda i, j: (i, 0, 0)),
                  layer_spec(w_in), layer_spec(w_out),
                  _const_spec((1, d)), _const_spec((1, d))],
        out_specs=row_spec,
        out_shape=jax.ShapeDtypeStruct((b, t, d), F32),
        scratch_shapes=[pltpu.VMEM((tm, d_ff), BF16)],
        compiler_params=_params("parallel", "parallel"),
        name="swiglu_ln",
    )(x, mod, w_in, w_out, ln_g.reshape(1, d), ln_b.reshape(1, d))


def _rwkv_in_kernel(x_ref, xp_ref, mod_ref, mu_ref, wrkv_ref, w1_ref, w2_ref, a1_ref, a2_ref,
                    g1_ref, g2_ref, vec_ref, bd_ref,
                    r_ref, lw_ref, k_ref, v_ref, kk_ref, a_ref, g_ref, *, d):
    t = pl.program_id(1)
    mod = mod_ref[0]
    sc = 1.0 + mod[1:2]
    sh = mod[0:1]
    h = x_ref[0] * sc + sh
    tm = h.shape[0]
    prev = xp_ref[0][7:8, :] * sc + sh
    prev = jnp.where(t == 0, jnp.zeros_like(prev), prev)
    rows = lax.broadcasted_iota(jnp.int32, h.shape, 0)
    hprev = jnp.where(rows == 0, prev, pltpu.roll(h, 1, axis=0))
    dx = hprev - h
    mu = mu_ref[...]
    vec = vec_ref[...]
    w0, a0, k_k, k_a = vec[0:1], vec[1:2], vec[2:3], vec[3:4]

    def mix(n):
        return (h + dx * mu[n:n + 1]).astype(BF16)

    dd = functools.partial(jnp.dot, preferred_element_type=F32)
    t_w = dd(mix(3), w1_ref[...])
    t_a = dd(mix(4), a1_ref[...])
    t_g = dd(mix(5), g1_ref[...])
    k = dd(mix(1), wrkv_ref[1])
    a = jax.nn.sigmoid(a0 + dd(t_a.astype(BF16), a2_ref[...]))
    a_ref[0] = a.astype(a_ref.dtype)
    r = dd(mix(0), wrkv_ref[0])
    kk = k * k_k
    k_ref[0] = (k * (1.0 + (a - 1.0) * k_a)).astype(k_ref.dtype)
    bd = bd_ref[...]
    for j in range(d // SUM_W):
        sl = slice(j * SUM_W, (j + 1) * SUM_W)
        kkj = kk[:, sl]
        ss = _head_sums(kkj * kkj, bd)
        kk_ref[0, :, sl] = (kkj * jnp.minimum(lax.rsqrt(ss), 1e12)).astype(kk_ref.dtype)
    v = dd(mix(2), wrkv_ref[2])
    r_ref[0] = r.astype(r_ref.dtype)
    ww = w0 + dd(jnp.tanh(t_w).astype(BF16), w2_ref[...])
    lw_ref[0] = DECAY_SCALE * jax.nn.sigmoid(ww)
    g_ref[0] = dd(jax.nn.sigmoid(t_g).astype(BF16), g2_ref[...]).astype(g_ref.dtype)
    v_ref[0] = v.astype(v_ref.dtype)


def _rwkv_in(x, mod, mu, w_rkv, w1, w2, a1, a2, g1, g2, vec, tm=512):
    b, t, d = x.shape
    bd = _head_sum_matrix()
    row_spec = pl.BlockSpec((1, tm, d), lambda i, j: (i, j, 0))
    prev_spec = pl.BlockSpec((1, 8, d), lambda i, j: (i, jnp.maximum(j * (tm // 8) - 1, 0), 0))
    act = jax.ShapeDtypeStruct((b, t, d), BF16)
    consts = (mu, w_rkv, w1, w2, a1, a2, g1, g2, vec, bd)
    return pl.pallas_call(
        functools.partial(_rwkv_in_kernel, d=d),
        grid=(b, t // tm),
        in_specs=[row_spec, prev_spec, pl.BlockSpec((1, 6, d), lambda i, j: (i, 0, 0))]
        + [_const_spec(c.shape) for c in consts],
        out_specs=[row_spec] * 7,
        out_shape=[act, jax.ShapeDtypeStruct((b, t, d), F32), act, act, act, act, act],
        compiler_params=_params("parallel", "parallel"),
        name="rwkv_in_proj",
    )(x, x, mod, *consts)


def _rwkv_scan_kernel(r_ref, lw_ref, k_ref, v_ref, kk_ref, a_ref, lev_ref, tri_ref,
                      y_ref, h_ref, yi_ref, rq_ref, m_ref, n_ref, *, groups, steps_per_seq):
    s = pl.program_id(0)
    gr = 2 * CHUNK
    st = 2 * gr
    n_lev = CHUNK.bit_length() - 1

    @pl.when(s == 0)
    def _():
        h_ref[...] = jnp.zeros_like(h_ref)
        yi_ref[...] = jnp.zeros_like(yi_ref)
        rq_ref[...] = jnp.zeros_like(rq_ref)
        m_ref[...] = jnp.zeros_like(m_ref)
        n_ref[...] = jnp.zeros_like(n_ref)

    first_of_seq = lax.rem(jnp.maximum(s - 1, 0), steps_per_seq) == 0
    state = [jnp.where(first_of_seq, 0.0, h_ref[...])]
    pending = list(range(2 * groups))

    def state_steps(count):
        for _ in range(count):
            if pending:
                i = pending.pop(0)
                lo = i * CHUNK
                h = state[0]
                y_ref[0, lo:lo + CHUNK, :] = (yi_ref[lo:lo + CHUNK, :]
                                              + _dot(rq_ref[lo:lo + CHUNK, :], h))
                state[0] = _dot3(m_ref[i], h) + n_ref[i]

    lane = lax.broadcasted_iota(jnp.int32, (gr, LANES), 1)
    head0 = lane < HEAD_DIM
    lev = lev_ref[...]
    tri = tri_ref[...]
    strict = lev >= 0
    srow = lax.broadcasted_iota(jnp.int32, (st, st), 0)
    scol = lax.broadcasted_iota(jnp.int32, (st, st), 1)
    eye_st = srow == scol
    incl = strict | eye_st
    hrow = lax.broadcasted_iota(jnp.int32, (LANES, LANES), 0)
    hcol = lax.broadcasted_iota(jnp.int32, (LANES, LANES), 1)
    same_head = (hrow < HEAD_DIM) == (hcol < HEAD_DIM)
    eye_h = hrow == hcol

    def stack(x):
        zero = jnp.zeros_like(x)
        return jnp.concatenate([jnp.where(head0, x, zero), jnp.where(head0, zero, x)], axis=0)

    def unstack(x):
        return x[:gr] + x[gr:]

    def log_decay_cumsum(g):
        hi, mid, lo = _split3(lw_ref[0, g * gr:(g + 1) * gr, :])
        dd = functools.partial(jnp.dot, preferred_element_type=F32)
        return dd(tri, hi) + (dd(tri, mid) + dd(tri, lo))

    def prep(g, cum):
        rs = slice(g * gr, (g + 1) * gr)
        lw = lw_ref[0, rs, :]
        clast = jnp.concatenate(
            [jnp.broadcast_to(cum[(j + 1) * CHUNK - 1:(j + 1) * CHUNK], (CHUNK, LANES))
             for j in range(2)], axis=0)
        g_inv = jnp.exp(-cum)
        g_rem = jnp.exp(clast - cum)
        r = r_ref[0, rs, :].astype(F32)
        k = k_ref[0, rs, :].astype(F32)
        v = v_ref[0, rs, :].astype(F32)
        kk = kk_ref[0, rs, :].astype(F32)
        bt = kk * a_ref[0, rs, :].astype(F32)
        a_st = stack(-kk * jnp.exp(cum - lw))
        r_st = stack(r * jnp.exp(cum))
        b_st = stack(bt * g_inv)
        k_st = stack(k * g_inv)
        prods = []
        for hd in range(2):
            hs = slice(hd * gr, (hd + 1) * gr)
            prods.append(_dot_nt(jnp.concatenate([a_st[hs], r_st[hs]], axis=0),
                                 jnp.concatenate([b_st[hs], k_st[hs]], axis=0)))
        zero = jnp.zeros((gr, gr), F32)

        def by_head(rows, cols):
            return jnp.concatenate(
                [jnp.concatenate([prods[0][rows, cols], zero], axis=1),
                 jnp.concatenate([zero, prods[1][rows, cols]], axis=1)], axis=0)

        top, bot = slice(0, gr), slice(gr, st)
        return dict(
            v=v, v_st=stack(v), a_st=a_st, r_st=r_st, b_rem=bt * g_rem, k_rem=k * g_rem,
            g_last=jnp.exp(clast),
            a_ab=jnp.where(strict, by_head(top, top), 0.0),
            a_ak=jnp.where(strict, by_head(top, bot), 0.0),
            a_rb=jnp.where(incl, by_head(bot, top), 0.0),
            a_rk=jnp.where(incl, by_head(bot, bot), 0.0))

    def lower_rows(m, bsz):
        return jnp.concatenate([m[i + bsz:i + 2 * bsz] for i in range(0, st, 2 * bsz)], axis=0)

    def scatter_lower(full, low, bsz):
        parts = []
        for n, i in enumerate(range(0, st, 2 * bsz)):
            parts += [full[i:i + bsz], low[n * bsz:(n + 1) * bsz]]
        return jnp.concatenate(parts, axis=0)

    def precompute(gs):
        cums = {g: log_decay_cumsum(g) for g in gs}
        yield
        ps = {g: prep(g, cums[g]) for g in gs}
        yield
        xs = {g: jnp.where(eye_st, 1.0, jnp.where(lev == 0, ps[g]["a_ab"], 0.0)) for g in gs}
        for level in range(1, n_lev):
            bsz = 1 << level
            a_off = {g: jnp.where(lev == level, ps[g]["a_ab"], 0.0) for g in gs}
            if bsz % 8:
                ws = {g: _dot(a_off[g], xs[g]) for g in gs}
                yield
                xs = {g: xs[g] + _dot(xs[g], ws[g]) for g in gs}
            else:
                w_low = {g: _dot(lower_rows(a_off[g], bsz), xs[g]) for g in gs}
                yield
                zero = jnp.zeros((st, st), F32)
                upd = {g: _dot(lower_rows(xs[g], bsz), scatter_lower(zero, w_low[g], bsz))
                       for g in gs}
                xs = {g: scatter_lower(xs[g], lower_rows(xs[g], bsz) + upd[g], bsz) for g in gs}
            yield
        gm = {g: jnp.concatenate([_dot(ps[g]["a_ak"], ps[g]["v_st"]), ps[g]["a_st"]], axis=1)
              for g in gs}
        yield
        tg = {g: _dot(xs[g], gm[g]) for g in gs}
        yield
        rb = {g: _dot(ps[g]["a_rb"], tg[g]) for g in gs}
        rk = {g: _dot(ps[g]["a_rk"], ps[g]["v_st"]) for g in gs}
        yield
        assert not pending
        for g in gs:
            p = ps[g]
            yi_ref[g * gr:(g + 1) * gr, :] = unstack(rb[g][:, :LANES] + rk[g])
            rq_ref[g * gr:(g + 1) * gr, :] = unstack(p["r_st"] + rb[g][:, LANES:])
            uv = unstack(tg[g][:, :LANES])
            wa = unstack(tg[g][:, LANES:])
            for j in range(2):
                cs = slice(j * CHUNK, (j + 1) * CHUNK)
                lhs_t = jnp.concatenate([p["b_rem"][cs], p["k_rem"][cs]], axis=0)
                rhs_t = jnp.concatenate(
                    [jnp.concatenate([uv[cs], wa[cs]], axis=1),
                     jnp.concatenate([p["v"][cs], jnp.zeros_like(p["v"][cs])], axis=1)], axis=0)
                nm = _dot_tn(lhs_t, rhs_t)
                n_ref[2 * g + j] = jnp.where(same_head, nm[:, :LANES], 0.0)
                decay = jnp.broadcast_to(p["g_last"][j * CHUNK:j * CHUNK + 1], (LANES, LANES))
                m_ref[2 * g + j] = (jnp.where(same_head, nm[:, LANES:], 0.0)
                                    + jnp.where(eye_h, decay, 0.0))

    n_stages = 2 * n_lev + 3
    for stage, _ in enumerate(precompute(range(groups))):
        state_steps(-(-len(pending) // (n_stages - stage)))
        if not pending and state:
            h_ref[...] = state.pop()


def _rwkv_scan(r, lw, k, v, kk, a, groups=2):
    b, t, d = r.shape
    gr = 2 * CHUNK
    st = 2 * gr
    rows = gr * groups
    n_hp = d // LANES
    steps_per_seq = t // rows
    n_blocks = b * n_hp * steps_per_seq

    def block_index(blk):
        seq = blk // steps_per_seq
        return seq // n_hp, blk % steps_per_seq, seq % n_hp

    in_blk = pl.BlockSpec((1, rows, LANES), lambda s: block_index(jnp.minimum(s, n_blocks - 1)))
    out_blk = pl.BlockSpec((1, rows, LANES), lambda s: block_index(jnp.maximum(s - 1, 0)))
    idx = jnp.arange(st)
    xor = idx[:, None] ^ idx[None, :]
    same = (idx[:, None] // CHUNK) == (idx[None, :] // CHUNK)
    lower = idx[None, :] < idx[:, None]
    msb = jnp.floor(jnp.log2(jnp.maximum(xor, 1).astype(F32))).astype(jnp.int32)
    lev = jnp.where(same & lower, msb, -1).astype(jnp.int32)
    ti = jnp.arange(gr)
    tri = ((ti[:, None] >= ti[None, :]) & ((ti[:, None] // CHUNK) == (ti[None, :] // CHUNK))).astype(BF16)
    return pl.pallas_call(
        functools.partial(_rwkv_scan_kernel, groups=groups, steps_per_seq=steps_per_seq),
        grid=(n_blocks + 1,),
        in_specs=[in_blk] * 6 + [_const_spec(lev.shape), _const_spec(tri.shape)],
        out_specs=out_blk,
        out_shape=jax.ShapeDtypeStruct((b, t, d), F32),
        scratch_shapes=[pltpu.VMEM((LANES, LANES), F32),
                        pltpu.VMEM((rows, LANES), F32), pltpu.VMEM((rows, LANES), F32),
                        pltpu.VMEM((2 * groups, LANES, LANES), F32),
                        pltpu.VMEM((2 * groups, LANES, LANES), F32)],
        compiler_params=_params("arbitrary"),
        name="rwkv7_chunk_scan",
    )(r, lw, k, v, kk, a, lev, tri)


def _rwkv_out_kernel(y_ref, r_ref, k_ref, v_ref, g_ref, x_ref, mod_ref, vec_ref, bd_ref, w_ref,
                     lng_ref, lnb_ref, o_ref, act_ref, *, d):
    bd = bd_ref[...]
    vec = vec_ref[...]
    inv = 1.0 / HEAD_DIM
    for j in range(d // SUM_W):
        sl = slice(j * SUM_W, (j + 1) * SUM_W)
        y = y_ref[0, :, sl]
        mean = _dot_exact_rhs(y, bd) * inv
        yc = y - mean
        var = _head_sums(yc * yc, bd) * inv
        yn = yc * lax.rsqrt(var + GN_EPS) * vec[0:1, sl] + vec[1:2, sl]
        r = r_ref[0, :, sl].astype(F32)
        k = k_ref[0, :, sl].astype(F32)
        bonus = _dot_exact_rhs(r * k * vec[2:3, sl], bd) * v_ref[0, :, sl].astype(F32)
        act_ref[:, sl] = ((yn + bonus) * g_ref[0, :, sl].astype(F32)).astype(BF16)
    out = jnp.dot(act_ref[...], w_ref[...], preferred_element_type=F32)
    z = DEEPNORM_ALPHA * x_ref[0] + mod_ref[0][2:3] * out
    o_ref[0] = _layer_norm(z, lng_ref[...], lnb_ref[...])


def _rwkv_out(y, r, k, v, g, x, mod, vec, w, ln_g, ln_b, tm=512):
    b, t, d = x.shape
    bd = _head_sum_matrix()
    row_spec = pl.BlockSpec((1, tm, d), lambda i, j: (i, j, 0))
    return pl.pallas_call(
        functools.partial(_rwkv_out_kernel, d=d),
        grid=(b, t // tm),
        in_specs=[row_spec] * 6
        + [pl.BlockSpec((1, 6, d), lambda i, j: (i, 0, 0)),
           _const_spec(vec.shape), _const_spec(bd.shape), _const_spec(w.shape),
           _const_spec((1, d)), _const_spec((1, d))],
        out_specs=row_spec,
        out_shape=jax.ShapeDtypeStruct((b, t, d), F32),
        scratch_shapes=[pltpu.VMEM((tm, d), BF16)],
        compiler_params=_params("parallel", "parallel"),
        name="rwkv_out_proj_ln",
    )(y, r, k, v, g, x, mod, vec, bd, w, ln_g.reshape(1, d), ln_b.reshape(1, d))


def _trunk(x, c, ada_w, ada_b, ln_g, ln_b, ffn_w_in, ffn_w_out, fox_w_in, fox_b_f, fox_q_g, fox_k_g, fox_w_o, rwkv_mu, rwkv_w_rkv, rwkv_w0, rwkv_w1, rwkv_w2, rwkv_a0, rwkv_a1, rwkv_a2, rwkv_g1, rwkv_g2, rwkv_k_k, rwkv_k_a, rwkv_r_k, rwkv_lnx_g, rwkv_lnx_b, rwkv_w_o,
           *, tm, tm_rwkv, tq, q_parts, groups, tn, nc, fc):
    b, t, d = x.shape
    n_heads = d // HEAD_DIM
    mods = _mods(c, ada_w, ada_b, tn=tn)
    ffn_in = ffn_w_in.astype(BF16)
    ffn_out = ffn_w_out.astype(BF16)

    w_in = fox_w_in[0]
    f_lo = 3 * d
    w_qk = w_in[:, :2 * d].astype(BF16)
    w_o = w_in[:, f_lo + n_heads:].astype(BF16)
    w_vt = w_in[:, 2 * d:f_lo].T.astype(BF16)
    w_f = jnp.pad(w_in[:, f_lo:f_lo + n_heads], ((0, 0), (0, LANES - n_heads))).astype(BF16)
    b_f = jnp.pad(fox_b_f[0], (0, LANES - n_heads)).reshape(1, LANES)
    q, k, og, v, logit_bound = _fox_in(x, mods[0], w_qk, w_o, w_vt, w_f, b_f, fox_q_g[0], fox_k_g[0],
                                       tm=tm, nc=nc)
    att = _fox_attention(q, k, v, og, logit_bound, tq=tq, q_parts=q_parts)
    x = _proj_ln(att, x, mods[0], fox_w_o[0].astype(BF16), ln_g[0, 0], ln_b[0, 0], tm=tm)
    x = _ffn(x, mods[0], ffn_in, ffn_out, 0, ln_g[0, 1], ln_b[0, 1], tm=tm, fc=fc)

    vec_in = jnp.stack([rwkv_w0[0], rwkv_a0[0], rwkv_k_k[0], rwkv_k_a[0]])
    r, lw, k, v, kk, a, g = _rwkv_in(
        x, mods[1], rwkv_mu[0], rwkv_w_rkv[0].astype(BF16),
        rwkv_w1[0].astype(BF16), rwkv_w2[0].astype(BF16),
        rwkv_a1[0].astype(BF16), rwkv_a2[0].astype(BF16),
        rwkv_g1[0].astype(BF16), rwkv_g2[0].astype(BF16), vec_in, tm=tm_rwkv)
    y = _rwkv_scan(r, lw, k, v, kk, a, groups=groups)
    vec_out = jnp.stack([rwkv_lnx_g[0], rwkv_lnx_b[0], rwkv_r_k[0].reshape(d)])
    x = _rwkv_out(y, r, k, v, g, x, mods[1], vec_out, rwkv_w_o[0].astype(BF16), ln_g[1, 0], ln_b[1, 0],
                  tm=tm)
    x = _ffn(x, mods[1], ffn_in, ffn_out, 1, ln_g[1, 1], ln_b[1, 1], tm=tm, fc=fc)
    return x


def kernel(x, c, ada_w, ada_b, ln_g, ln_b, ffn_w_in, ffn_w_out, fox_w_in, fox_b_f, fox_q_g, fox_k_g, fox_w_o, rwkv_mu, rwkv_w_rkv, rwkv_w0, rwkv_w1, rwkv_w2, rwkv_a0, rwkv_a1, rwkv_a2, rwkv_g1, rwkv_g2, rwkv_k_k, rwkv_k_a, rwkv_r_k, rwkv_lnx_g, rwkv_lnx_b, rwkv_w_o):
    return _trunk(x, c, ada_w, ada_b, ln_g, ln_b, ffn_w_in, ffn_w_out, fox_w_in, fox_b_f, fox_q_g, fox_k_g, fox_w_o, rwkv_mu, rwkv_w_rkv, rwkv_w0, rwkv_w1, rwkv_w2, rwkv_a0, rwkv_a1, rwkv_a2, rwkv_g1, rwkv_g2, rwkv_k_k, rwkv_k_a, rwkv_r_k, rwkv_lnx_g, rwkv_lnx_b, rwkv_w_o,
                  tm=512, tm_rwkv=512, tq=2048, q_parts=4, groups=8, tn=1536, nc=512, fc=256)
```

```python
import functools

import jax
import jax.numpy as jnp
import numpy as np
from jax import lax
from jax.experimental import pallas as pl
from jax.experimental.pallas import tpu as pltpu

F32 = jnp.float32
BF16 = jnp.bfloat16

HEAD_DIM = 64
DEPTH = 2
DEEPNORM_ALPHA = (2 * DEPTH) ** 0.25
LN_EPS = 1e-5
QK_EPS = 1e-6
GN_EPS = HEAD_DIM * 1e-5
LANES = 128
SUM_W = 256
CHUNK = 64
NEG_BIG = -1e30
LOG2E = 1.4426950408889634
DECAY_SCALE = -0.6065306597126334
SAFE_LOGIT_RANGE = 96.0
N_SPLIT = 3
LOOKAHEAD = 2
AUG_STRIDE = 8
VMEM_LIMIT = 56 * 1024 * 1024


def _dot(a, b):
    return jnp.dot(a.astype(BF16), b.astype(BF16), preferred_element_type=F32)


def _dot_nt(a, b):
    return lax.dot_general(a.astype(BF16), b.astype(BF16), (((1,), (1,)), ((), ())),
                           preferred_element_type=F32)


def _dot_tn(a, b):
    return lax.dot_general(a.astype(BF16), b.astype(BF16), (((0,), (0,)), ((), ())),
                           preferred_element_type=F32)


def _split2(a):
    hi = a.astype(BF16)
    lo = (a - hi.astype(F32)).astype(BF16)
    return hi, lo


def _split3(a):
    hi = a.astype(BF16)
    r1 = a - hi.astype(F32)
    mid = r1.astype(BF16)
    lo = (r1 - mid.astype(F32)).astype(BF16)
    return hi, mid, lo


def _dot3(a, b):
    ah, al = _split2(a)
    bh, bl = _split2(b)
    d = functools.partial(jnp.dot, preferred_element_type=F32)
    return d(ah, bh) + (d(al, bh) + d(ah, bl))


def _dot_exact_rhs(a, b_exact):
    ah, al = _split2(a)
    d = functools.partial(jnp.dot, preferred_element_type=F32)
    return d(ah, b_exact) + d(al, b_exact)


def _head_sum_matrix():
    idx = jnp.arange(SUM_W) // HEAD_DIM
    return (idx[:, None] == idx[None, :]).astype(BF16)


def _layer_norm(z, g, b):
    mu = jnp.mean(z, axis=-1, keepdims=True)
    zc = z - mu
    var = jnp.mean(zc * zc, axis=-1, keepdims=True)
    return zc * lax.rsqrt(var + LN_EPS) * g + b


def _head_sums(a, head_sum_matrix):
    return jnp.dot(a.astype(BF16), head_sum_matrix, preferred_element_type=F32)


def _params(*sem):
    return pltpu.CompilerParams(dimension_semantics=sem, vmem_limit_bytes=VMEM_LIMIT)


def _const_spec(shape):
    nd = len(shape)
    return pl.BlockSpec(shape, lambda *_: (0,) * nd, pipeline_mode=pl.Buffered(1))


def _mods_kernel(c_ref, w_ref, b_ref, o_ref):
    c = c_ref[...]
    ca = c * jax.nn.sigmoid(c)
    o_ref[0] = _dot3(ca, w_ref[0]) + b_ref[0]


def _mods(c, ada_w, ada_b, tn=1536):
    depth, d, n = ada_w.shape
    b = c.shape[0]
    rows = 8
    cp = jnp.pad(c, ((0, rows - b), (0, 0)))
    out = pl.pallas_call(
        _mods_kernel,
        grid=(depth, n // tn),
        in_specs=[pl.BlockSpec((rows, d), lambda l, j: (0, 0)),
                  pl.BlockSpec((1, d, tn), lambda l, j: (l, 0, j)),
                  pl.BlockSpec((1, 1, tn), lambda l, j: (l, 0, j))],
        out_specs=pl.BlockSpec((1, rows, tn), lambda l, j: (l, 0, j)),
        out_shape=jax.ShapeDtypeStruct((depth, rows, n), F32),
        compiler_params=_params("parallel", "parallel"),
        name="adaln_mods",
    )(cp, ada_w, ada_b.reshape(depth, 1, n))
    return out[:, :b].reshape(depth, b, 6, d)


def _fox_in_kernel(x_ref, mod_ref, wqk_ref, wo_ref, wf_ref, bf_ref, qg_ref, kg_ref, bd_ref, tri_ref,
                   pq_ref, pk_ref, oq_ref, ok_ref, wvt_ref,
                   q_ref, k_ref, o_ref, vt_ref, carry_ref, *, d, nc):
    t = pl.program_id(1)

    @pl.when(t == 0)
    def _():
        carry_ref[...] = jnp.zeros_like(carry_ref)

    x = x_ref[0]
    mod = mod_ref[0]
    h = (x * (1.0 + mod[1:2]) + mod[0:1]).astype(BF16)
    tm = h.shape[0]
    bd = bd_ref[...]
    dd = functools.partial(jnp.dot, preferred_element_type=F32)
    n_heads = d // HEAD_DIM
    lane = lax.broadcasted_iota(jnp.int32, (tm, LANES), 1)
    low = lane < HEAD_DIM
    aug_even = (lane >= HEAD_DIM) & (lane < HEAD_DIM + AUG_STRIDE)
    aug_odd = lane < AUG_STRIDE
    aug_all = {}

    def forget_gate_stages():
        fl = dd(h, wf_ref[...]) + bf_ref[...]
        yield
        lf = jnp.minimum(fl, 0.0) - jnp.log(1.0 + jnp.exp(-jnp.abs(fl)))
        hi, mid, lo = _split3(lf)
        yield
        tri = tri_ref[...]
        cs = dd(tri, hi) + (dd(tri, mid) + dd(tri, lo))
        yield
        f = cs + carry_ref[0:1, :]
        carry_ref[...] = jnp.broadcast_to(f[tm - 1:tm, :], carry_ref.shape)
        parts = [jnp.where(lane < n_heads, p.astype(F32), 0.0) for p in _split3(f * LOG2E)]
        fpack = parts[0]
        for i in range(1, N_SPLIT):
            fpack = fpack + pltpu.roll(parts[i], i * n_heads, axis=1)
        fpack = fpack.astype(BF16)
        yield
        aug_all["q"] = dd(fpack, pq_ref[...]) + oq_ref[...]
        aug_all["k"] = dd(fpack, pk_ref[...]) + ok_ref[...]

    def head_rms(zp, g):
        ss = _head_sums(zp * zp, bd)
        return zp * lax.rsqrt(ss * (1.0 / HEAD_DIM) + QK_EPS) * g

    def store_augmented(out_ref, zj, tile, aug):
        even = pltpu.roll(aug, (HEAD_DIM - AUG_STRIDE * 2 * tile) % LANES, axis=1)
        odd = pltpu.roll(aug, (-AUG_STRIDE * (2 * tile + 1)) % LANES, axis=1)
        sl = slice(2 * tile * LANES, (2 * tile + 2) * LANES)
        out_ref[0, :, sl] = jnp.concatenate(
            [jnp.where(low, zj, jnp.where(aug_even, even, 0.0)),
             jnp.where(low, jnp.where(aug_odd, odd, 0.0), zj)], axis=1).astype(out_ref.dtype)

    def project(kind, c):
        if kind == "v":
            return lax.dot_general(wvt_ref[c * nc:(c + 1) * nc, :], h, (((1,), (1,)), ((), ())),
                                   preferred_element_type=F32)
        if kind == "o":
            return dd(h, wo_ref[:, c * nc:(c + 1) * nc])
        col = (0 if kind == "q" else d) + c * nc
        return dd(h, wqk_ref[:, col:col + nc])

    def finish(kind, c, z):
        if kind == "v":
            vt_ref[0, c * nc:(c + 1) * nc, :] = z.astype(vt_ref.dtype)
        elif kind == "o":
            o_ref[0, :, c * nc:(c + 1) * nc] = jax.nn.sigmoid(z).astype(o_ref.dtype)
        else:
            out_ref, g_ref = (q_ref, qg_ref) if kind == "q" else (k_ref, kg_ref)
            for pr in range(nc // SUM_W):
                zn = head_rms(z[:, pr * SUM_W:(pr + 1) * SUM_W], g_ref[...])
                for j in range(SUM_W // LANES):
                    tile = (c * nc + pr * SUM_W) // LANES + j
                    store_augmented(out_ref, zn[:, j * LANES:(j + 1) * LANES], tile, aug_all[kind])

    jobs = [(kind, c) for kind in ("o", "q", "k", "v") for c in range(d // nc)]
    f_stages = forget_gate_stages()
    z_next = project(*jobs[0])
    for i, job in enumerate(jobs):
        z = z_next
        if i + 1 < len(jobs):
            z_next = project(*jobs[i + 1])
        if job[0] in ("q", "k"):
            for _ in f_stages:
                pass
        finish(*job, z)
        next(f_stages, None)
        next(f_stages, None)


def _fox_in(x, mod, w_qk, w_o, w_vt, w_f, b_f, q_g, k_g, tm=512, nc=512):
    b, t, d = x.shape
    n_heads = d // HEAD_DIM
    assert N_SPLIT * n_heads <= LANES
    qg = jnp.tile(q_g * (HEAD_DIM ** -0.5 * LOG2E), SUM_W // HEAD_DIM).reshape(1, SUM_W)
    kg = jnp.tile(k_g, SUM_W // HEAD_DIM).reshape(1, SUM_W)
    logit_bound = HEAD_DIM * jnp.max(jnp.abs(qg)) * jnp.max(jnp.abs(kg))
    bd = _head_sum_matrix()
    tri = (jnp.arange(tm)[:, None] >= jnp.arange(tm)[None, :]).astype(BF16)
    assert AUG_STRIDE * n_heads <= LANES and 2 * N_SPLIT + 1 <= AUG_STRIDE
    pq = np.zeros((LANES, LANES), np.float32)
    pk = np.zeros_like(pq)
    oq = np.zeros((1, LANES), np.float32)
    ok = np.zeros_like(oq)
    shift_lane = np.zeros_like(oq)
    for hd in range(n_heads):
        base = hd * AUG_STRIDE
        for part in range(N_SPLIT):
            pq[part * n_heads + hd, base + N_SPLIT + part] = 1.0
            ok[0, base + N_SPLIT + part] = 1.0
            pk[part * n_heads + hd, base + part] = -1.0
            oq[0, base + part] = 1.0
        shift_lane[0, base + 2 * N_SPLIT] = 1.0
    pq = jnp.asarray(pq, BF16)
    pk = jnp.asarray(pk, BF16)
    oq = jnp.asarray(oq) - logit_bound * jnp.asarray(shift_lane)
    ok = jnp.asarray(ok + shift_lane)
    act = jax.ShapeDtypeStruct((b, t, d), BF16)
    aug = jax.ShapeDtypeStruct((b, t, n_heads * LANES), BF16)
    row_spec = pl.BlockSpec((1, tm, d), lambda i, j: (i, j, 0))
    aug_spec = pl.BlockSpec((1, tm, n_heads * LANES), lambda i, j: (i, j, 0))
    kern = functools.partial(_fox_in_kernel, d=d, nc=nc)
    consts = (w_qk, w_o, w_f, b_f, qg, kg, bd, tri, pq, pk, oq, ok, w_vt)
    q_aug, k_aug, og, vt = pl.pallas_call(
        kern,
        grid=(b, t // tm),
        in_specs=[row_spec, pl.BlockSpec((1, 6, d), lambda i, j: (i, 0, 0))]
        + [_const_spec(c.shape) for c in consts],
        out_specs=[aug_spec, aug_spec, row_spec, pl.BlockSpec((1, d, tm), lambda i, j: (i, 0, j))],
        out_shape=[aug, aug, act, jax.ShapeDtypeStruct((b, d, t), BF16)],
        scratch_shapes=[pltpu.VMEM((8, LANES), F32)],
        compiler_params=_params("parallel", "arbitrary"),
        name="fox_in_proj",
    )(x, mod, *consts)
    return q_aug, k_aug, og, vt, logit_bound


def _attn_kernel(q_ref, k_ref, vt_ref, g_ref, y_ref, *, tq, tk):
    qi = pl.program_id(2)
    heads = LANES // HEAD_DIM
    pair = 2 * tk
    key = lax.broadcasted_iota(jnp.int32, (tk, tq), 0)
    qry = lax.broadcasted_iota(jnp.int32, (tk, tq), 1) + qi * tq
    qs = [q_ref[0, :, hh * LANES:(hh + 1) * LANES] for hh in range(heads)]

    def step(j, carry, masked):
        starts = [pl.multiple_of(j * pair + bb * tk, tk) for bb in range(2)]

        def scores(hh, bb):
            z = _dot_nt(k_ref[0, pl.ds(starts[bb], tk), hh * LANES:(hh + 1) * LANES], qs[hh])
            if masked:
                z = jnp.where(key + starts[bb] <= qry, z, NEG_BIG)
            return z

        def weights(z, m):
            p = jnp.exp2(z - m)
            return p, jnp.sum(p, axis=0, keepdims=True)

        def values(hh, bb, p):
            vt = vt_ref[0, hh * HEAD_DIM:(hh + 1) * HEAD_DIM, pl.ds(starts[bb], tk)]
            return jnp.dot(vt, p.astype(BF16), preferred_element_type=F32)

        colmax = lambda z: jnp.max(z, axis=0, keepdims=True)
        (m0, l0, a0), (m1, l1, a1) = carry
        z00 = scores(0, 0)
        z10 = scores(1, 0)
        m0a = jnp.maximum(m0, colmax(z00))
        z01 = scores(0, 1)
        p00, s00 = weights(z00, m0a)
        m1a = jnp.maximum(m1, colmax(z10))
        z11 = scores(1, 1)
        v00 = values(0, 0, p00)
        p10, s10 = weights(z10, m1a)
        m0b = jnp.maximum(m0a, colmax(z01))
        v10 = values(1, 0, p10)
        p01, s01 = weights(z01, m0b)
        m1b = jnp.maximum(m1a, colmax(z11))
        v01 = values(0, 1, p01)
        p11, s11 = weights(z11, m1b)
        v11 = values(1, 1, p11)

        def merge(m, l, a, ma, mb, sa, sb, va, vb):
            ra = jnp.exp2(m - ma)
            rb = jnp.exp2(ma - mb)
            return mb, rb * (ra * l + sa) + sb, rb * (ra * a + va) + vb

        return (merge(m0, l0, a0, m0a, m0b, s00, s01, v00, v01),
                merge(m1, l1, a1, m1a, m1b, s10, s11, v10, v11))

    init = tuple((jnp.full((1, tq), NEG_BIG, F32), jnp.zeros((1, tq), F32),
                  jnp.zeros((HEAD_DIM, tq), F32)) for _ in range(heads))
    n_full = (qi * tq) // pair
    carry = lax.fori_loop(0, n_full, functools.partial(step, masked=False), init)
    for jm in range(tq // pair):
        carry = step(n_full + jm, carry, True)
    yt = jnp.concatenate([carry[hh][2] / carry[hh][1] for hh in range(heads)], axis=0)
    y_ref[0] = (yt.T * g_ref[0].astype(F32)).astype(y_ref.dtype)


def _attn_bounded_kernel(q_ref, k_ref, vt_ref, g_ref, y_ref, *, tq, q_parts=2, step_blocks=(2, 1)):
    qi = pl.program_id(2)
    heads = LANES // HEAD_DIM
    part = tq // q_parts
    slots = [(hh, pi) for hh in range(heads) for pi in range(q_parts)]

    def run_chains(chains, carry):
        def scores(hh, k0, nk, pi, q0, nq, masked):
            row = pi * part + q0
            z = _dot_nt(k_ref[0, pl.ds(k0, nk), hh * LANES:(hh + 1) * LANES],
                        q_ref[0, row:row + nq, hh * LANES:(hh + 1) * LANES])
            if masked:
                key = lax.broadcasted_iota(jnp.int32, (nk, nq), 0) + k0
                qry = lax.broadcasted_iota(jnp.int32, (nk, nq), 1) + (qi * tq + row)
                z = jnp.where(key <= qry, z, NEG_BIG)
            return z

        def total(s):
            l, acc = carry[slots.index(s)]
            if sums[s]:
                l = l + functools.reduce(jnp.add, sums[s])
                acc = acc + functools.reduce(jnp.add, vals[s])
            return l, acc

        def write_part(pi):
            done_parts = [total((hh, pi)) for hh in range(heads)]
            yt = jnp.concatenate([acc / l for l, acc in done_parts], axis=0)
            rows = slice(pi * part, (pi + 1) * part)
            y_ref[0, rows, :] = (yt.T * g_ref[0, rows, :].astype(F32)).astype(y_ref.dtype)

        sums = {s: [] for s in slots}
        vals = {s: [] for s in slots}
        work = [ch for ch in chains if not isinstance(ch, int)]
        ahead = [scores(*ch) for ch in work[:LOOKAHEAD]]
        c = -1
        for ch in chains:
            if isinstance(ch, int):
                write_part(ch)
                continue
            c += 1
            hh, k0, nk, pi, q0, nq, _ = ch
            z = ahead.pop(0)
            if c + LOOKAHEAD < len(work):
                ahead.append(scores(*work[c + LOOKAHEAD]))
            p = jnp.exp2(z)
            vt = vt_ref[0, hh * HEAD_DIM:(hh + 1) * HEAD_DIM, pl.ds(k0, nk)]
            s = jnp.sum(p, axis=0, keepdims=True)
            v = jnp.dot(vt, p.astype(BF16), preferred_element_type=F32)
            if nq != part:
                s = jnp.concatenate([jnp.zeros((1, q0), F32), s], axis=1)
                v = jnp.concatenate([jnp.zeros((HEAD_DIM, q0), F32), v], axis=1)
            sums[(hh, pi)].append(s)
            vals[(hh, pi)].append(v)
        return tuple(total(s) for s in slots)

    def full_step(j, carry, nb, base):
        starts = [pl.multiple_of(base + j * (nb * part) + bb * part, part) for bb in range(nb)]
        return run_chains([(hh, starts[bb], part, pi, 0, part, False)
                           for bb in range(nb) for (hh, pi) in slots], carry)

    carry = tuple((jnp.zeros((1, part), F32), jnp.zeros((HEAD_DIM, part), F32)) for _ in slots)
    done = 0
    for nb in step_blocks:
        n_steps = (qi * tq - done) // (nb * part)
        carry = lax.fori_loop(0, n_steps, functools.partial(full_step, nb=nb, base=done), carry)
        done = done + n_steps * (nb * part)
    half = part // 2
    chains = []
    for kb in range(q_parts):
        k0 = pl.multiple_of(qi * tq + kb * part, part)
        for hh in range(heads):
            chains.append((hh, k0, half, kb, 0, part, True))
            chains.append((hh, pl.multiple_of(k0 + half, half), half, kb, half, half, True))
            chains += [(hh, k0, part, pi, 0, part, False) for pi in range(kb + 1, q_parts)]
        chains.append(kb)
    run_chains(chains, carry)


def _fox_attention(q_aug, k_aug, vt, og, logit_bound, tq=1024, q_parts=2):
    part = tq // q_parts
    return lax.cond(
        2.0 * logit_bound <= SAFE_LOGIT_RANGE,
        functools.partial(_attention_call, tq=tq, name="fox_attention_bounded",
                          kernel_fn=functools.partial(_attn_bounded_kernel, tq=tq, q_parts=q_parts)),
        functools.partial(_attention_call, tq=part, name="fox_attention",
                          kernel_fn=functools.partial(_attn_kernel, tq=part, tk=part // 2)),
        q_aug, k_aug, vt, og)


def _attention_call(q_aug, k_aug, vt, og, *, kernel_fn, tq, name):
    b, d, t = vt.shape
    heads = LANES // HEAD_DIM
    blk = pl.BlockSpec((1, tq, LANES), lambda i, p, j: (i, j, p))
    q_blk = pl.BlockSpec((1, tq, heads * LANES), lambda i, p, j: (i, j, p))
    k_full = pl.BlockSpec((1, t, heads * LANES), lambda i, p, j: (i, 0, p))
    vt_full = pl.BlockSpec((1, LANES, t), lambda i, p, j: (i, p, 0))
    return pl.pallas_call(
        kernel_fn,
        grid=(b, d // LANES, t // tq),
        in_specs=[q_blk, k_full, vt_full, blk],
        out_specs=blk,
        out_shape=jax.ShapeDtypeStruct((b, t, d), BF16),
        compiler_params=_params("parallel", "parallel", "arbitrary"),
        name=name,
    )(q_aug, k_aug, vt, og)


def _proj_ln_kernel(a_ref, x_ref, mod_ref, w_ref, lng_ref, lnb_ref, o_ref):
    y = jnp.dot(a_ref[0], w_ref[...], preferred_element_type=F32)
    z = DEEPNORM_ALPHA * x_ref[0] + mod_ref[0][2:3] * y
    o_ref[0] = _layer_norm(z, lng_ref[...], lnb_ref[...])


def _proj_ln(a, x, mod, w, ln_g, ln_b, tm=512):
    b, t, d = x.shape
    row_spec = pl.BlockSpec((1, tm, d), lambda i, j: (i, j, 0))
    return pl.pallas_call(
        _proj_ln_kernel,
        grid=(b, t // tm),
        in_specs=[row_spec, row_spec,
                  pl.BlockSpec((1, 6, d), lambda i, j: (i, 0, 0)),
                  _const_spec(w.shape), _const_spec((1, d)), _const_spec((1, d))],
        out_specs=row_spec,
        out_shape=jax.ShapeDtypeStruct((b, t, d), F32),
        compiler_params=_params("parallel", "parallel"),
        name="attn_out_proj_ln",
    )(a, x, mod, w, ln_g.reshape(1, d), ln_b.reshape(1, d))


def _ffn_kernel(x_ref, mod_ref, win_ref, wout_ref, lng_ref, lnb_ref, o_ref, act_ref, *, d_ff, fc):
    x = x_ref[0]
    mod = mod_ref[0]
    h = (x * (1.0 + mod[4:5]) + mod[3:4]).astype(BF16)
    for c in range(d_ff // fc):
        g = jnp.dot(h, win_ref[:, c * fc:(c + 1) * fc], preferred_element_type=F32)
        u = jnp.dot(h, win_ref[:, d_ff + c * fc:d_ff + (c + 1) * fc], preferred_element_type=F32)
        act_ref[:, c * fc:(c + 1) * fc] = (g * jax.nn.sigmoid(g) * u).astype(BF16)
    y = jnp.dot(act_ref[...], wout_ref[...], preferred_element_type=F32)
    z = DEEPNORM_ALPHA * x + mod[5:6] * y
    o_ref[0] = _layer_norm(z, lng_ref[...], lnb_ref[...])


def _ffn(x, mod, w_in, w_out, layer, ln_g, ln_b, tm=512, fc=256):
    b, t, d = x.shape
    d_ff = w_out.shape[1]
    row_spec = pl.BlockSpec((1, tm, d), lambda i, j: (i, j, 0))

    def layer_spec(w):
        return pl.BlockSpec((None,) + w.shape[1:], lambda i, j: (layer, 0, 0),
                            pipeline_mode=pl.Buffered(1))

    return pl.pallas_call(
        functools.partial(_ffn_kernel, d_ff=d_ff, fc=fc),
        grid=(b, t // tm),
        in_specs=[row_spec,
                  pl.BlockSpec((1, 6, d), lambda i, j: (i, 0, 0)),
                  layer_spec(w_in), layer_spec(w_out),
                  _const_spec((1, d)), _const_spec((1, d))],
        out_specs=row_spec,
        out_shape=jax.ShapeDtypeStruct((b, t, d), F32),
        scratch_shapes=[pltpu.VMEM((tm, d_ff), BF16)],
        compiler_params=_params("parallel", "parallel"),
        name="swiglu_ln",
    )(x, mod, w_in, w_out, ln_g.reshape(1, d), ln_b.reshape(1, d))


def _rwkv_in_kernel(x_ref, xp_ref, mod_ref, mu_ref, wrkv_ref, w1_ref, w2_ref, a1_ref, a2_ref,
                    g1_ref, g2_ref, vec_ref, bd_ref,
                    r_ref, lw_ref, k_ref, v_ref, kk_ref, a_ref, g_ref, *, d):
    t = pl.program_id(1)
    mod = mod_ref[0]
    sc = 1.0 + mod[1:2]
    sh = mod[0:1]
    h = x_ref[0] * sc + sh
    tm = h.shape[0]
    prev = xp_ref[0][7:8, :] * sc + sh
    prev = jnp.where(t == 0, jnp.zeros_like(prev), prev)
    rows = lax.broadcasted_iota(jnp.int32, h.shape, 0)
    hprev = jnp.where(rows == 0, prev, pltpu.roll(h, 1, axis=0))
    dx = hprev - h
    mu = mu_ref[...]
    vec = vec_ref[...]
    w0, a0, k_k, k_a = vec[0:1], vec[1:2], vec[2:3], vec[3:4]

    def mix(n):
        return (h + dx * mu[n:n + 1]).astype(BF16)

    dd = functools.partial(jnp.dot, preferred_element_type=F32)
    t_w = dd(mix(3), w1_ref[...])
    t_a = dd(mix(4), a1_ref[...])
    t_g = dd(mix(5), g1_ref[...])
    k = dd(mix(1), wrkv_ref[1])
    a = jax.nn.sigmoid(a0 + dd(t_a.astype(BF16), a2_ref[...]))
    a_ref[0] = a.astype(a_ref.dtype)
    r = dd(mix(0), wrkv_ref[0])
    kk = k * k_k
    k_ref[0] = (k * (1.0 + (a - 1.0) * k_a)).astype(k_ref.dtype)
    bd = bd_ref[...]
    for j in range(d // SUM_W):
        sl = slice(j * SUM_W, (j + 1) * SUM_W)
        kkj = kk[:, sl]
        ss = _head_sums(kkj * kkj, bd)
        kk_ref[0, :, sl] = (kkj * jnp.minimum(lax.rsqrt(ss), 1e12)).astype(kk_ref.dtype)
    v = dd(mix(2), wrkv_ref[2])
    r_ref[0] = r.astype(r_ref.dtype)
    ww = w0 + dd(jnp.tanh(t_w).astype(BF16), w2_ref[...])
    lw_ref[0] = DECAY_SCALE * jax.nn.sigmoid(ww)
    g_ref[0] = dd(jax.nn.sigmoid(t_g).astype(BF16), g2_ref[...]).astype(g_ref.dtype)
    v_ref[0] = v.astype(v_ref.dtype)


def _rwkv_in(x, mod, mu, w_rkv, w1, w2, a1, a2, g1, g2, vec, tm=512):
    b, t, d = x.shape
    bd = _head_sum_matrix()
    row_spec = pl.BlockSpec((1, tm, d), lambda i, j: (i, j, 0))
    prev_spec = pl.BlockSpec((1, 8, d), lambda i, j: (i, jnp.maximum(j * (tm // 8) - 1, 0), 0))
    act = jax.ShapeDtypeStruct((b, t, d), BF16)
    consts = (mu, w_rkv, w1, w2, a1, a2, g1, g2, vec, bd)
    return pl.pallas_call(
        functools.partial(_rwkv_in_kernel, d=d),
        grid=(b, t // tm),
        in_specs=[row_spec, prev_spec, pl.BlockSpec((1, 6, d), lambda i, j: (i, 0, 0))]
        + [_const_spec(c.shape) for c in consts],
        out_specs=[row_spec] * 7,
        out_shape=[act, jax.ShapeDtypeStruct((b, t, d), F32), act, act, act, act, act],
        compiler_params=_params("parallel", "parallel"),
        name="rwkv_in_proj",
    )(x, x, mod, *consts)


def _rwkv_scan_kernel(r_ref, lw_ref, k_ref, v_ref, kk_ref, a_ref, lev_ref, tri_ref,
                      y_ref, h_ref, yi_ref, rq_ref, m_ref, n_ref, *, groups, steps_per_seq):
    s = pl.program_id(0)
    gr = 2 * CHUNK
    st = 2 * gr
    n_lev = CHUNK.bit_length() - 1

    @pl.when(s == 0)
    def _():
        h_ref[...] = jnp.zeros_like(h_ref)
        yi_ref[...] = jnp.zeros_like(yi_ref)
        rq_ref[...] = jnp.zeros_like(rq_ref)
        m_ref[...] = jnp.zeros_like(m_ref)
        n_ref[...] = jnp.zeros_like(n_ref)

    first_of_seq = lax.rem(jnp.maximum(s - 1, 0), steps_per_seq) == 0
    state = [jnp.where(first_of_seq, 0.0, h_ref[...])]
    pending = list(range(2 * groups))

    def state_steps(count):
        for _ in range(count):
            if pending:
                i = pending.pop(0)
                lo = i * CHUNK
                h = state[0]
                y_ref[0, lo:lo + CHUNK, :] = (yi_ref[lo:lo + CHUNK, :]
                                              + _dot(rq_ref[lo:lo + CHUNK, :], h))
                state[0] = _dot3(m_ref[i], h) + n_ref[i]

    lane = lax.broadcasted_iota(jnp.int32, (gr, LANES), 1)
    head0 = lane < HEAD_DIM
    lev = lev_ref[...]
    tri = tri_ref[...]
    strict = lev >= 0
    srow = lax.broadcasted_iota(jnp.int32, (st, st), 0)
    scol = lax.broadcasted_iota(jnp.int32, (st, st), 1)
    eye_st = srow == scol
    incl = strict | eye_st
    hrow = lax.broadcasted_iota(jnp.int32, (LANES, LANES), 0)
    hcol = lax.broadcasted_iota(jnp.int32, (LANES, LANES), 1)
    same_head = (hrow < HEAD_DIM) == (hcol < HEAD_DIM)
    eye_h = hrow == hcol

    def stack(x):
        zero = jnp.zeros_like(x)
        return jnp.concatenate([jnp.where(head0, x, zero), jnp.where(head0, zero, x)], axis=0)

    def unstack(x):
        return x[:gr] + x[gr:]

    def log_decay_cumsum(g):
        hi, mid, lo = _split3(lw_ref[0, g * gr:(g + 1) * gr, :])
        dd = functools.partial(jnp.dot, preferred_element_type=F32)
        return dd(tri, hi) + (dd(tri, mid) + dd(tri, lo))

    def prep(g, cum):
        rs = slice(g * gr, (g + 1) * gr)
        lw = lw_ref[0, rs, :]
        clast = jnp.concatenate(
            [jnp.broadcast_to(cum[(j + 1) * CHUNK - 1:(j + 1) * CHUNK], (CHUNK, LANES))
             for j in range(2)], axis=0)
        g_inv = jnp.exp(-cum)
        g_rem = jnp.exp(clast - cum)
        r = r_ref[0, rs, :].astype(F32)
        k = k_ref[0, rs, :].astype(F32)
        v = v_ref[0, rs, :].astype(F32)
        kk = kk_ref[0, rs, :].astype(F32)
        bt = kk * a_ref[0, rs, :].astype(F32)
        a_st = stack(-kk * jnp.exp(cum - lw))
        r_st = stack(r * jnp.exp(cum))
        b_st = stack(bt * g_inv)
        k_st = stack(k * g_inv)
        prods = []
        for hd in range(2):
            hs = slice(hd * gr, (hd + 1) * gr)
            prods.append(_dot_nt(jnp.concatenate([a_st[hs], r_st[hs]], axis=0),
                                 jnp.concatenate([b_st[hs], k_st[hs]], axis=0)))
        zero = jnp.zeros((gr, gr), F32)

        def by_head(rows, cols):
            return jnp.concatenate(
                [jnp.concatenate([prods[0][rows, cols], zero], axis=1),
                 jnp.concatenate([zero, prods[1][rows, cols]], axis=1)], axis=0)

        top, bot = slice(0, gr), slice(gr, st)
        return dict(
            v=v, v_st=stack(v), a_st=a_st, r_st=r_st, b_rem=bt * g_rem, k_rem=k * g_rem,
            g_last=jnp.exp(clast),
            a_ab=jnp.where(strict, by_head(top, top), 0.0),
            a_ak=jnp.where(strict, by_head(top, bot), 0.0),
            a_rb=jnp.where(incl, by_head(bot, top), 0.0),
            a_rk=jnp.where(incl, by_head(bot, bot), 0.0))

    def lower_rows(m, bsz):
        return jnp.concatenate([m[i + bsz:i + 2 * bsz] for i in range(0, st, 2 * bsz)], axis=0)

    def scatter_lower(full, low, bsz):
        parts = []
        for n, i in enumerate(range(0, st, 2 * bsz)):
            parts += [full[i:i + bsz], low[n * bsz:(n + 1) * bsz]]
        return jnp.concatenate(parts, axis=0)

    def precompute(gs):
        cums = {g: log_decay_cumsum(g) for g in gs}
        yield
        ps = {g: prep(g, cums[g]) for g in gs}
        yield
        xs = {g: jnp.where(eye_st, 1.0, jnp.where(lev == 0, ps[g]["a_ab"], 0.0)) for g in gs}
        for level in range(1, n_lev):
            bsz = 1 << level
            a_off = {g: jnp.where(lev == level, ps[g]["a_ab"], 0.0) for g in gs}
            if bsz % 8:
                ws = {g: _dot(a_off[g], xs[g]) for g in gs}
                yield
                xs = {g: xs[g] + _dot(xs[g], ws[g]) for g in gs}
            else:
                w_low = {g: _dot(lower_rows(a_off[g], bsz), xs[g]) for g in gs}
                yield
                zero = jnp.zeros((st, st), F32)
                upd = {g: _dot(lower_rows(xs[g], bsz), scatter_lower(zero, w_low[g], bsz))
                       for g in gs}
                xs = {g: scatter_lower(xs[g], lower_rows(xs[g], bsz) + upd[g], bsz) for g in gs}
            yield
        gm = {g: jnp.concatenate([_dot(ps[g]["a_ak"], ps[g]["v_st"]), ps[g]["a_st"]], axis=1)
              for g in gs}
        yield
        tg = {g: _dot(xs[g], gm[g]) for g in gs}
        yield
        rb = {g: _dot(ps[g]["a_rb"], tg[g]) for g in gs}
        rk = {g: _dot(ps[g]["a_rk"], ps[g]["v_st"]) for g in gs}
        yield
        assert not pending
        for g in gs:
            p = ps[g]
            yi_ref[g * gr:(g + 1) * gr, :] = unstack(rb[g][:, :LANES] + rk[g])
            rq_ref[g * gr:(g + 1) * gr, :] = unstack(p["r_st"] + rb[g][:, LANES:])
            uv = unstack(tg[g][:, :LANES])
            wa = unstack(tg[g][:, LANES:])
            for j in range(2):
                cs = slice(j * CHUNK, (j + 1) * CHUNK)
                lhs_t = jnp.concatenate([p["b_rem"][cs], p["k_rem"][cs]], axis=0)
                rhs_t = jnp.concatenate(
                    [jnp.concatenate([uv[cs], wa[cs]], axis=1),
                     jnp.concatenate([p["v"][cs], jnp.zeros_like(p["v"][cs])], axis=1)], axis=0)
                nm = _dot_tn(lhs_t, rhs_t)
                n_ref[2 * g + j] = jnp.where(same_head, nm[:, :LANES], 0.0)
                decay = jnp.broadcast_to(p["g_last"][j * CHUNK:j * CHUNK + 1], (LANES, LANES))
                m_ref[2 * g + j] = (jnp.where(same_head, nm[:, LANES:], 0.0)
                                    + jnp.where(eye_h, decay, 0.0))

    n_stages = 2 * n_lev + 3
    for stage, _ in enumerate(precompute(range(groups))):
        state_steps(-(-len(pending) // (n_stages - stage)))
        if not pending and state:
            h_ref[...] = state.pop()


def _rwkv_scan(r, lw, k, v, kk, a, groups=2):
    b, t, d = r.shape
    gr = 2 * CHUNK
    st = 2 * gr
    rows = gr * groups
    n_hp = d // LANES
    steps_per_seq = t // rows
    n_blocks = b * n_hp * steps_per_seq

    def block_index(blk):
        seq = blk // steps_per_seq
        return seq // n_hp, blk % steps_per_seq, seq % n_hp

    in_blk = pl.BlockSpec((1, rows, LANES), lambda s: block_index(jnp.minimum(s, n_blocks - 1)))
    out_blk = pl.BlockSpec((1, rows, LANES), lambda s: block_index(jnp.maximum(s - 1, 0)))
    idx = jnp.arange(st)
    xor = idx[:, None] ^ idx[None, :]
    same = (idx[:, None] // CHUNK) == (idx[None, :] // CHUNK)
    lower = idx[None, :] < idx[:, None]
    msb = jnp.floor(jnp.log2(jnp.maximum(xor, 1).astype(F32))).astype(jnp.int32)
    lev = jnp.where(same & lower, msb, -1).astype(jnp.int32)
    ti = jnp.arange(gr)
    tri = ((ti[:, None] >= ti[None, :]) & ((ti[:, None] // CHUNK) == (ti[None, :] // CHUNK))).astype(BF16)
    return pl.pallas_call(
        functools.partial(_rwkv_scan_kernel, groups=groups, steps_per_seq=steps_per_seq),
        grid=(n_blocks + 1,),
        in_specs=[in_blk] * 6 + [_const_spec(lev.shape), _const_spec(tri.shape)],
        out_specs=out_blk,
        out_shape=jax.ShapeDtypeStruct((b, t, d), F32),
        scratch_shapes=[pltpu.VMEM((LANES, LANES), F32),
                        pltpu.VMEM((rows, LANES), F32), pltpu.VMEM((rows, LANES), F32),
                        pltpu.VMEM((2 * groups, LANES, LANES), F32),
                        pltpu.VMEM((2 * groups, LANES, LANES), F32)],
        compiler_params=_params("arbitrary"),
        name="rwkv7_chunk_scan",
    )(r, lw, k, v, kk, a, lev, tri)


def _rwkv_out_kernel(y_ref, r_ref, k_ref, v_ref, g_ref, x_ref, mod_ref, vec_ref, bd_ref, w_ref,
                     lng_ref, lnb_ref, o_ref, act_ref, *, d):
    bd = bd_ref[...]
    vec = vec_ref[...]
    inv = 1.0 / HEAD_DIM
    for j in range(d // SUM_W):
        sl = slice(j * SUM_W, (j + 1) * SUM_W)
        y = y_ref[0, :, sl]
        mean = _dot_exact_rhs(y, bd) * inv
        yc = y - mean
        var = _head_sums(yc * yc, bd) * inv
        yn = yc * lax.rsqrt(var + GN_EPS) * vec[0:1, sl] + vec[1:2, sl]
        r = r_ref[0, :, sl].astype(F32)
        k = k_ref[0, :, sl].astype(F32)
        bonus = _dot_exact_rhs(r * k * vec[2:3, sl], bd) * v_ref[0, :, sl].astype(F32)
        act_ref[:, sl] = ((yn + bonus) * g_ref[0, :, sl].astype(F32)).astype(BF16)
    out = jnp.dot(act_ref[...], w_ref[...], preferred_element_type=F32)
    z = DEEPNORM_ALPHA * x_ref[0] + mod_ref[0][2:3] * out
    o_ref[0] = _layer_norm(z, lng_ref[...], lnb_ref[...])


def _rwkv_out(y, r, k, v, g, x, mod, vec, w, ln_g, ln_b, tm=512):
    b, t, d = x.shape
    bd = _head_sum_matrix()
    row_spec = pl.BlockSpec((1, tm, d), lambda i, j: (i, j, 0))
    return pl.pallas_call(
        functools.partial(_rwkv_out_kernel, d=d),
        grid=(b, t // tm),
        in_specs=[row_spec] * 6
        + [pl.BlockSpec((1, 6, d), lambda i, j: (i, 0, 0)),
           _const_spec(vec.shape), _const_spec(bd.shape), _const_spec(w.shape),
           _const_spec((1, d)), _const_spec((1, d))],
        out_specs=row_spec,
        out_shape=jax.ShapeDtypeStruct((b, t, d), F32),
        scratch_shapes=[pltpu.VMEM((tm, d), BF16)],
        compiler_params=_params("parallel", "parallel"),
        name="rwkv_out_proj_ln",
    )(y, r, k, v, g, x, mod, vec, bd, w, ln_g.reshape(1, d), ln_b.reshape(1, d))


def _trunk(x, c, ada_w, ada_b, ln_g, ln_b, ffn_w_in, ffn_w_out, fox_w_in, fox_b_f, fox_q_g, fox_k_g, fox_w_o, rwkv_mu, rwkv_w_rkv, rwkv_w0, rwkv_w1, rwkv_w2, rwkv_a0, rwkv_a1, rwkv_a2, rwkv_g1, rwkv_g2, rwkv_k_k, rwkv_k_a, rwkv_r_k, rwkv_lnx_g, rwkv_lnx_b, rwkv_w_o,
           *, tm, tm_rwkv, tq, q_parts, groups, tn, nc, fc):
    b, t, d = x.shape
    n_heads = d // HEAD_DIM
    mods = _mods(c, ada_w, ada_b, tn=tn)
    ffn_in = ffn_w_in.astype(BF16)
    ffn_out = ffn_w_out.astype(BF16)

    w_in = fox_w_in[0]
    f_lo = 3 * d
    w_qk = w_in[:, :2 * d].astype(BF16)
    w_o = w_in[:, f_lo + n_heads:].astype(BF16)
    w_vt = w_in[:, 2 * d:f_lo].T.astype(BF16)
    w_f = jnp.pad(w_in[:, f_lo:f_lo + n_heads], ((0, 0), (0, LANES - n_heads))).astype(BF16)
    b_f = jnp.pad(fox_b_f[0], (0, LANES - n_heads)).reshape(1, LANES)
    q, k, og, v, logit_bound = _fox_in(x, mods[0], w_qk, w_o, w_vt, w_f, b_f, fox_q_g[0], fox_k_g[0],
                                       tm=tm, nc=nc)
    att = _fox_attention(q, k, v, og, logit_bound, tq=tq, q_parts=q_parts)
    x = _proj_ln(att, x, mods[0], fox_w_o[0].astype(BF16), ln_g[0, 0], ln_b[0, 0], tm=tm)
    x = _ffn(x, mods[0], ffn_in, ffn_out, 0, ln_g[0, 1], ln_b[0, 1], tm=tm, fc=fc)

    vec_in = jnp.stack([rwkv_w0[0], rwkv_a0[0], rwkv_k_k[0], rwkv_k_a[0]])
    r, lw, k, v, kk, a, g = _rwkv_in(
        x, mods[1], rwkv_mu[0], rwkv_w_rkv[0].astype(BF16),
        rwkv_w1[0].astype(BF16), rwkv_w2[0].astype(BF16),
        rwkv_a1[0].astype(BF16), rwkv_a2[0].astype(BF16),
        rwkv_g1[0].astype(BF16), rwkv_g2[0].astype(BF16), vec_in, tm=tm_rwkv)
    y = _rwkv_scan(r, lw, k, v, kk, a, groups=groups)
    vec_out = jnp.stack([rwkv_lnx_g[0], rwkv_lnx_b[0], rwkv_r_k[0].reshape(d)])
    x = _rwkv_out(y, r, k, v, g, x, mods[1], vec_out, rwkv_w_o[0].astype(BF16), ln_g[1, 0], ln_b[1, 0],
                  tm=tm)
    x = _ffn(x, mods[1], ffn_in, ffn_out, 1, ln_g[1, 1], ln_b[1, 1], tm=tm, fc=fc)
    return x


def kernel(x, c, ada_w, ada_b, ln_g, ln_b, ffn_w_in, ffn_w_out, fox_w_in, fox_b_f, fox_q_g, fox_k_g, fox_w_o, rwkv_mu, rwkv_w_rkv, rwkv_w0, rwkv_w1, rwkv_w2, rwkv_a0, rwkv_a1, rwkv_a2, rwkv_g1, rwkv_g2, rwkv_k_k, rwkv_k_a, rwkv_r_k, rwkv_lnx_g, rwkv_lnx_b, rwkv_w_o):
    return _trunk(x, c, ada_w, ada_b, ln_g, ln_b, ffn_w_in, ffn_w_out, fox_w_in, fox_b_f, fox_q_g, fox_k_g, fox_w_o, rwkv_mu, rwkv_w_rkv, rwkv_w0, rwkv_w1, rwkv_w2, rwkv_a0, rwkv_a1, rwkv_a2, rwkv_g1, rwkv_g2, rwkv_k_k, rwkv_k_a, rwkv_r_k, rwkv_lnx_g, rwkv_lnx_b, rwkv_w_o,
                  tm=512, tm_rwkv=512, tq=4096, q_parts=8, groups=8, tn=1536, nc=512, fc=256)
```

```python
import functools

import jax
import jax.numpy as jnp
import numpy as np
from jax import lax
from jax.experimental import pallas as pl
from jax.experimental.pallas import tpu as pltpu

F32 = jnp.float32
BF16 = jnp.bfloat16

HEAD_DIM = 64
DEPTH = 2
DEEPNORM_ALPHA = (2 * DEPTH) ** 0.25
LN_EPS = 1e-5
QK_EPS = 1e-6
GN_EPS = HEAD_DIM * 1e-5
LANES = 128
SUM_W = 256
CHUNK = 64
NEG_BIG = -1e30
LOG2E = 1.4426950408889634
DECAY_SCALE = -0.6065306597126334
SAFE_LOGIT_RANGE = 96.0
N_SPLIT = 3
LOOKAHEAD = 2
AUG_STRIDE = 8
VMEM_LIMIT = 56 * 1024 * 1024


def _dot(a, b):
    return jnp.dot(a.astype(BF16), b.astype(BF16), preferred_element_type=F32)


def _dot_nt(a, b):
    return lax.dot_general(a.astype(BF16), b.astype(BF16), (((1,), (1,)), ((), ())),
                           preferred_element_type=F32)


def _dot_tn(a, b):
    return lax.dot_general(a.astype(BF16), b.astype(BF16), (((0,), (0,)), ((), ())),
                           preferred_element_type=F32)


def _split2(a):
    hi = a.astype(BF16)
    lo = (a - hi.astype(F32)).astype(BF16)
    return hi, lo


def _split3(a):
    hi = a.astype(BF16)
    r1 = a - hi.astype(F32)
    mid = r1.astype(BF16)
    lo = (r1 - mid.astype(F32)).astype(BF16)
    return hi, mid, lo


def _dot3(a, b):
    ah, al = _split2(a)
    bh, bl = _split2(b)
    d = functools.partial(jnp.dot, preferred_element_type=F32)
    return d(ah, bh) + (d(al, bh) + d(ah, bl))


def _dot_exact_rhs(a, b_exact):
    ah, al = _split2(a)
    d = functools.partial(jnp.dot, preferred_element_type=F32)
    return d(ah, b_exact) + d(al, b_exact)


def _head_sum_matrix():
    idx = jnp.arange(SUM_W) // HEAD_DIM
    return (idx[:, None] == idx[None, :]).astype(BF16)


def _layer_norm(z, g, b):
    mu = jnp.mean(z, axis=-1, keepdims=True)
    zc = z - mu
    var = jnp.mean(zc * zc, axis=-1, keepdims=True)
    return zc * lax.rsqrt(var + LN_EPS) * g + b


def _head_sums(a, head_sum_matrix):
    return jnp.dot(a.astype(BF16), head_sum_matrix, preferred_element_type=F32)


def _params(*sem):
    return pltpu.CompilerParams(dimension_semantics=sem, vmem_limit_bytes=VMEM_LIMIT)


def _const_spec(shape):
    nd = len(shape)
    return pl.BlockSpec(shape, lambda *_: (0,) * nd, pipeline_mode=pl.Buffered(1))


def _mods_kernel(c_ref, w_ref, b_ref, o_ref):
    c = c_ref[...]
    ca = c * jax.nn.sigmoid(c)
    o_ref[0] = _dot3(ca, w_ref[0]) + b_ref[0]


def _mods(c, ada_w, ada_b, tn=1536):
    depth, d, n = ada_w.shape
    b = c.shape[0]
    rows = 8
    cp = jnp.pad(c, ((0, rows - b), (0, 0)))
    out = pl.pallas_call(
        _mods_kernel,
        grid=(depth, n // tn),
        in_specs=[pl.BlockSpec((rows, d), lambda l, j: (0, 0)),
                  pl.BlockSpec((1, d, tn), lambda l, j: (l, 0, j)),
                  pl.BlockSpec((1, 1, tn), lambda l, j: (l, 0, j))],
        out_specs=pl.BlockSpec((1, rows, tn), lambda l, j: (l, 0, j)),
        out_shape=jax.ShapeDtypeStruct((depth, rows, n), F32),
        compiler_params=_params("parallel", "parallel"),
        name="adaln_mods",
    )(cp, ada_w, ada_b.reshape(depth, 1, n))
    return out[:, :b].reshape(depth, b, 6, d)


def _fox_in_kernel(x_ref, mod_ref, wqk_ref, wo_ref, wf_ref, bf_ref, qg_ref, kg_ref, bd_ref, tri_ref,
                   pq_ref, pk_ref, oq_ref, ok_ref, wvt_ref,
                   q_ref, k_ref, o_ref, vt_ref, carry_ref, *, d, nc):
    t = pl.program_id(1)

    @pl.when(t == 0)
    def _():
        carry_ref[...] = jnp.zeros_like(carry_ref)

    x = x_ref[0]
    mod = mod_ref[0]
    h = (x * (1.0 + mod[1:2]) + mod[0:1]).astype(BF16)
    tm = h.shape[0]
    bd = bd_ref[...]
    dd = functools.partial(jnp.dot, preferred_element_type=F32)
    n_heads = d // HEAD_DIM
    lane = lax.broadcasted_iota(jnp.int32, (tm, LANES), 1)
    low = lane < HEAD_DIM
    aug_even = (lane >= HEAD_DIM) & (lane < HEAD_DIM + AUG_STRIDE)
    aug_odd = lane < AUG_STRIDE
    aug_all = {}

    def forget_gate_stages():
        fl = dd(h, wf_ref[...]) + bf_ref[...]
        yield
        lf = jnp.minimum(fl, 0.0) - jnp.log(1.0 + jnp.exp(-jnp.abs(fl)))
        hi, mid, lo = _split3(lf)
        yield
        tri = tri_ref[...]
        cs = dd(tri, hi) + (dd(tri, mid) + dd(tri, lo))
        yield
        f = cs + carry_ref[0:1, :]
        carry_ref[...] = jnp.broadcast_to(f[tm - 1:tm, :], carry_ref.shape)
        parts = [jnp.where(lane < n_heads, p.astype(F32), 0.0) for p in _split3(f * LOG2E)]
        fpack = parts[0]
        for i in range(1, N_SPLIT):
            fpack = fpack + pltpu.roll(parts[i], i * n_heads, axis=1)
        fpack = fpack.astype(BF16)
        yield
        aug_all["q"] = dd(fpack, pq_ref[...]) + oq_ref[...]
        aug_all["k"] = dd(fpack, pk_ref[...]) + ok_ref[...]

    def head_rms(zp, g):
        ss = _head_sums(zp * zp, bd)
        return zp * lax.rsqrt(ss * (1.0 / HEAD_DIM) + QK_EPS) * g

    def store_augmented(out_ref, zj, tile, aug):
        even = pltpu.roll(aug, (HEAD_DIM - AUG_STRIDE * 2 * tile) % LANES, axis=1)
        odd = pltpu.roll(aug, (-AUG_STRIDE * (2 * tile + 1)) % LANES, axis=1)
        sl = slice(2 * tile * LANES, (2 * tile + 2) * LANES)
        out_ref[0, :, sl] = jnp.concatenate(
            [jnp.where(low, zj, jnp.where(aug_even, even, 0.0)),
             jnp.where(low, jnp.where(aug_odd, odd, 0.0), zj)], axis=1).astype(out_ref.dtype)

    def project(kind, c):
        if kind == "v":
            return lax.dot_general(wvt_ref[c * nc:(c + 1) * nc, :], h, (((1,), (1,)), ((), ())),
                                   preferred_element_type=F32)
        if kind == "o":
            return dd(h, wo_ref[:, c * nc:(c + 1) * nc])
        col = (0 if kind == "q" else d) + c * nc
        return dd(h, wqk_ref[:, col:col + nc])

    def finish(kind, c, z):
        if kind == "v":
            vt_ref[0, c * nc:(c + 1) * nc, :] = z.astype(vt_ref.dtype)
        elif kind == "o":
            o_ref[0, :, c * nc:(c + 1) * nc] = jax.nn.sigmoid(z).astype(o_ref.dtype)
        else:
            out_ref, g_ref = (q_ref, qg_ref) if kind == "q" else (k_ref, kg_ref)
            for pr in range(nc // SUM_W):
                zn = head_rms(z[:, pr * SUM_W:(pr + 1) * SUM_W], g_ref[...])
                for j in range(SUM_W // LANES):
                    tile = (c * nc + pr * SUM_W) // LANES + j
                    store_augmented(out_ref, zn[:, j * LANES:(j + 1) * LANES], tile, aug_all[kind])

    jobs = [(kind, c) for kind in ("o", "q", "k", "v") for c in range(d // nc)]
    f_stages = forget_gate_stages()
    z_next = project(*jobs[0])
    for i, job in enumerate(jobs):
        z = z_next
        if i + 1 < len(jobs):
            z_next = project(*jobs[i + 1])
        if job[0] in ("q", "k"):
            for _ in f_stages:
                pass
        finish(*job, z)
        next(f_stages, None)
        next(f_stages, None)


def _fox_in(x, mod, w_qk, w_o, w_vt, w_f, b_f, q_g, k_g, tm=512, nc=512):
    b, t, d = x.shape
    n_heads = d // HEAD_DIM
    assert N_SPLIT * n_heads <= LANES
    qg = jnp.tile(q_g * (HEAD_DIM ** -0.5 * LOG2E), SUM_W // HEAD_DIM).reshape(1, SUM_W)
    kg = jnp.tile(k_g, SUM_W // HEAD_DIM).reshape(1, SUM_W)
    logit_bound = HEAD_DIM * jnp.max(jnp.abs(qg)) * jnp.max(jnp.abs(kg))
    bd = _head_sum_matrix()
    tri = (jnp.arange(tm)[:, None] >= jnp.arange(tm)[None, :]).astype(BF16)
    assert AUG_STRIDE * n_heads <= LANES and 2 * N_SPLIT + 1 <= AUG_STRIDE
    pq = np.zeros((LANES, LANES), np.float32)
    pk = np.zeros_like(pq)
    oq = np.zeros((1, LANES), np.float32)
    ok = np.zeros_like(oq)
    shift_lane = np.zeros_like(oq)
    for hd in range(n_heads):
        base = hd * AUG_STRIDE
        for part in range(N_SPLIT):
            pq[part * n_heads + hd, base + N_SPLIT + part] = 1.0
            ok[0, base + N_SPLIT + part] = 1.0
            pk[part * n_heads + hd, base + part] = -1.0
            oq[0, base + part] = 1.0
        shift_lane[0, base + 2 * N_SPLIT] = 1.0
    pq = jnp.asarray(pq, BF16)
    pk = jnp.asarray(pk, BF16)
    oq = jnp.asarray(oq) - logit_bound * jnp.asarray(shift_lane)
    ok = jnp.asarray(ok + shift_lane)
    act = jax.ShapeDtypeStruct((b, t, d), BF16)
    aug = jax.ShapeDtypeStruct((b, t, n_heads * LANES), BF16)
    row_spec = pl.BlockSpec((1, tm, d), lambda i, j: (i, j, 0))
    aug_spec = pl.BlockSpec((1, tm, n_heads * LANES), lambda i, j: (i, j, 0))
    kern = functools.partial(_fox_in_kernel, d=d, nc=nc)
    consts = (w_qk, w_o, w_f, b_f, qg, kg, bd, tri, pq, pk, oq, ok, w_vt)
    q_aug, k_aug, og, vt = pl.pallas_call(
        kern,
        grid=(b, t // tm),
        in_specs=[row_spec, pl.BlockSpec((1, 6, d), lambda i, j: (i, 0, 0))]
        + [_const_spec(c.shape) for c in consts],
        out_specs=[aug_spec, aug_spec, row_spec, pl.BlockSpec((1, d, tm), lambda i, j: (i, 0, j))],
        out_shape=[aug, aug, act, jax.ShapeDtypeStruct((b, d, t), BF16)],
        scratch_shapes=[pltpu.VMEM((8, LANES), F32)],
        compiler_params=_params("parallel", "arbitrary"),
        name="fox_in_proj",
    )(x, mod, *consts)
    return q_aug, k_aug, og, vt, logit_bound


def _attn_kernel(q_ref, k_ref, vt_ref, g_ref, y_ref, *, tq, tk):
    qi = pl.program_id(2)
    heads = LANES // HEAD_DIM
    pair = 2 * tk
    key = lax.broadcasted_iota(jnp.int32, (tk, tq), 0)
    qry = lax.broadcasted_iota(jnp.int32, (tk, tq), 1) + qi * tq
    qs = [q_ref[0, :, hh * LANES:(hh + 1) * LANES] for hh in range(heads)]

    def step(j, carry, masked):
        starts = [pl.multiple_of(j * pair + bb * tk, tk) for bb in range(2)]

        def scores(hh, bb):
            z = _dot_nt(k_ref[0, pl.ds(starts[bb], tk), hh * LANES:(hh + 1) * LANES], qs[hh])
            if masked:
                z = jnp.where(key + starts[bb] <= qry, z, NEG_BIG)
            return z

        def weights(z, m):
            p = jnp.exp2(z - m)
            return p, jnp.sum(p, axis=0, keepdims=True)

        def values(hh, bb, p):
            vt = vt_ref[0, hh * HEAD_DIM:(hh + 1) * HEAD_DIM, pl.ds(starts[bb], tk)]
            return jnp.dot(vt, p.astype(BF16), preferred_element_type=F32)

        colmax = lambda z: jnp.max(z, axis=0, keepdims=True)
        (m0, l0, a0), (m1, l1, a1) = carry
        z00 = scores(0, 0)
        z10 = scores(1, 0)
        m0a = jnp.maximum(m0, colmax(z00))
        z01 = scores(0, 1)
        p00, s00 = weights(z00, m0a)
        m1a = jnp.maximum(m1, colmax(z10))
        z11 = scores(1, 1)
        v00 = values(0, 0, p00)
        p10, s10 = weights(z10, m1a)
        m0b = jnp.maximum(m0a, colmax(z01))
        v10 = values(1, 0, p10)
        p01, s01 = weights(z01, m0b)
        m1b = jnp.maximum(m1a, colmax(z11))
        v01 = values(0, 1, p01)
        p11, s11 = weights(z11, m1b)
        v11 = values(1, 1, p11)

        def merge(m, l, a, ma, mb, sa, sb, va, vb):
            ra = jnp.exp2(m - ma)
            rb = jnp.exp2(ma - mb)
            return mb, rb * (ra * l + sa) + sb, rb * (ra * a + va) + vb

        return (merge(m0, l0, a0, m0a, m0b, s00, s01, v00, v01),
                merge(m1, l1, a1, m1a, m1b, s10, s11, v10, v11))

    init = tuple((jnp.full((1, tq), NEG_BIG, F32), jnp.zeros((1, tq), F32),
                  jnp.zeros((HEAD_DIM, tq), F32)) for _ in range(heads))
    n_full = (qi * tq) // pair
    carry = lax.fori_loop(0, n_full, functools.partial(step, masked=False), init)
    for jm in range(tq // pair):
        carry = step(n_full + jm, carry, True)
    yt = jnp.concatenate([carry[hh][2] / carry[hh][1] for hh in range(heads)], axis=0)
    y_ref[0] = (yt.T * g_ref[0].astype(F32)).astype(y_ref.dtype)


def _attn_bounded_kernel(q_ref, k_ref, vt_ref, g_ref, y_ref, *, tq, q_parts=2, step_blocks=(2, 1)):
    qi = pl.program_id(2)
    heads = LANES // HEAD_DIM
    part = tq // q_parts
    slots = [(hh, pi) for hh in range(heads) for pi in range(q_parts)]

    def run_chains(chains, carry):
        def scores(hh, k0, nk, pi, q0, nq, masked):
            row = pi * part + q0
            z = _dot_nt(k_ref[0, pl.ds(k0, nk), hh * LANES:(hh + 1) * LANES],
                        q_ref[0, row:row + nq, hh * LANES:(hh + 1) * LANES])
            if masked:
                key = lax.broadcasted_iota(jnp.int32, (nk, nq), 0) + k0
                qry = lax.broadcasted_iota(jnp.int32, (nk, nq), 1) + (qi * tq + row)
                z = jnp.where(key <= qry, z, NEG_BIG)
            return z

        def total(s):
            l, acc = carry[slots.index(s)]
            if sums[s]:
                l = l + functools.reduce(jnp.add, sums[s])
                acc = acc + functools.reduce(jnp.add, vals[s])
            return l, acc

        def write_part(pi):
            done_parts = [total((hh, pi)) for hh in range(heads)]
            yt = jnp.concatenate([acc / l for l, acc in done_parts], axis=0)
            rows = slice(pi * part, (pi + 1) * part)
            y_ref[0, rows, :] = (yt.T * g_ref[0, rows, :].astype(F32)).astype(y_ref.dtype)

        sums = {s: [] for s in slots}
        vals = {s: [] for s in slots}
        work = [ch for ch in chains if not isinstance(ch, int)]
        ahead = [scores(*ch) for ch in work[:LOOKAHEAD]]
        c = -1
        for ch in chains:
            if isinstance(ch, int):
                write_part(ch)
                continue
            c += 1
            hh, k0, nk, pi, q0, nq, _ = ch
            z = ahead.pop(0)
            if c + LOOKAHEAD < len(work):
                ahead.append(scores(*work[c + LOOKAHEAD]))
            p = jnp.exp2(z)
            vt = vt_ref[0, hh * HEAD_DIM:(hh + 1) * HEAD_DIM, pl.ds(k0, nk)]
            s = jnp.sum(p, axis=0, keepdims=True)
            v = jnp.dot(vt, p.astype(BF16), preferred_element_type=F32)
            if nq != part:
                s = jnp.concatenate([jnp.zeros((1, q0), F32), s], axis=1)
                v = jnp.concatenate([jnp.zeros((HEAD_DIM, q0), F32), v], axis=1)
            sums[(hh, pi)].append(s)
            vals[(hh, pi)].append(v)
        return tuple(total(s) for s in slots)

    def full_step(j, carry, nb, base):
        starts = [pl.multiple_of(base + j * (nb * part) + bb * part, part) for bb in range(nb)]
        return run_chains([(hh, starts[bb], part, pi, 0, part, False)
                           for bb in range(nb) for (hh, pi) in slots], carry)

    carry = tuple((jnp.zeros((1, part), F32), jnp.zeros((HEAD_DIM, part), F32)) for _ in slots)
    done = 0
    for nb in step_blocks:
        n_steps = (qi * tq - done) // (nb * part)
        carry = lax.fori_loop(0, n_steps, functools.partial(full_step, nb=nb, base=done), carry)
        done = done + n_steps * (nb * part)
    half = part // 2
    chains = []
    for kb in range(q_parts):
        k0 = pl.multiple_of(qi * tq + kb * part, part)
        for hh in range(heads):
            chains.append((hh, k0, half, kb, 0, part, True))
            chains.append((hh, pl.multiple_of(k0 + half, half), half, kb, half, half, True))
            chains += [(hh, k0, part, pi, 0, part, False) for pi in range(kb + 1, q_parts)]
        chains.append(kb)
    run_chains(chains, carry)


def _fox_attention(q_aug, k_aug, vt, og, logit_bound, tq=1024, q_parts=2):
    part = tq // q_parts
    return lax.cond(
        2.0 * logit_bound <= SAFE_LOGIT_RANGE,
        functools.partial(_attention_call, tq=tq, name="fox_attention_bounded",
                          kernel_fn=functools.partial(_attn_bounded_kernel, tq=tq, q_parts=q_parts)),
        functools.partial(_attention_call, tq=part, name="fox_attention",
                          kernel_fn=functools.partial(_attn_kernel, tq=part, tk=part // 2)),
        q_aug, k_aug, vt, og)


def _attention_call(q_aug, k_aug, vt, og, *, kernel_fn, tq, name):
    b, d, t = vt.shape
    heads = LANES // HEAD_DIM
    blk = pl.BlockSpec((1, tq, LANES), lambda i, p, j: (i, j, p))
    q_blk = pl.BlockSpec((1, tq, heads * LANES), lambda i, p, j: (i, j, p))
    k_full = pl.BlockSpec((1, t, heads * LANES), lambda i, p, j: (i, 0, p))
    vt_full = pl.BlockSpec((1, LANES, t), lambda i, p, j: (i, p, 0))
    return pl.pallas_call(
        kernel_fn,
        grid=(b, d // LANES, t // tq),
        in_specs=[q_blk, k_full, vt_full, blk],
        out_specs=blk,
        out_shape=jax.ShapeDtypeStruct((b, t, d), BF16),
        compiler_params=_params("parallel", "parallel", "arbitrary"),
        name=name,
    )(q_aug, k_aug, vt, og)


def _proj_ln_kernel(a_ref, x_ref, mod_ref, w_ref, lng_ref, lnb_ref, o_ref):
    y = jnp.dot(a_ref[0], w_ref[...], preferred_element_type=F32)
    z = DEEPNORM_ALPHA * x_ref[0] + mod_ref[0][2:3] * y
    o_ref[0] = _layer_norm(z, lng_ref[...], lnb_ref[...])


def _proj_ln(a, x, mod, w, ln_g, ln_b, tm=512):
    b, t, d = x.shape
    row_spec = pl.BlockSpec((1, tm, d), lambda i, j: (i, j, 0))
    return pl.pallas_call(
        _proj_ln_kernel,
        grid=(b, t // tm),
        in_specs=[row_spec, row_spec,
                  pl.BlockSpec((1, 6, d), lambda i, j: (i, 0, 0)),
                  _const_spec(w.shape), _const_spec((1, d)), _const_spec((1, d))],
        out_specs=row_spec,
        out_shape=jax.ShapeDtypeStruct((b, t, d), F32),
        compiler_params=_params("parallel", "parallel"),
        name="attn_out_proj_ln",
    )(a, x, mod, w, ln_g.reshape(1, d), ln_b.reshape(1, d))


def _ffn_kernel(x_ref, mod_ref, win_ref, wout_ref, lng_ref, lnb_ref, o_ref, act_ref, *, d_ff, fc):
    x = x_ref[0]
    mod = mod_ref[0]
    h = (x * (1.0 + mod[4:5]) + mod[3:4]).astype(BF16)
    for c in range(d_ff // fc):
        g = jnp.dot(h, win_ref[:, c * fc:(c + 1) * fc], preferred_element_type=F32)
        u = jnp.dot(h, win_ref[:, d_ff + c * fc:d_ff + (c + 1) * fc], preferred_element_type=F32)
        act_ref[:, c * fc:(c + 1) * fc] = (g * jax.nn.sigmoid(g) * u).astype(BF16)
    y = jnp.dot(act_ref[...], wout_ref[...], preferred_element_type=F32)
    z = DEEPNORM_ALPHA * x + mod[5:6] * y
    o_ref[0] = _layer_norm(z, lng_ref[...], lnb_ref[...])


def _ffn(x, mod, w_in, w_out, layer, ln_g, ln_b, tm=512, fc=256):
    b, t, d = x.shape
    d_ff = w_out.shape[1]
    row_spec = pl.BlockSpec((1, tm, d), lambda i, j: (i, j, 0))

    def layer_spec(w):
        return pl.BlockSpec((None,) + w.shape[1:], lambda i, j: (layer, 0, 0),
                            pipeline_mode=pl.Buffered(1))

    return pl.pallas_call(
        functools.partial(_ffn_kernel, d_ff=d_ff, fc=fc),
        grid=(b, t // tm),
        in_specs=[row_spec,
                  pl.BlockSpec((1, 6, d), lambda i, j: (i, 0, 0)),
                  layer_spec(w_in), layer_spec(w_out),
                  _const_spec((1, d)), _const_spec((1, d))],
        out_specs=row_spec,
        out_shape=jax.ShapeDtypeStruct((b, t, d), F32),
        scratch_shapes=[pltpu.VMEM((tm, d_ff), BF16)],
        compiler_params=_params("parallel", "parallel"),
        name="swiglu_ln",
    )(x, mod, w_in, w_out, ln_g.reshape(1, d), ln_b.reshape(1, d))


def _rwkv_in_kernel(x_ref, xp_ref, mod_ref, mu_ref, wrkv_ref, w1_ref, w2_ref, a1_ref, a2_ref,
                    g1_ref, g2_ref, vec_ref, bd_ref,
                    r_ref, lw_ref, k_ref, v_ref, kk_ref, a_ref, g_ref, *, d):
    t = pl.program_id(1)
    mod = mod_ref[0]
    sc = 1.0 + mod[1:2]
    sh = mod[0:1]
    h = x_ref[0] * sc + sh
    tm = h.shape[0]
    prev = xp_ref[0][7:8, :] * sc + sh
    prev = jnp.where(t == 0, jnp.zeros_like(prev), prev)
    rows = lax.broadcasted_iota(jnp.int32, h.shape, 0)
    hprev = jnp.where(rows == 0, prev, pltpu.roll(h, 1, axis=0))
    dx = hprev - h
    mu = mu_ref[...]
    vec = vec_ref[...]
    w0, a0, k_k, k_a = vec[0:1], vec[1:2], vec[2:3], vec[3:4]

    def mix(n):
        return (h + dx * mu[n:n + 1]).astype(BF16)

    dd = functools.partial(jnp.dot, preferred_element_type=F32)
    t_w = dd(mix(3), w1_ref[...])
    t_a = dd(mix(4), a1_ref[...])
    t_g = dd(mix(5), g1_ref[...])
    k = dd(mix(1), wrkv_ref[1])
    a = jax.nn.sigmoid(a0 + dd(t_a.astype(BF16), a2_ref[...]))
    a_ref[0] = a.astype(a_ref.dtype)
    r = dd(mix(0), wrkv_ref[0])
    kk = k * k_k
    k_ref[0] = (k * (1.0 + (a - 1.0) * k_a)).astype(k_ref.dtype)
    bd = bd_ref[...]
    for j in range(d // SUM_W):
        sl = slice(j * SUM_W, (j + 1) * SUM_W)
        kkj = kk[:, sl]
        ss = _head_sums(kkj * kkj, bd)
        kk_ref[0, :, sl] = (kkj * jnp.minimum(lax.rsqrt(ss), 1e12)).astype(kk_ref.dtype)
    v = dd(mix(2), wrkv_ref[2])
    r_ref[0] = r.astype(r_ref.dtype)
    ww = w0 + dd(jnp.tanh(t_w).astype(BF16), w2_ref[...])
    lw_ref[0] = DECAY_SCALE * jax.nn.sigmoid(ww)
    g_ref[0] = dd(jax.nn.sigmoid(t_g).astype(BF16), g2_ref[...]).astype(g_ref.dtype)
    v_ref[0] = v.astype(v_ref.dtype)


def _rwkv_in(x, mod, mu, w_rkv, w1, w2, a1, a2, g1, g2, vec, tm=512):
    b, t, d = x.shape
    bd = _head_sum_matrix()
    row_spec = pl.BlockSpec((1, tm, d), lambda i, j: (i, j, 0))
    prev_spec = pl.BlockSpec((1, 8, d), lambda i, j: (i, jnp.maximum(j * (tm // 8) - 1, 0), 0))
    act = jax.ShapeDtypeStruct((b, t, d), BF16)
    consts = (mu, w_rkv, w1, w2, a1, a2, g1, g2, vec, bd)
    return pl.pallas_call(
        functools.partial(_rwkv_in_kernel, d=d),
        grid=(b, t // tm),
        in_specs=[row_spec, prev_spec, pl.BlockSpec((1, 6, d), lambda i, j: (i, 0, 0))]
        + [_const_spec(c.shape) for c in consts],
        out_specs=[row_spec] * 7,
        out_shape=[act, jax.ShapeDtypeStruct((b, t, d), F32), act, act, act, act, act],
        compiler_params=_params("parallel", "parallel"),
        name="rwkv_in_proj",
    )(x, x, mod, *consts)


def _rwkv_scan_kernel(r_ref, lw_ref, k_ref, v_ref, kk_ref, a_ref, lev_ref, tri_ref,
                      y_ref, h_ref, yi_ref, rq_ref, m_ref, n_ref, *, groups, steps_per_seq):
    s = pl.program_id(0)
    gr = 2 * CHUNK
    st = 2 * gr
    n_lev = CHUNK.bit_length() - 1

    @pl.when(s == 0)
    def _():
        h_ref[...] = jnp.zeros_like(h_ref)
        yi_ref[...] = jnp.zeros_like(yi_ref)
        rq_ref[...] = jnp.zeros_like(rq_ref)
        m_ref[...] = jnp.zeros_like(m_ref)
        n_ref[...] = jnp.zeros_like(n_ref)

    first_of_seq = lax.rem(jnp.maximum(s - 1, 0), steps_per_seq) == 0
    state = [jnp.where(first_of_seq, 0.0, h_ref[...])]
    pending = list(range(2 * groups))

    def state_steps(count):
        for _ in range(count):
            if pending:
                i = pending.pop(0)
                lo = i * CHUNK
                h = state[0]
                y_ref[0, lo:lo + CHUNK, :] = (yi_ref[lo:lo + CHUNK, :]
                                              + _dot(rq_ref[lo:lo + CHUNK, :], h))
                state[0] = _dot3(m_ref[i], h) + n_ref[i]

    lane = lax.broadcasted_iota(jnp.int32, (gr, LANES), 1)
    head0 = lane < HEAD_DIM
    lev = lev_ref[...]
    tri = tri_ref[...]
    strict = lev >= 0
    srow = lax.broadcasted_iota(jnp.int32, (st, st), 0)
    scol = lax.broadcasted_iota(jnp.int32, (st, st), 1)
    eye_st = srow == scol
    incl = strict | eye_st
    hrow = lax.broadcasted_iota(jnp.int32, (LANES, LANES), 0)
    hcol = lax.broadcasted_iota(jnp.int32, (LANES, LANES), 1)
    same_head = (hrow < HEAD_DIM) == (hcol < HEAD_DIM)
    eye_h = hrow == hcol

    def stack(x):
        zero = jnp.zeros_like(x)
        return jnp.concatenate([jnp.where(head0, x, zero), jnp.where(head0, zero, x)], axis=0)

    def unstack(x):
        return x[:gr] + x[gr:]

    def log_decay_cumsum(g):
        hi, mid, lo = _split3(lw_ref[0, g * gr:(g + 1) * gr, :])
        dd = functools.partial(jnp.dot, preferred_element_type=F32)
        return dd(tri, hi) + (dd(tri, mid) + dd(tri, lo))

    def prep(g, cum):
        rs = slice(g * gr, (g + 1) * gr)
        lw = lw_ref[0, rs, :]
        clast = jnp.concatenate(
            [jnp.broadcast_to(cum[(j + 1) * CHUNK - 1:(j + 1) * CHUNK], (CHUNK, LANES))
             for j in range(2)], axis=0)
        g_inv = jnp.exp(-cum)
        g_rem = jnp.exp(clast - cum)
        r = r_ref[0, rs, :].astype(F32)
        k = k_ref[0, rs, :].astype(F32)
        v = v_ref[0, rs, :].astype(F32)
        kk = kk_ref[0, rs, :].astype(F32)
        bt = kk * a_ref[0, rs, :].astype(F32)
        a_st = stack(-kk * jnp.exp(cum - lw))
        r_st = stack(r * jnp.exp(cum))
        b_st = stack(bt * g_inv)
        k_st = stack(k * g_inv)
        prods = []
        for hd in range(2):
            hs = slice(hd * gr, (hd + 1) * gr)
            prods.append(_dot_nt(jnp.concatenate([a_st[hs], r_st[hs]], axis=0),
                                 jnp.concatenate([b_st[hs], k_st[hs]], axis=0)))
        zero = jnp.zeros((gr, gr), F32)

        def by_head(rows, cols):
            return jnp.concatenate(
                [jnp.concatenate([prods[0][rows, cols], zero], axis=1),
                 jnp.concatenate([zero, prods[1][rows, cols]], axis=1)], axis=0)

        top, bot = slice(0, gr), slice(gr, st)
        return dict(
            v=v, v_st=stack(v), a_st=a_st, r_st=r_st, b_rem=bt * g_rem, k_rem=k * g_rem,
            g_last=jnp.exp(clast),
            a_ab=jnp.where(strict, by_head(top, top), 0.0),
            a_ak=jnp.where(strict, by_head(top, bot), 0.0),
            a_rb=jnp.where(incl, by_head(bot, top), 0.0),
            a_rk=jnp.where(incl, by_head(bot, bot), 0.0))

    def lower_rows(m, bsz):
        return jnp.concatenate([m[i + bsz:i + 2 * bsz] for i in range(0, st, 2 * bsz)], axis=0)

    def scatter_lower(full, low, bsz):
        parts = []
        for n, i in enumerate(range(0, st, 2 * bsz)):
            parts += [full[i:i + bsz], low[n * bsz:(n + 1) * bsz]]
        return jnp.concatenate(parts, axis=0)

    def precompute(gs):
        cums = {g: log_decay_cumsum(g) for g in gs}
        yield
        ps = {g: prep(g, cums[g]) for g in gs}
        yield
        xs = {g: jnp.where(eye_st, 1.0, jnp.where(lev == 0, ps[g]["a_ab"], 0.0)) for g in gs}
        for level in range(1, n_lev):
            bsz = 1 << level
            a_off = {g: jnp.where(lev == level, ps[g]["a_ab"], 0.0) for g in gs}
            if bsz % 8:
                ws = {g: _dot(a_off[g], xs[g]) for g in gs}
                yield
                xs = {g: xs[g] + _dot(xs[g], ws[g]) for g in gs}
            else:
                w_low = {g: _dot(lower_rows(a_off[g], bsz), xs[g]) for g in gs}
                yield
                zero = jnp.zeros((st, st), F32)
                upd = {g: _dot(lower_rows(xs[g], bsz), scatter_lower(zero, w_low[g], bsz))
                       for g in gs}
                xs = {g: scatter_lower(xs[g], lower_rows(xs[g], bsz) + upd[g], bsz) for g in gs}
            yield
        gm = {g: jnp.concatenate([_dot(ps[g]["a_ak"], ps[g]["v_st"]), ps[g]["a_st"]], axis=1)
              for g in gs}
        yield
        tg = {g: _dot(xs[g], gm[g]) for g in gs}
        yield
        rb = {g: _dot(ps[g]["a_rb"], tg[g]) for g in gs}
        rk = {g: _dot(ps[g]["a_rk"], ps[g]["v_st"]) for g in gs}
        yield
        assert not pending
        for g in gs:
            p = ps[g]
            yi_ref[g * gr:(g + 1) * gr, :] = unstack(rb[g][:, :LANES] + rk[g])
            rq_ref[g * gr:(g + 1) * gr, :] = unstack(p["r_st"] + rb[g][:, LANES:])
            uv = unstack(tg[g][:, :LANES])
            wa = unstack(tg[g][:, LANES:])
            for j in range(2):
                cs = slice(j * CHUNK, (j + 1) * CHUNK)
                lhs_t = jnp.concatenate([p["b_rem"][cs], p["k_rem"][cs]], axis=0)
                rhs_t = jnp.concatenate(
                    [jnp.concatenate([uv[cs], wa[cs]], axis=1),
                     jnp.concatenate([p["v"][cs], jnp.zeros_like(p["v"][cs])], axis=1)], axis=0)
                nm = _dot_tn(lhs_t, rhs_t)
                n_ref[2 * g + j] = jnp.where(same_head, nm[:, :LANES], 0.0)
                decay = jnp.broadcast_to(p["g_last"][j * CHUNK:j * CHUNK + 1], (LANES, LANES))
                m_ref[2 * g + j] = (jnp.where(same_head, nm[:, LANES:], 0.0)
                                    + jnp.where(eye_h, decay, 0.0))

    n_stages = 2 * n_lev + 3
    for stage, _ in enumerate(precompute(range(groups))):
        state_steps(-(-len(pending) // (n_stages - stage)))
        if not pending and state:
            h_ref[...] = state.pop()


def _rwkv_scan(r, lw, k, v, kk, a, groups=2):
    b, t, d = r.shape
    gr = 2 * CHUNK
    st = 2 * gr
    rows = gr * groups
    n_hp = d // LANES
    steps_per_seq = t // rows
    n_blocks = b * n_hp * steps_per_seq

    def block_index(blk):
        seq = blk // steps_per_seq
        return seq // n_hp, blk % steps_per_seq, seq % n_hp

    in_blk = pl.BlockSpec((1, rows, LANES), lambda s: block_index(jnp.minimum(s, n_blocks - 1)))
    out_blk = pl.BlockSpec((1, rows, LANES), lambda s: block_index(jnp.maximum(s - 1, 0)))
    idx = jnp.arange(st)
    xor = idx[:, None] ^ idx[None, :]
    same = (idx[:, None] // CHUNK) == (idx[None, :] // CHUNK)
    lower = idx[None, :] < idx[:, None]
    msb = jnp.floor(jnp.log2(jnp.maximum(xor, 1).astype(F32))).astype(jnp.int32)
    lev = jnp.where(same & lower, msb, -1).astype(jnp.int32)
    ti = jnp.arange(gr)
    tri = ((ti[:, None] >= ti[None, :]) & ((ti[:, None] // CHUNK) == (ti[None, :] // CHUNK))).astype(BF16)
    return pl.pallas_call(
        functools.partial(_rwkv_scan_kernel, groups=groups, steps_per_seq=steps_per_seq),
        grid=(n_blocks + 1,),
        in_specs=[in_blk] * 6 + [_const_spec(lev.shape), _const_spec(tri.shape)],
        out_specs=out_blk,
        out_shape=jax.ShapeDtypeStruct((b, t, d), F32),
        scratch_shapes=[pltpu.VMEM((LANES, LANES), F32),
                        pltpu.VMEM((rows, LANES), F32), pltpu.VMEM((rows, LANES), F32),
                        pltpu.VMEM((2 * groups, LANES, LANES), F32),
                        pltpu.VMEM((2 * groups, LANES, LANES), F32)],
        compiler_params=_params("arbitrary"),
        name="rwkv7_chunk_scan",
    )(r, lw, k, v, kk, a, lev, tri)


def _rwkv_out_kernel(y_ref, r_ref, k_ref, v_ref, g_ref, x_ref, mod_ref, vec_ref, bd_ref, w_ref,
                     lng_ref, lnb_ref, o_ref, act_ref, *, d):
    bd = bd_ref[...]
    vec = vec_ref[...]
    inv = 1.0 / HEAD_DIM
    for j in range(d // SUM_W):
        sl = slice(j * SUM_W, (j + 1) * SUM_W)
        y = y_ref[0, :, sl]
        mean = _dot_exact_rhs(y, bd) * inv
        yc = y - mean
        var = _head_sums(yc * yc, bd) * inv
        yn = yc * lax.rsqrt(var + GN_EPS) * vec[0:1, sl] + vec[1:2, sl]
        r = r_ref[0, :, sl].astype(F32)
        k = k_ref[0, :, sl].astype(F32)
        bonus = _dot_exact_rhs(r * k * vec[2:3, sl], bd) * v_ref[0, :, sl].astype(F32)
        act_ref[:, sl] = ((yn + bonus) * g_ref[0, :, sl].astype(F32)).astype(BF16)
    out = jnp.dot(act_ref[...], w_ref[...], preferred_element_type=F32)
    z = DEEPNORM_ALPHA * x_ref[0] + mod_ref[0][2:3] * out
    o_ref[0] = _layer_norm(z, lng_ref[...], lnb_ref[...])


def _rwkv_out(y, r, k, v, g, x, mod, vec, w, ln_g, ln_b, tm=512):
    b, t, d = x.shape
    bd = _head_sum_matrix()
    row_spec = pl.BlockSpec((1, tm, d), lambda i, j: (i, j, 0))
    return pl.pallas_call(
        functools.partial(_rwkv_out_kernel, d=d),
        grid=(b, t // tm),
        in_specs=[row_spec] * 6
        + [pl.BlockSpec((1, 6, d), lambda i, j: (i, 0, 0)),
           _const_spec(vec.shape), _const_spec(bd.shape), _const_spec(w.shape),
           _const_spec((1, d)), _const_spec((1, d))],
        out_specs=row_spec,
        out_shape=jax.ShapeDtypeStruct((b, t, d), F32),
        scratch_shapes=[pltpu.VMEM((tm, d), BF16)],
        compiler_params=_params("parallel", "parallel"),
        name="rwkv_out_proj_ln",
    )(y, r, k, v, g, x, mod, vec, bd, w, ln_g.reshape(1, d), ln_b.reshape(1, d))


def _trunk(x, c, ada_w, ada_b, ln_g, ln_b, ffn_w_in, ffn_w_out, fox_w_in, fox_b_f, fox_q_g, fox_k_g, fox_w_o, rwkv_mu, rwkv_w_rkv, rwkv_w0, rwkv_w1, rwkv_w2, rwkv_a0, rwkv_a1, rwkv_a2, rwkv_g1, rwkv_g2, rwkv_k_k, rwkv_k_a, rwkv_r_k, rwkv_lnx_g, rwkv_lnx_b, rwkv_w_o,
           *, tm, tm_rwkv, tq, q_parts, groups, tn, nc, fc):
    b, t, d = x.shape
    n_heads = d // HEAD_DIM
    mods = _mods(c, ada_w, ada_b, tn=tn)
    ffn_in = ffn_w_in.astype(BF16)
    ffn_out = ffn_w_out.astype(BF16)

    w_in = fox_w_in[0]
    f_lo = 3 * d
    w_qk = w_in[:, :2 * d].astype(BF16)
    w_o = w_in[:, f_lo + n_heads:].astype(BF16)
    w_vt = w_in[:, 2 * d:f_lo].T.astype(BF16)
    w_f = jnp.pad(w_in[:, f_lo:f_lo + n_heads], ((0, 0), (0, LANES - n_heads))).astype(BF16)
    b_f = jnp.pad(fox_b_f[0], (0, LANES - n_heads)).reshape(1, LANES)
    q, k, og, v, logit_bound = _fox_in(x, mods[0], w_qk, w_o, w_vt, w_f, b_f, fox_q_g[0], fox_k_g[0],
                                       tm=tm, nc=nc)
    att = _fox_attention(q, k, v, og, logit_bound, tq=tq, q_parts=q_parts)
    x = _proj_ln(att, x, mods[0], fox_w_o[0].astype(BF16), ln_g[0, 0], ln_b[0, 0], tm=tm)
    x = _ffn(x, mods[0], ffn_in, ffn_out, 0, ln_g[0, 1], ln_b[0, 1], tm=tm, fc=fc)

    vec_in = jnp.stack([rwkv_w0[0], rwkv_a0[0], rwkv_k_k[0], rwkv_k_a[0]])
    r, lw, k, v, kk, a, g = _rwkv_in(
        x, mods[1], rwkv_mu[0], rwkv_w_rkv[0].astype(BF16),
        rwkv_w1[0].astype(BF16), rwkv_w2[0].astype(BF16),
        rwkv_a1[0].astype(BF16), rwkv_a2[0].astype(BF16),
        rwkv_g1[0].astype(BF16), rwkv_g2[0].astype(BF16), vec_in, tm=tm_rwkv)
    y = _rwkv_scan(r, lw, k, v, kk, a, groups=groups)
    vec_out = jnp.stack([rwkv_lnx_g[0], rwkv_lnx_b[0], rwkv_r_k[0].reshape(d)])
    x = _rwkv_out(y, r, k, v, g, x, mods[1], vec_out, rwkv_w_o[0].astype(BF16), ln_g[1, 0], ln_b[1, 0],
                  tm=tm)
    x = _ffn(x, mods[1], ffn_in, ffn_out, 1, ln_g[1, 1], ln_b[1, 1], tm=tm, fc=fc)
    return x


def kernel(x, c, ada_w, ada_b, ln_g, ln_b, ffn_w_in, ffn_w_out, fox_w_in, fox_b_f, fox_q_g, fox_k_g, fox_w_o, rwkv_mu, rwkv_w_rkv, rwkv_w0, rwkv_w1, rwkv_w2, rwkv_a0, rwkv_a1, rwkv_a2, rwkv_g1, rwkv_g2, rwkv_k_k, rwkv_k_a, rwkv_r_k, rwkv_lnx_g, rwkv_lnx_b, rwkv_w_o):
    return _trunk(x, c, ada_w, ada_b, ln_g, ln_b, ffn_w_in, ffn_w_out, fox_w_in, fox_b_f, fox_q_g, fox_k_g, fox_w_o, rwkv_mu, rwkv_w_rkv, rwkv_w0, rwkv_w1, rwkv_w2, rwkv_a0, rwkv_a1, rwkv_a2, rwkv_g1, rwkv_g2, rwkv_k_k, rwkv_k_a, rwkv_r_k, rwkv_lnx_g, rwkv_lnx_b, rwkv_w_o,
                  tm=512, tm_rwkv=512, tq=4096, q_parts=8, groups=8, tn=1536, nc=256, fc=256)
```
